```python
import jax
import jax.numpy as jnp
from jax import lax
import numpy as np

D_MODEL = 1024
BATCH = 32
SEQ = 256
DEPTH = 1
DEC_BATCH = 2
DEC_SEQ = 2048
PAST_LEN = 256

GRID_W = 64
MIX_WIDTH = D_MODEL
CONV_WIDTH = MIX_WIDTH // 2
RET_WIDTH = MIX_WIDTH - CONV_WIDTH
RET_HEADS = 4
RET_DK = RET_WIDTH // RET_HEADS
RET_DV = RET_WIDTH // RET_HEADS
RET_CHUNK = 128
ROPE_FREQS = RET_DK // 4
ROPE_BASE = 10000.0
CONV_K = 3
IN_COLS = 3 * CONV_WIDTH + 4 * RET_WIDTH
N_EXPERTS = 256
N_GROUPS = 8
TOPK_GROUPS = 4
TOP_K = 8
EXPERT_FF = D_MODEL // 4
SHARED_FF = D_MODEL // 4
ROUTED_SCALE = 2.5
MOE_BLOCK = 64
EPS = 1e-6

kernel_name = 'hybrid_conv_retention_moe_diffusion_step'


def _rmsnorm(x, g):
    x32 = x.astype(jnp.float32)
    y = x32 * lax.rsqrt(jnp.mean(x32 * x32, axis=-1, keepdims=True) + EPS) * g.astype(jnp.float32)
    return y.astype(x.dtype)


def _conv3(z, w, b, axis):
    L = z.shape[axis]
    pad = [(0, 0)] * z.ndim
    pad[axis] = (1, 1)
    zp = jnp.pad(z, pad)
    left = lax.slice_in_dim(zp, 0, L, axis=axis)
    mid = lax.slice_in_dim(zp, 1, L + 1, axis=axis)
    right = lax.slice_in_dim(zp, 2, L + 2, axis=axis)
    return left * w[0] + mid * w[1] + right * w[2] + b


def _rope_2d(x, rows):
    row = jnp.repeat(jnp.arange(rows, dtype=jnp.float32), GRID_W)
    col = jnp.tile(jnp.arange(GRID_W, dtype=jnp.float32), rows)
    inv = ROPE_BASE ** (-jnp.arange(ROPE_FREQS, dtype=jnp.float32) / ROPE_FREQS)

    def rot(xh, pos):
        ang = pos[:, None] * inv[None, :]
        cos = jnp.cos(ang)[None, :, None, :]
        sin = jnp.sin(ang)[None, :, None, :]
        x1, x2 = jnp.split(xh, 2, axis=-1)
        return jnp.concatenate([x1 * cos - x2 * sin, x1 * sin + x2 * cos], axis=-1)

    half = x.shape[-1] // 2
    return jnp.concatenate([rot(x[..., :half], row), rot(x[..., half:], col)], axis=-1)


def _retention_chunked(q, k, v, log_gamma, s0):
    B, T, H, _ = q.shape
    n = T // RET_CHUNK

    def chunks(a):
        return a.reshape(B, n, RET_CHUNK, H, a.shape[-1]).transpose(1, 0, 3, 2, 4)

    pos = jnp.arange(RET_CHUNK, dtype=jnp.float32)
    diff = pos[:, None] - pos[None, :]
    lower = diff >= 0
    intra = jnp.where(lower[None], jnp.exp(jnp.where(lower, diff, 0.0)[None] * log_gamma[:, None, None]), 0.0)
    q_dec = jnp.exp((pos + 1.0)[None, :] * log_gamma[:, None])
    k_dec = jnp.exp((RET_CHUNK - 1.0 - pos)[None, :] * log_gamma[:, None])
    c_dec = jnp.exp(RET_CHUNK * log_gamma)

    def step(S, qkv):
        qc, kc, vc = qkv
        att = jnp.einsum('bhqd,bhkd->bhqk', qc, kc) * intra
        o = jnp.einsum('bhqk,bhkv->bhqv', att, vc) + jnp.einsum('bhqd,bhdv->bhqv', qc * q_dec[:, :, None], S)
        S = S * c_dec[:, None, None] + jnp.einsum('bhkd,bhkv->bhdv', kc * k_dec[:, :, None], vc)
        return S, o

    S, o = lax.scan(step, s0, (chunks(q), chunks(k), chunks(v)))
    o = o.transpose(1, 0, 3, 2, 4).reshape(B, T, H, v.shape[-1])
    return o, S


def _token_mixer(h, w_in, conv_w, conv_b, decay_fwd, decay_bwd, gn_g, w_out, s0_f, s0_b, rows):
    B, T, _ = h.shape
    p = jnp.einsum('btd,de->bte', h, w_in)
    cw, rw = CONV_WIDTH, RET_WIDTH
    p_b, p_c, p_x, p_q, p_k, p_v, p_g = jnp.split(
        p, [cw, 2 * cw, 3 * cw, 3 * cw + rw, 3 * cw + 2 * rw, 3 * cw + 3 * rw], axis=-1)

    z = p_c * p_x
    if rows is None:
        zc = _conv3(z, conv_w, conv_b, 1)
    else:
        zc = _conv3(z.reshape(B, rows, GRID_W, cw), conv_w, conv_b, 2).reshape(B, T, cw)
    y_conv = p_b * zc

    q = p_q.astype(jnp.float32).reshape(B, T, RET_HEADS, RET_DK)
    k = p_k.astype(jnp.float32).reshape(B, T, RET_HEADS, RET_DK)
    v = p_v.astype(jnp.float32).reshape(B, T, RET_HEADS, RET_DV)
    if rows is not None:
        q = _rope_2d(q, rows)
        k = _rope_2d(k, rows)
    q = q * (RET_DK ** -0.5)
    lg_f = jax.nn.log_sigmoid(decay_fwd.astype(jnp.float32))
    lg_b = jax.nn.log_sigmoid(decay_bwd.astype(jnp.float32))
    o_f, s_f = _retention_chunked(q, k, v, lg_f, s0_f.astype(jnp.float32))
    o_b, s_b = _retention_chunked(jnp.flip(q, 1), jnp.flip(k, 1), jnp.flip(v, 1), lg_b, s0_b.astype(jnp.float32))
    o = o_f + jnp.flip(o_b, 1)
    mu = jnp.mean(o, axis=-1, keepdims=True)
    var = jnp.mean(jnp.square(o - mu), axis=-1, keepdims=True)
    o = ((o - mu) * lax.rsqrt(var + EPS)).reshape(B, T, rw) * gn_g.astype(jnp.float32)
    y_ret = (jax.nn.silu(p_g.astype(jnp.float32)) * o).astype(h.dtype)

    u = jnp.einsum('bte,ed->btd', jnp.concatenate([y_conv, y_ret], axis=-1), w_out)
    return u, s_f, s_b


def _moe(h, router_w, router_bias, w_gate, w_up, w_down, sw_gate, sw_up, sw_down):
    N, D = h.shape
    scores = jax.nn.sigmoid(jnp.einsum('nd,de->ne', h, router_w).astype(jnp.float32))
    biased = scores + router_bias.astype(jnp.float32)
    per_group = N_EXPERTS // N_GROUPS
    group_score = lax.top_k(biased.reshape(N, N_GROUPS, per_group), 2)[0].sum(-1)
    _, top_groups = lax.top_k(group_score, TOPK_GROUPS)
    group_mask = jnp.any(top_groups[:, :, None] == jnp.arange(N_GROUPS)[None, None, :], axis=1)
    masked = jnp.where(jnp.repeat(group_mask, per_group, axis=-1), biased, -jnp.inf)
    _, topi = lax.top_k(masked, TOP_K)
    topw = jnp.take_along_axis(scores, topi, axis=-1)
    topw = topw / jnp.sum(topw, axis=-1, keepdims=True) * ROUTED_SCALE

    A = N * TOP_K
    P = ((A + MOE_BLOCK - 1) // MOE_BLOCK) * MOE_BLOCK + N_EXPERTS * MOE_BLOCK
    n_blk = P // MOE_BLOCK
    expert_flat = topi.reshape(-1).astype(jnp.int32)
    token_flat = jnp.repeat(jnp.arange(N, dtype=jnp.int32), TOP_K)
    order = jnp.argsort(expert_flat)
    exp_sorted = expert_flat[order]
    counts = jnp.bincount(expert_flat, length=N_EXPERTS).astype(jnp.int32)
    start = jnp.cumsum(counts) - counts
    padded = ((counts + MOE_BLOCK - 1) // MOE_BLOCK) * MOE_BLOCK
    padded_end = jnp.cumsum(padded)
    padded_start = padded_end - padded
    dest = padded_start[exp_sorted] + (jnp.arange(A, dtype=jnp.int32) - start[exp_sorted])
    slot_token = jnp.full((P,), N, dtype=jnp.int32).at[dest].set(token_flat[order])
    slot_w = jnp.zeros((P,), jnp.float32).at[dest].set(topw.reshape(-1)[order])
    block_start = jnp.arange(n_blk, dtype=jnp.int32) * MOE_BLOCK
    block_expert = jnp.minimum(jnp.searchsorted(padded_end, block_start, side='right'), N_EXPERTS - 1).astype(jnp.int32)
    h_pad = jnp.concatenate([h, jnp.zeros((1, D), h.dtype)], axis=0)

    def expert_block(args):
        tok, e = args
        xb = h_pad[tok]
        a = xb @ w_gate[e]
        b = xb @ w_up[e]
        return (jax.nn.silu(a) * b) @ w_down[e]

    outs = lax.map(expert_block, (slot_token.reshape(n_blk, MOE_BLOCK), block_expert))
    routed = jnp.zeros((N + 1, D), jnp.float32).at[slot_token].add(outs.reshape(P, D) * slot_w[:, None])[:N]
    shared = (jax.nn.silu(h @ sw_gate) * (h @ sw_up)) @ sw_down
    return (routed + shared).astype(h.dtype)


def setup_inputs(seed: int = 0) -> dict:
    key = jax.random.key(seed)
    ks = jax.random.split(key, 32)
    f32 = jnp.float32

    def nrm(k, shape, scale):
        return jax.random.normal(k, shape, f32) * scale

    base_decay = jnp.log(2.0 ** (5.0 + jnp.arange(RET_HEADS, dtype=f32)) - 1.0)
    return {
        'x_prompt': nrm(ks[0], (BATCH, SEQ, D_MODEL), 1.0),
        'x_sample': nrm(ks[1], (DEC_BATCH, DEC_SEQ, D_MODEL), 1.0),
        'state_ret_fwd': nrm(ks[2], (DEC_BATCH, DEPTH, RET_HEADS, RET_DK, RET_DV), 1.0),
        'state_ret_bwd': nrm(ks[3], (DEC_BATCH, DEPTH, RET_HEADS, RET_DK, RET_DV), 1.0),
        'c': nrm(ks[4], (DEC_BATCH, D_MODEL), 1.0),
        'c_ctx': nrm(ks[5], (D_MODEL,), 0.5),
        'w_mod': nrm(ks[6], (DEPTH, D_MODEL, 6 * D_MODEL), 0.5 * D_MODEL ** -0.5),
        'b_mod': nrm(ks[7], (DEPTH, 6 * D_MODEL), 0.02),
        'norm_mix_pre': 1.0 + nrm(ks[8], (DEPTH, D_MODEL), 0.05),
        'norm_mix_post': 1.0 + nrm(ks[9], (DEPTH, D_MODEL), 0.05),
        'norm_ffn_pre': 1.0 + nrm(ks[10], (DEPTH, D_MODEL), 0.05),
        'norm_ffn_post': 1.0 + nrm(ks[11], (DEPTH, D_MODEL), 0.05),
        'w_in': nrm(ks[12], (DEPTH, D_MODEL, IN_COLS), D_MODEL ** -0.5),
        'conv_w': nrm(ks[13], (DEPTH, CONV_K, CONV_WIDTH), CONV_K ** -0.5),
        'conv_b': nrm(ks[14], (DEPTH, CONV_WIDTH), 0.02),
        'ret_decay_fwd': base_decay[None, :] + nrm(ks[15], (DEPTH, RET_HEADS), 0.05),
        'ret_decay_bwd': base_decay[None, :] + nrm(ks[16], (DEPTH, RET_HEADS), 0.05),
        'ret_gn_g': 1.0 + nrm(ks[17], (DEPTH, RET_WIDTH), 0.05),
        'w_out': nrm(ks[18], (DEPTH, MIX_WIDTH, D_MODEL), MIX_WIDTH ** -0.5),
        'router_w': nrm(ks[19], (DEPTH, D_MODEL, N_EXPERTS), D_MODEL ** -0.5),
        'router_bias': nrm(ks[20], (DEPTH, N_EXPERTS), 0.01),
        'expert_w_gate': nrm(ks[21], (DEPTH, N_EXPERTS, D_MODEL, EXPERT_FF), D_MODEL ** -0.5),
        'expert_w_up': nrm(ks[22], (DEPTH, N_EXPERTS, D_MODEL, EXPERT_FF), D_MODEL ** -0.5),
        'expert_w_down': nrm(ks[23], (DEPTH, N_EXPERTS, EXPERT_FF, D_MODEL), EXPERT_FF ** -0.5),
        'shared_w_gate': nrm(ks[24], (DEPTH, D_MODEL, SHARED_FF), D_MODEL ** -0.5),
        'shared_w_up': nrm(ks[25], (DEPTH, D_MODEL, SHARED_FF), D_MODEL ** -0.5),
        'shared_w_down': nrm(ks[26], (DEPTH, SHARED_FF, D_MODEL), SHARED_FF ** -0.5),
    }


def reference(x_prompt, x_sample, state_ret_fwd, state_ret_bwd, c, c_ctx, w_mod, b_mod,
              norm_mix_pre, norm_mix_post, norm_ffn_pre, norm_ffn_post, w_in, conv_w, conv_b,
              ret_decay_fwd, ret_decay_bwd, ret_gn_g, w_out, router_w, router_bias,
              expert_w_gate, expert_w_up, expert_w_down, shared_w_gate, shared_w_up, shared_w_down):
    xp = x_prompt
    xs = x_sample
    Bp, Tp, D = xp.shape
    Bs, Ts, _ = xs.shape
    ROWS = Ts // GRID_W
    zero_state = jnp.zeros((Bp, RET_HEADS, RET_DK, RET_DV), jnp.float32)
    new_f, new_b = [], []
    for l in range(DEPTH):
        mod_ctx = jax.nn.silu(c_ctx) @ w_mod[l] + b_mod[l]
        mod_lat = (jax.nn.silu(c) @ w_mod[l] + b_mod[l])[:, None, :]
        sh_mc, sc_mc, g_mc, sh_fc, sc_fc, g_fc = jnp.split(mod_ctx, 6, axis=-1)
        sh_ml, sc_ml, g_ml, sh_fl, sc_fl, g_fl = jnp.split(mod_lat, 6, axis=-1)

        h = _rmsnorm(xp, norm_mix_pre[l]) * (1.0 + sc_mc) + sh_mc
        u, s_f, s_b = _token_mixer(h, w_in[l], conv_w[l], conv_b[l], ret_decay_fwd[l], ret_decay_bwd[l],
                                   ret_gn_g[l], w_out[l], zero_state, zero_state, None)
        xp = xp + g_mc * _rmsnorm(u, norm_mix_post[l])
        new_f.append(s_f)
        new_b.append(s_b)

        h = _rmsnorm(xs, norm_mix_pre[l]) * (1.0 + sc_ml) + sh_ml
        u, _, _ = _token_mixer(h, w_in[l], conv_w[l], conv_b[l], ret_decay_fwd[l], ret_decay_bwd[l],
                               ret_gn_g[l], w_out[l], state_ret_fwd[:, l], state_ret_bwd[:, l], ROWS)
        xs = xs + g_ml * _rmsnorm(u, norm_mix_post[l])

        hp = _rmsnorm(xp, norm_ffn_pre[l]) * (1.0 + sc_fc) + sh_fc
        hs = _rmsnorm(xs, norm_ffn_pre[l]) * (1.0 + sc_fl) + sh_fl
        tokens = jnp.concatenate([hp.reshape(Bp * Tp, D), hs.reshape(Bs * Ts, D).astype(hp.dtype)], axis=0)
        f = _moe(tokens, router_w[l], router_bias[l], expert_w_gate[l], expert_w_up[l], expert_w_down[l],
                 shared_w_gate[l], shared_w_up[l], shared_w_down[l])
        xp = xp + g_fc * _rmsnorm(f[:Bp * Tp].reshape(Bp, Tp, D), norm_ffn_post[l])
        xs = xs + g_fl * _rmsnorm(f[Bp * Tp:].reshape(Bs, Ts, D), norm_ffn_post[l])

    new_state_ret_fwd = jnp.stack(new_f, axis=1)
    new_state_ret_bwd = jnp.stack(new_b, axis=1)
    return (xp, xs, new_state_ret_fwd, new_state_ret_bwd)
```

```python
import functools

import jax
import jax.numpy as jnp
from jax import lax
from jax.experimental import pallas as pl
from jax.experimental.pallas import tpu as pltpu

F32 = jnp.float32
BF16 = jnp.bfloat16
I32 = jnp.int32

D_MODEL = 1024
CONV_W = 512
RET_W = 512
HEADS = 4
DK = 128
CHUNK = 128
GRID_W = 64
ROPE_FREQS = 32
ROPE_BASE = 10000.0
IN_COLS = 3 * CONV_W + 4 * RET_W
N_EXPERTS = 256
N_GROUPS = 8
GROUP_SIZE = N_EXPERTS // N_GROUPS
TOPK_GROUPS = 4
TOP_K = 8
FF = 256
ROUTED_SCALE = 2.5
EPS = 1e-6

TM = 256
SUB = 128
TT = 128
RT = 512
RB = 2048
VMEM_LIMIT = 56 * 1024 * 1024


def _cparams(n_axes=1, vmem=VMEM_LIMIT):
    return pltpu.CompilerParams(dimension_semantics=("arbitrary",) * n_axes,
                                vmem_limit_bytes=vmem)


def _silu(x):
    return x * jax.nn.sigmoid(x)


def _log_sigmoid(x):
    return jnp.minimum(x, 0.0) - jnp.log1p(jnp.exp(-jnp.abs(x)))


def _rms(x, g):
    return x * lax.rsqrt(jnp.mean(x * x, axis=-1, keepdims=True) + EPS) * g


def _dot(a, b):
    return jnp.dot(a, b, preferred_element_type=F32)


def _mod_body(c_ref, w_ref, b_ref, o_ref):
    s = _silu(c_ref[...]).astype(BF16)
    o_ref[...] = _dot(s, w_ref[...].astype(BF16)) + b_ref[...]


def _modulation(c_rows, w_mod, b_mod):
    n_col = w_mod.shape[1]
    blk = 1536
    return pl.pallas_call(
        _mod_body,
        out_shape=jax.ShapeDtypeStruct((8, n_col), F32),
        grid=(n_col // blk,),
        in_specs=[pl.BlockSpec((8, D_MODEL), lambda i: (0, 0)),
                  pl.BlockSpec((D_MODEL, blk), lambda i: (0, i)),
                  pl.BlockSpec((1, blk), lambda i: (0, i))],
        out_specs=pl.BlockSpec((8, blk), lambda i: (0, i)),
        compiler_params=_cparams(),
        name="mod",
    )(c_rows, w_mod, b_mod)


class _Tiles:
    def __init__(self, n_ctx_seq, n_lat_seq, lat_len):
        self.n_ctx = n_ctx_seq
        self.lat_tiles = lat_len // TM
        self.n_lat_seq = n_lat_seq
        self.n_tiles = n_ctx_seq + n_lat_seq * self.lat_tiles

    def is_ctx(self, i):
        return i < self.n_ctx

    def lat_pos(self, i):
        j = jnp.maximum(i - self.n_ctx, 0)
        return j // self.lat_tiles, j % self.lat_tiles

    def phys_reversed(self, i):
        b, t = self.lat_pos(i)
        return jnp.where(i < self.n_ctx, i, self.n_ctx + b * self.lat_tiles + (self.lat_tiles - 1 - t))

    def mod_row(self, i):
        b, _ = self.lat_pos(i)
        return jnp.where(i < self.n_ctx, 0, 1 + b)


def _rope(x, cos, sin_signed):
    lane = lax.broadcasted_iota(I32, x.shape, 1)
    partner = jnp.where((lane & 63) < 32, pltpu.roll(x, 96, 1), pltpu.roll(x, 32, 1))
    return x * cos + partner * sin_signed


def _mix_a_body(tiles, xp_ref, xs_ref, mod_ref, gpre_ref, win_ref, cw_ref, cb_ref, dec_ref,
                cos_ref, sin_ref, s0b_ref,
                yconv_ref, q_ref, v_ref, g_ref, kt_ref, sbin_ref, sbfin_ref,
                sb_scr, tab_scr):
    i = pl.program_id(0)
    is_ctx = tiles.is_ctx(i)
    _, t_rev = tiles.lat_pos(i)
    first = jnp.logical_or(is_ctx, t_rev == 0)

    @pl.when(i == 0)
    def _():
        lg = _log_sigmoid(dec_ref[1])
        col = lax.broadcasted_iota(I32, lg.shape, 2).astype(F32)
        tab_scr[0] = jnp.exp(col * lg)
        tab_scr[1] = jnp.exp(float(CHUNK) * lg)

    @pl.when(first)
    def _():
        sb_scr[...] = jnp.where(is_ctx, 0.0, s0b_ref[0])

    x = jnp.where(is_ctx, xp_ref[...], xs_ref[...])
    h = (_rms(x, gpre_ref[...]) * (1.0 + mod_ref[0, 1:2, :]) + mod_ref[0, 0:1, :]).astype(BF16)

    def proj(k):
        return _dot(h, win_ref[:, k * 512:(k + 1) * 512])

    z = proj(1) * proj(2)
    row = lax.broadcasted_iota(I32, z.shape, 0)
    period = jnp.where(is_ctx, TM, GRID_W)
    pos = row & (period - 1)
    left = jnp.where(pos == 0, 0.0, pltpu.roll(z, 1, 0))
    right = jnp.where(pos == period - 1, 0.0, pltpu.roll(z, TM - 1, 0))
    zc = left * cw_ref[0:1, :] + z * cw_ref[1:2, :] + right * cw_ref[2:3, :] + cb_ref[...]
    yconv_ref[...] = (proj(0) * zc).astype(BF16)

    cos = cos_ref[0]
    sin = sin_ref[0]
    q = proj(3)
    k = proj(4)
    q = jnp.concatenate([_rope(q[:, hh * DK:(hh + 1) * DK], cos, sin) for hh in range(HEADS)], axis=1)
    k = jnp.concatenate([_rope(k[:, hh * DK:(hh + 1) * DK], cos, sin) for hh in range(HEADS)], axis=1)
    q_ref[...] = (q * (DK ** -0.5)).astype(BF16)
    kt = k.T
    kt_ref[...] = kt.astype(BF16)
    v = proj(5).astype(BF16)
    v_ref[...] = v
    g_ref[...] = proj(6)

    for c in (1, 0):
        for hh in range(HEADS):
            sbin_ref[c, hh] = sb_scr[hh].astype(BF16)
            kts = (kt[hh * DK:(hh + 1) * DK, c * CHUNK:(c + 1) * CHUNK] * tab_scr[0, hh]).astype(BF16)
            vc = v[c * CHUNK:(c + 1) * CHUNK, hh * DK:(hh + 1) * DK]
            sb_scr[hh] = sb_scr[hh] * tab_scr[1, hh] + _dot(kts, vc)

    @pl.when(is_ctx)
    def _():
        sbfin_ref[0] = sb_scr[...]


def _mix_b_body(tiles, xp_ref, xs_ref, mod_ref, q_ref, kt_ref, v_ref, g_ref, yconv_ref, sbin_ref,
                wout_ref, gpost_ref, gffn_ref, gn_ref, dec_ref, s0f_ref,
                xnew_ref, tok_ref, sffin_ref,
                sf_scr, tab_scr, ycat_scr):
    i = pl.program_id(0)
    is_ctx = tiles.is_ctx(i)
    _, t_pos = tiles.lat_pos(i)
    first = jnp.logical_or(is_ctx, t_pos == 0)

    @pl.when(i == 0)
    def _():
        lgf = _log_sigmoid(dec_ref[0])
        lgb = _log_sigmoid(dec_ref[1])
        row = lax.broadcasted_iota(I32, lgf.shape, 1)
        col = lax.broadcasted_iota(I32, lgf.shape, 2)
        d = (row - col).astype(F32)
        tab_scr[0] = (jnp.where(row >= col, jnp.exp(jnp.where(row >= col, d, 0.0) * lgf), 0.0)
                      + jnp.where(col >= row, jnp.exp(jnp.where(col >= row, -d, 0.0) * lgb), 0.0))
        tab_scr[1] = jnp.exp((row + 1).astype(F32) * lgf)
        tab_scr[2] = jnp.exp((CHUNK - row).astype(F32) * lgb)
        tab_scr[3] = jnp.exp((CHUNK - 1 - col).astype(F32) * lgf)
        tab_scr[4] = jnp.exp(float(CHUNK) * lgf)

    @pl.when(first)
    def _():
        sf_scr[...] = jnp.where(is_ctx, 0.0, s0f_ref[0])

    for c in range(TM // CHUNK):
        rows = slice(c * CHUNK, (c + 1) * CHUNK)
        for hh in range(HEADS):
            cols = slice(hh * DK, (hh + 1) * DK)
            qc = q_ref[rows, cols]
            ktc = kt_ref[cols, rows]
            vc = v_ref[rows, cols]
            att = (_dot(qc, ktc) * tab_scr[0, hh]).astype(BF16)
            o = (_dot(att, vc)
                 + tab_scr[1, hh] * _dot(qc, sf_scr[hh].astype(BF16))
                 + tab_scr[2, hh] * _dot(qc, sbin_ref[c, hh]))
            kts = (ktc.astype(F32) * tab_scr[3, hh]).astype(BF16)
            sf_scr[hh] = sf_scr[hh] * tab_scr[4, hh] + _dot(kts, vc)
            mu = jnp.mean(o, axis=-1, keepdims=True)
            dev = o - mu
            var = jnp.mean(dev * dev, axis=-1, keepdims=True)
            on = dev * lax.rsqrt(var + EPS) * gn_ref[:, cols]
            ycat_scr[rows, RET_W + hh * DK:RET_W + (hh + 1) * DK] = (_silu(g_ref[rows, cols]) * on).astype(BF16)
    ycat_scr[:, 0:CONV_W] = yconv_ref[...]

    @pl.when(is_ctx)
    def _():
        sffin_ref[0] = sf_scr[...]

    x = jnp.where(is_ctx, xp_ref[...], xs_ref[...])
    u = _dot(ycat_scr[...], wout_ref[...])
    xn = x + mod_ref[0, 2:3, :] * _rms(u, gpost_ref[...])
    xnew_ref[...] = xn
    tok_ref[...] = _rms(xn, gffn_ref[...]) * (1.0 + mod_ref[0, 4:5, :]) + mod_ref[0, 3:4, :]


def _token_mixer(tiles, xp2, xs2, mod3, g_pre, win_bf, conv_w, conv_b, dec, cos_t, sin_t,
                 s0f, s0b, wout_bf, g_post, g_ffn, gn_g):
    n_tok = tiles.n_tiles * TM
    n_ctx = tiles.n_ctx
    last_ctx = n_ctx - 1

    def full(shape):
        return pl.BlockSpec(shape, lambda i: (0,) * len(shape))

    def xp_spec(phys):
        return pl.BlockSpec((TM, D_MODEL), lambda i: (jnp.minimum(phys(i), last_ctx), 0))

    def xs_spec(phys):
        return pl.BlockSpec((TM, D_MODEL), lambda i: (jnp.maximum(phys(i) - n_ctx, 0), 0))

    mod_spec = pl.BlockSpec((1, 6, D_MODEL), lambda i: (tiles.mod_row(i), 0, 0))
    state_in = pl.BlockSpec((1, HEADS, DK, DK), lambda i: (tiles.lat_pos(i)[0], 0, 0, 0))
    state_out = pl.BlockSpec((1, HEADS, DK, DK), lambda i: (jnp.minimum(i, last_ctx), 0, 0, 0))

    rev = tiles.phys_reversed

    def rope_idx(i):
        _, t = tiles.lat_pos(i)
        return jnp.where(i < n_ctx, 0, 1 + (tiles.lat_tiles - 1 - t))

    rope_spec = pl.BlockSpec((1, TM, DK), lambda i: (rope_idx(i), 0, 0))

    def rows(width, phys):
        return pl.BlockSpec((TM, width), lambda i: (phys(i), 0))

    yconv, q, v, g, kt, sbin, sb_fin = pl.pallas_call(
        functools.partial(_mix_a_body, tiles),
        out_shape=(jax.ShapeDtypeStruct((n_tok, CONV_W), BF16),
                   jax.ShapeDtypeStruct((n_tok, RET_W), BF16),
                   jax.ShapeDtypeStruct((n_tok, RET_W), BF16),
                   jax.ShapeDtypeStruct((n_tok, RET_W), F32),
                   jax.ShapeDtypeStruct((RET_W, n_tok), BF16),
                   jax.ShapeDtypeStruct((n_tok // CHUNK, HEADS, DK, DK), BF16),
                   jax.ShapeDtypeStruct((n_ctx, HEADS, DK, DK), F32)),
        grid=(tiles.n_tiles,),
        in_specs=[xp_spec(rev), xs_spec(rev), mod_spec, full((1, D_MODEL)), full((D_MODEL, IN_COLS)),
                  full((3, CONV_W)), full((1, CONV_W)), full((2, HEADS, DK, DK)),
                  rope_spec, rope_spec, state_in],
        out_specs=(rows(CONV_W, rev), rows(RET_W, rev), rows(RET_W, rev), rows(RET_W, rev),
                   pl.BlockSpec((RET_W, TM), lambda i: (0, rev(i))),
                   pl.BlockSpec((TM // CHUNK, HEADS, DK, DK), lambda i: (rev(i), 0, 0, 0)),
                   state_out),
        scratch_shapes=[pltpu.VMEM((HEADS, DK, DK), F32), pltpu.VMEM((2, HEADS, DK, DK), F32)],
        compiler_params=_cparams(),
        name="mix_a",
    )(xp2, xs2, mod3, g_pre, win_bf, conv_w, conv_b, dec, cos_t, sin_t, s0b)

    ident = lambda i: i
    xnew, tok, sf_fin = pl.pallas_call(
        functools.partial(_mix_b_body, tiles),
        out_shape=(jax.ShapeDtypeStruct((n_tok, D_MODEL), F32),
                   jax.ShapeDtypeStruct((n_tok, D_MODEL), F32),
                   jax.ShapeDtypeStruct((n_ctx, HEADS, DK, DK), F32)),
        grid=(tiles.n_tiles,),
        in_specs=[xp_spec(ident), xs_spec(ident), mod_spec,
                  rows(RET_W, ident),
                  pl.BlockSpec((RET_W, TM), lambda i: (0, i)),
                  rows(RET_W, ident), rows(RET_W, ident), rows(CONV_W, ident),
                  pl.BlockSpec((TM // CHUNK, HEADS, DK, DK), lambda i: (i, 0, 0, 0)),
                  full((D_MODEL, D_MODEL)), full((1, D_MODEL)), full((1, D_MODEL)), full((1, RET_W)),
                  full((2, HEADS, DK, DK)), state_in],
        out_specs=(rows(D_MODEL, ident), rows(D_MODEL, ident), state_out),
        scratch_shapes=[pltpu.VMEM((HEADS, DK, DK), F32), pltpu.VMEM((5, HEADS, DK, DK), F32),
                        pltpu.VMEM((TM, D_MODEL), BF16)],
        compiler_params=_cparams(),
        name="mix_b",
    )(xp2, xs2, mod3, q, kt, v, g, yconv, sbin, wout_bf, g_post, g_ffn, gn_g, dec, s0f)
    return xnew, tok, sf_fin, sb_fin


def _route_body(tok_ref, rwt_ref, bias_ref, topi_ref, topw_ref):
    h = tok_ref[...].astype(BF16)
    logits = lax.dot_general(rwt_ref[...], h, (((1,), (1,)), ((), ())), preferred_element_type=F32)
    shape3 = (GROUP_SIZE, N_GROUPS, 128)
    member = lax.broadcasted_iota(I32, shape3, 0)
    group = lax.broadcasted_iota(I32, shape3, 1)
    expert = group * GROUP_SIZE + member
    group2 = lax.broadcasted_iota(I32, (N_GROUPS, 128), 0)
    neg = -jnp.inf
    for lb in range(RT // 128):
        scores = jax.nn.sigmoid(logits[:, lb * 128:(lb + 1) * 128]).reshape(shape3)
        biased = scores + bias_ref[...].reshape(shape3)
        m1 = jnp.max(biased, axis=0)
        first = jnp.min(jnp.where(biased == m1, member, GROUP_SIZE), axis=0)
        m2 = jnp.max(jnp.where(member == first, neg, biased), axis=0)
        gs = m1 + m2
        beaten = jnp.zeros(gs.shape, I32)
        for s in range(1, N_GROUPS):
            other = pltpu.roll(gs, s, 0)
            wins = (other > gs) | ((other == gs) & (group2 >= s))
            beaten = beaten + wins.astype(I32)
        keep = beaten < TOPK_GROUPS
        cand = jnp.where(keep, biased, neg)
        idx_rows, w_rows = [], []
        for _ in range(TOP_K):
            best = jnp.max(jnp.max(cand, axis=0), axis=0, keepdims=True)
            pick = jnp.min(jnp.min(jnp.where(cand == best, expert, N_EXPERTS), axis=0), axis=0, keepdims=True)
            hit = expert == pick
            w_rows.append(jnp.sum(jnp.sum(jnp.where(hit, scores, 0.0), axis=0), axis=0, keepdims=True))
            idx_rows.append(pick)
            cand = jnp.where(hit, neg, cand)
        w = jnp.concatenate(w_rows, axis=0)
        topi_ref[:, lb * 128:(lb + 1) * 128] = jnp.concatenate(idx_rows, axis=0)
        topw_ref[:, lb * 128:(lb + 1) * 128] = w / jnp.sum(w, axis=0, keepdims=True) * ROUTED_SCALE


def _route(tok, rwt_bf, bias_b):
    n_tok = tok.shape[0]
    return pl.pallas_call(
        _route_body,
        out_shape=(jax.ShapeDtypeStruct((TOP_K, n_tok), I32), jax.ShapeDtypeStruct((TOP_K, n_tok), F32)),
        grid=(n_tok // RT,),
        in_specs=[pl.BlockSpec((RT, D_MODEL), lambda i: (i, 0)),
                  pl.BlockSpec((N_EXPERTS, D_MODEL), lambda i: (0, 0)),
                  pl.BlockSpec((N_EXPERTS, 128), lambda i: (0, 0))],
        out_specs=(pl.BlockSpec((TOP_K, RT), lambda i: (0, i)), pl.BlockSpec((TOP_K, RT), lambda i: (0, i))),
        compiler_params=_cparams(),
        name="route",
    )(tok, rwt_bf, bias_b)


def _onehot(ids_row):
    e = lax.broadcasted_iota(I32, (N_EXPERTS, 256), 0)
    return e == ids_row


def _rank_body(topi_ref, rank_ref, counts_ref, run_scr):
    i = pl.program_id(0)

    @pl.when(i == 0)
    def _():
        run_scr[...] = jnp.zeros(run_scr.shape, F32)

    a0 = lax.broadcasted_iota(I32, (256, 256), 0)
    a1 = lax.broadcasted_iota(I32, (256, 256), 1)
    upper = (a0 <= a1).astype(BF16)
    ones = jnp.ones((256, 256), BF16)
    for k in range(TOP_K):
        for sb in range(RB // 256):
            lanes = slice(sb * 256, (sb + 1) * 256)
            oh = _onehot(topi_ref[k:k + 1, lanes])
            ohb = oh.astype(BF16)
            seen = _dot(ohb, upper) + run_scr[...]
            r = jnp.sum(jnp.where(oh, seen, 0.0), axis=0, keepdims=True) - 1.0
            rank_ref[k:k + 1, lanes] = r.astype(I32)
            run_scr[...] = run_scr[...] + _dot(ohb, ones)

    @pl.when(i == pl.num_programs(0) - 1)
    def _():
        counts_ref[...] = run_scr[:, 0:128]


def _dest_body(topi_ref, rank_ref, counts_ref, dest_ref, start_ref, start_scr):
    i = pl.program_id(0)

    @pl.when(i == 0)
    def _():
        nb = jnp.floor((counts_ref[...] + float(SUB - 1)) / float(SUB))
        hi = jnp.floor(nb / 16.0)
        lo = nb - hi * 16.0
        e0 = lax.broadcasted_iota(I32, (N_EXPERTS, N_EXPERTS), 0)
        e1 = lax.broadcasted_iota(I32, (N_EXPERTS, N_EXPERTS), 1)
        below = (e1 < e0).astype(BF16)
        first_blk = 16.0 * _dot(below, hi.astype(BF16)) + _dot(below, lo.astype(BF16))
        start_scr[...] = first_blk * float(SUB)
        start_ref[...] = start_scr[...]

    start = jnp.concatenate([start_scr[...], start_scr[...]], axis=1)
    for k in range(TOP_K):
        for sb in range(RB // 256):
            lanes = slice(sb * 256, (sb + 1) * 256)
            oh = _onehot(topi_ref[k:k + 1, lanes])
            base = jnp.sum(jnp.where(oh, start, 0.0), axis=0, keepdims=True)
            dest_ref[k:k + 1, lanes] = base.astype(I32) + rank_ref[k:k + 1, lanes]


def _dispatch_plan(topi):
    n_tok = topi.shape[1]
    blk = pl.BlockSpec((TOP_K, RB), lambda i: (0, i))
    whole = pl.BlockSpec((N_EXPERTS, 128), lambda i: (0, 0))
    rank, counts = pl.pallas_call(
        _rank_body,
        out_shape=(jax.ShapeDtypeStruct((TOP_K, n_tok), I32), jax.ShapeDtypeStruct((N_EXPERTS, 128), F32)),
        grid=(n_tok // RB,),
        in_specs=[blk],
        out_specs=(blk, whole),
        scratch_shapes=[pltpu.VMEM((N_EXPERTS, 256), F32)],
        compiler_params=_cparams(),
        name="rank",
    )(topi)
    dest, start = pl.pallas_call(
        _dest_body,
        out_shape=(jax.ShapeDtypeStruct((TOP_K, n_tok), I32), jax.ShapeDtypeStruct((N_EXPERTS, 128), F32)),
        grid=(n_tok // RB,),
        in_specs=[blk, blk, whole],
        out_specs=(blk, whole),
        scratch_shapes=[pltpu.VMEM((N_EXPERTS, 128), F32)],
        compiler_params=_cparams(),
        name="dest",
    )(topi, rank, counts)
    return dest, start, counts


def _gather_rows(src_hbm, idx_of, dst, sem, n_rows):
    def body(r, carry):
        pltpu.make_async_copy(src_hbm.at[pl.ds(idx_of(r), 1)], dst.at[pl.ds(r, 1)], sem).start()
        return carry
    lax.fori_loop(0, n_rows, body, 0, unroll=8)


def _experts_body(start_ref, nsub_ref, slot_tok_ref,
                  tok_hbm, wg_ref, wu_ref, wd_ref, y_hbm,
                  xbuf, ybuf, wg_bf, wu_bf, wd_bf, gsem, osem):
    e = pl.program_id(0)
    start = start_ref[e]
    nsub = nsub_ref[e]

    def gather(j, slot):
        base = start + j * SUB
        _gather_rows(tok_hbm, lambda r: slot_tok_ref[base + r], xbuf.at[slot], gsem.at[slot], SUB)

    def gather_wait(slot):
        pltpu.make_async_copy(tok_hbm.at[pl.ds(0, SUB)], xbuf.at[slot], gsem.at[slot]).wait()

    def out_copy(j, slot):
        row0 = pl.multiple_of(start + j * SUB, SUB)
        return pltpu.make_async_copy(ybuf.at[slot], y_hbm.at[pl.ds(row0, SUB)], osem.at[slot])

    @pl.when(nsub > 0)
    def _():
        gather(0, 0)

    wg_bf[...] = wg_ref[0].astype(BF16)
    wu_bf[...] = wu_ref[0].astype(BF16)
    wd_bf[...] = wd_ref[0].astype(BF16)

    def step(j, carry):
        slot = j & 1

        @pl.when(j + 1 < nsub)
        def _():
            gather(j + 1, 1 - slot)

        gather_wait(slot)

        @pl.when(j >= 2)
        def _():
            out_copy(j - 2, slot).wait()

        xb = xbuf[slot].astype(BF16)
        a = _dot(xb, wg_bf[...])
        b = _dot(xb, wu_bf[...])
        ybuf[slot] = _dot((_silu(a) * b).astype(BF16), wd_bf[...])
        out_copy(j, slot).start()
        return carry

    lax.fori_loop(0, nsub, step, 0)

    @pl.when(nsub >= 2)
    def _():
        out_copy(nsub - 2, nsub & 1).wait()

    @pl.when(nsub >= 1)
    def _():
        out_copy(nsub - 1, (nsub - 1) & 1).wait()

    @pl.when(e == pl.num_programs(0) - 1)
    def _():
        ybuf[0] = jnp.zeros((SUB, D_MODEL), F32)
        used = start // SUB + nsub

        def tail_copy(sb):
            return pltpu.make_async_copy(ybuf.at[0], y_hbm.at[pl.ds(pl.multiple_of(sb * SUB, SUB), SUB)],
                                         osem.at[0])

        lax.fori_loop(used, y_hbm.shape[0] // SUB, lambda sb, c: (tail_copy(sb).start(), c)[1], 0)
        lax.fori_loop(used, y_hbm.shape[0] // SUB, lambda sb, c: (tail_copy(sb).wait(), c)[1], 0)


def _experts(start, nsub, slot_tok, tok, wg, wu, wd, n_slots):
    return pl.pallas_call(
        _experts_body,
        out_shape=jax.ShapeDtypeStruct((n_slots, D_MODEL), F32),
        grid_spec=pltpu.PrefetchScalarGridSpec(
            num_scalar_prefetch=3,
            grid=(N_EXPERTS,),
            in_specs=[pl.BlockSpec(memory_space=pl.ANY),
                      pl.BlockSpec((1, D_MODEL, FF), lambda e, *_: (e, 0, 0)),
                      pl.BlockSpec((1, D_MODEL, FF), lambda e, *_: (e, 0, 0)),
                      pl.BlockSpec((1, FF, D_MODEL), lambda e, *_: (e, 0, 0))],
            out_specs=pl.BlockSpec(memory_space=pl.ANY),
            scratch_shapes=[pltpu.VMEM((2, SUB, D_MODEL), F32), pltpu.VMEM((2, SUB, D_MODEL), F32),
                            pltpu.VMEM((D_MODEL, FF), BF16), pltpu.VMEM((D_MODEL, FF), BF16),
                            pltpu.VMEM((FF, D_MODEL), BF16),
                            pltpu.SemaphoreType.DMA((2,)), pltpu.SemaphoreType.DMA((2,))]),
        compiler_params=_cparams(),
        name="experts",
    )(start, nsub, slot_tok, tok, wg, wu, wd)


def _combine_body(n_tok, n_ctx_tiles, dest_ref,
                  y_hbm, topw_ref, tok_ref, xnew_ref, mod_ref, sg_ref, su_ref, sd_ref, gpost_ref,
                  outp_ref, outs_ref, gbuf, gsem):
    i = pl.program_id(0)
    n = pl.num_programs(0)

    def gather(tile, slot):
        for k in range(TOP_K):
            base = k * n_tok + tile * TT
            _gather_rows(y_hbm, lambda r: dest_ref[base + r], gbuf.at[slot, k], gsem.at[slot], TT)

    def gather_wait(slot):
        for k in range(TOP_K):
            pltpu.make_async_copy(y_hbm.at[pl.ds(0, TT)], gbuf.at[slot, k], gsem.at[slot]).wait()

    slot = i & 1

    @pl.when(i == 0)
    def _():
        gather(0, 0)

    @pl.when(i + 1 < n)
    def _():
        gather(i + 1, 1 - slot)

    w_t = jnp.concatenate([topw_ref[...], jnp.zeros((128 - TOP_K, TT), F32)], axis=0).T
    h = tok_ref[...].astype(BF16)
    shared = _dot((_silu(_dot(h, sg_ref[...])) * _dot(h, su_ref[...])).astype(BF16), sd_ref[...])

    gather_wait(slot)
    routed = gbuf[slot, 0] * w_t[:, 0:1]
    for k in range(1, TOP_K):
        routed = routed + gbuf[slot, k] * w_t[:, k:k + 1]
    out = xnew_ref[...] + mod_ref[0, 5:6, :] * _rms(routed + shared, gpost_ref[...])

    @pl.when(i < n_ctx_tiles)
    def _():
        outp_ref[...] = out

    @pl.when(i >= n_ctx_tiles)
    def _():
        outs_ref[...] = out


def _combine(tiles, dest_flat, ysorted, topw, tok, xnew, mod3, sg_bf, su_bf, sd_bf, g_post):
    n_tok = tok.shape[0]
    n_ctx_tok = tiles.n_ctx * TM
    n_ctx_tiles = n_ctx_tok // TT
    lat_tiles_per_seq = tiles.lat_tiles * TM // TT

    def mod_row(i):
        return jnp.where(i < n_ctx_tiles, 0, 1 + jnp.maximum(i - n_ctx_tiles, 0) // lat_tiles_per_seq)

    def full(shape):
        return pl.BlockSpec(shape, lambda i, *_: (0,) * len(shape))

    rows = pl.BlockSpec((TT, D_MODEL), lambda i, *_: (i, 0))
    return pl.pallas_call(
        functools.partial(_combine_body, n_tok, n_ctx_tiles),
        out_shape=(jax.ShapeDtypeStruct((n_ctx_tok, D_MODEL), F32),
                   jax.ShapeDtypeStruct((n_tok - n_ctx_tok, D_MODEL), F32)),
        grid_spec=pltpu.PrefetchScalarGridSpec(
            num_scalar_prefetch=1,
            grid=(n_tok // TT,),
            in_specs=[pl.BlockSpec(memory_space=pl.ANY),
                      pl.BlockSpec((TOP_K, TT), lambda i, *_: (0, i)),
                      rows, rows,
                      pl.BlockSpec((1, 6, D_MODEL), lambda i, *_: (mod_row(i), 0, 0)),
                      full((D_MODEL, FF)), full((D_MODEL, FF)), full((FF, D_MODEL)), full((1, D_MODEL))],
            out_specs=(pl.BlockSpec((TT, D_MODEL), lambda i, *_: (jnp.minimum(i, n_ctx_tiles - 1), 0)),
                       pl.BlockSpec((TT, D_MODEL), lambda i, *_: (jnp.maximum(i - n_ctx_tiles, 0), 0))),
            scratch_shapes=[pltpu.VMEM((2, TOP_K, TT, D_MODEL), F32), pltpu.SemaphoreType.DMA((2,))]),
        compiler_params=_cparams(),
        name="combine",
    )(dest_flat, ysorted, topw, tok, xnew, mod3, sg_bf, su_bf, sd_bf, g_post)


def _rope_tables(lat_len, lat_tiles):
    rows = lat_len // GRID_W
    row = jnp.repeat(jnp.arange(rows, dtype=F32), GRID_W)
    col = jnp.tile(jnp.arange(GRID_W, dtype=F32), rows)
    inv = ROPE_BASE ** (-jnp.arange(ROPE_FREQS, dtype=F32) / ROPE_FREQS)
    ang = jnp.concatenate([row[:, None] * inv[None, :]] * 2 + [col[:, None] * inv[None, :]] * 2, axis=1)
    sign = jnp.tile(jnp.concatenate([-jnp.ones((ROPE_FREQS,), F32), jnp.ones((ROPE_FREQS,), F32)]), 2)
    cos = jnp.cos(ang).reshape(lat_tiles, TM, DK)
    sin = (jnp.sin(ang) * sign[None, :]).reshape(lat_tiles, TM, DK)
    cos = jnp.concatenate([jnp.ones((1, TM, DK), F32), cos], axis=0)
    sin = jnp.concatenate([jnp.zeros((1, TM, DK), F32), sin], axis=0)
    return cos, sin


def kernel(x_prompt, x_sample, state_ret_fwd, state_ret_bwd, c, c_ctx, w_mod, b_mod, norm_mix_pre,
           norm_mix_post, norm_ffn_pre, norm_ffn_post, w_in, conv_w, conv_b, ret_decay_fwd,
           ret_decay_bwd, ret_gn_g, w_out, router_w, router_bias, expert_w_gate, expert_w_up,
           expert_w_down, shared_w_gate, shared_w_up, shared_w_down):
    bp, tp, d = x_prompt.shape
    bs, ts, _ = x_sample.shape
    depth = w_mod.shape[0]
    assert d == D_MODEL and tp == TM and ts % TM == 0 and bs + 1 <= 8
    tiles = _Tiles(bp, bs, ts)
    n_tok = tiles.n_tiles * TM
    n_slots = n_tok * TOP_K + N_EXPERTS * SUB
    cos_t, sin_t = _rope_tables(ts, tiles.lat_tiles)

    perm = (jnp.arange(N_EXPERTS) % N_GROUPS) * GROUP_SIZE + jnp.arange(N_EXPERTS) // N_GROUPS

    xp2 = x_prompt.reshape(bp * tp, d)
    xs2 = x_sample.reshape(bs * ts, d)
    new_f, new_b = [], []
    for l in range(depth):
        c_rows = jnp.concatenate([c_ctx[None, :], c, jnp.zeros((8 - 1 - bs, d), F32)], axis=0)
        mod3 = _modulation(c_rows, w_mod[l], b_mod[l][None, :]).reshape(8, 6, d)
        dec = jnp.broadcast_to(jnp.stack([ret_decay_fwd[l], ret_decay_bwd[l]])[:, :, None, None],
                               (2, HEADS, DK, DK)).astype(F32)
        xnew, tok, sf_fin, sb_fin = _token_mixer(
            tiles, xp2, xs2, mod3, norm_mix_pre[l][None, :], w_in[l].astype(BF16), conv_w[l],
            conv_b[l][None, :], dec, cos_t, sin_t, state_ret_fwd[:, l], state_ret_bwd[:, l],
            w_out[l].astype(BF16), norm_mix_post[l][None, :], norm_ffn_pre[l][None, :], ret_gn_g[l][None, :])
        new_f.append(sf_fin)
        new_b.append(sb_fin)

        rwt = router_w[l].T[perm].astype(BF16)
        bias_b = jnp.broadcast_to(router_bias[l][perm][:, None], (N_EXPERTS, 128)).astype(F32)
        topi, topw = _route(tok, rwt, bias_b)
        dest, start, counts = _dispatch_plan(topi)
        dest_flat = dest.reshape(-1)
        tok_of = jnp.tile(jnp.arange(n_tok, dtype=I32), TOP_K)
        slot_tok = jnp.zeros((n_slots,), I32).at[dest_flat].set(tok_of)
        start_i = start[:, 0].astype(I32)
        nsub = ((counts[:, 0] + float(SUB - 1)) / float(SUB)).astype(I32)
        ysorted = _experts(start_i, nsub, slot_tok, tok, expert_w_gate[l], expert_w_up[l],
                           expert_w_down[l], n_slots)
        xp2, xs2 = _combine(tiles, dest_flat, ysorted, topw, tok, xnew, mod3,
                            shared_w_gate[l].astype(BF16), shared_w_up[l].astype(BF16),
                            shared_w_down[l].astype(BF16), norm_ffn_post[l][None, :])

    return (xp2.reshape(bp, tp, d), xs2.reshape(bs, ts, d),
            jnp.stack(new_f, axis=1), jnp.stack(new_b, axis=1))
```

```python
import functools

import jax
import jax.numpy as jnp
from jax import lax
from jax.experimental import pallas as pl
from jax.experimental.pallas import tpu as pltpu

F32 = jnp.float32
BF16 = jnp.bfloat16
I32 = jnp.int32

D_MODEL = 1024
CONV_W = 512
RET_W = 512
HEADS = 4
DK = 128
CHUNK = 128
GRID_W = 64
ROPE_FREQS = 32
ROPE_BASE = 10000.0
IN_COLS = 3 * CONV_W + 4 * RET_W
N_EXPERTS = 256
N_GROUPS = 8
GROUP_SIZE = N_EXPERTS // N_GROUPS
TOPK_GROUPS = 4
TOP_K = 8
FF = 256
ROUTED_SCALE = 2.5
EPS = 1e-6

TM = 256
SUB = 128
TT = 128
RT = 512
RB = 2048
VMEM_LIMIT = 56 * 1024 * 1024


def _cparams(n_axes=1, vmem=VMEM_LIMIT):
    return pltpu.CompilerParams(dimension_semantics=("arbitrary",) * n_axes,
                                vmem_limit_bytes=vmem)


def _silu(x):
    return x * jax.nn.sigmoid(x)


def _log_sigmoid(x):
    return jnp.minimum(x, 0.0) - jnp.log1p(jnp.exp(-jnp.abs(x)))


def _rms(x, g):
    return x * lax.rsqrt(jnp.mean(x * x, axis=-1, keepdims=True) + EPS) * g


def _dot(a, b):
    return jnp.dot(a, b, preferred_element_type=F32)


def _mod_body(c_ref, w_ref, b_ref, o_ref):
    s = _silu(c_ref[...]).astype(BF16)
    o_ref[...] = _dot(s, w_ref[...].astype(BF16)) + b_ref[...]


def _modulation(c_rows, w_mod, b_mod):
    n_col = w_mod.shape[1]
    blk = 1536
    return pl.pallas_call(
        _mod_body,
        out_shape=jax.ShapeDtypeStruct((8, n_col), F32),
        grid=(n_col // blk,),
        in_specs=[pl.BlockSpec((8, D_MODEL), lambda i: (0, 0)),
                  pl.BlockSpec((D_MODEL, blk), lambda i: (0, i)),
                  pl.BlockSpec((1, blk), lambda i: (0, i))],
        out_specs=pl.BlockSpec((8, blk), lambda i: (0, i)),
        compiler_params=_cparams(),
        name="mod",
    )(c_rows, w_mod, b_mod)


class _Tiles:
    def __init__(self, n_ctx_seq, n_lat_seq, lat_len):
        self.n_ctx = n_ctx_seq
        self.lat_tiles = lat_len // TM
        self.n_lat_seq = n_lat_seq
        self.n_tiles = n_ctx_seq + n_lat_seq * self.lat_tiles

    def is_ctx(self, i):
        return i < self.n_ctx

    def lat_pos(self, i):
        j = jnp.maximum(i - self.n_ctx, 0)
        return j // self.lat_tiles, j % self.lat_tiles

    def phys_reversed(self, i):
        b, t = self.lat_pos(i)
        return jnp.where(i < self.n_ctx, i, self.n_ctx + b * self.lat_tiles + (self.lat_tiles - 1 - t))

    def mod_row(self, i):
        b, _ = self.lat_pos(i)
        return jnp.where(i < self.n_ctx, 0, 1 + b)


def _rope(x, cos, sin_signed):
    lane = lax.broadcasted_iota(I32, x.shape, 1)
    partner = jnp.where((lane & 63) < 32, pltpu.roll(x, 96, 1), pltpu.roll(x, 32, 1))
    return x * cos + partner * sin_signed


def _mix_a_body(tiles, xp_ref, xs_ref, mod_ref, gpre_ref, win_ref, cw_ref, cb_ref, dec_ref,
                cos_ref, sin_ref, s0b_ref,
                yconv_ref, q_ref, v_ref, g_ref, kt_ref, sbin_ref, sbfin_ref,
                sb_scr, tab_scr):
    i = pl.program_id(0)
    is_ctx = tiles.is_ctx(i)
    _, t_rev = tiles.lat_pos(i)
    first = jnp.logical_or(is_ctx, t_rev == 0)

    @pl.when(i == 0)
    def _():
        lg = _log_sigmoid(dec_ref[1])
        col = lax.broadcasted_iota(I32, lg.shape, 2).astype(F32)
        tab_scr[0] = jnp.exp(col * lg)
        tab_scr[1] = jnp.exp(float(CHUNK) * lg)

    @pl.when(first)
    def _():
        sb_scr[...] = jnp.where(is_ctx, 0.0, s0b_ref[0])

    x = jnp.where(is_ctx, xp_ref[...], xs_ref[...])
    h = (_rms(x, gpre_ref[...]) * (1.0 + mod_ref[0, 1:2, :]) + mod_ref[0, 0:1, :]).astype(BF16)

    def proj(k):
        return _dot(h, win_ref[:, k * 512:(k + 1) * 512])

    z = proj(1) * proj(2)
    row = lax.broadcasted_iota(I32, z.shape, 0)
    period = jnp.where(is_ctx, TM, GRID_W)
    pos = row & (period - 1)
    left = jnp.where(pos == 0, 0.0, pltpu.roll(z, 1, 0))
    right = jnp.where(pos == period - 1, 0.0, pltpu.roll(z, TM - 1, 0))
    zc = left * cw_ref[0:1, :] + z * cw_ref[1:2, :] + right * cw_ref[2:3, :] + cb_ref[...]
    yconv_ref[...] = (proj(0) * zc).astype(BF16)

    cos = cos_ref[0]
    sin = sin_ref[0]
    q = proj(3)
    k = proj(4)
    q = jnp.concatenate([_rope(q[:, hh * DK:(hh + 1) * DK], cos, sin) for hh in range(HEADS)], axis=1)
    k = jnp.concatenate([_rope(k[:, hh * DK:(hh + 1) * DK], cos, sin) for hh in range(HEADS)], axis=1)
    q_ref[...] = (q * (DK ** -0.5)).astype(BF16)
    kt = k.T
    kt_ref[...] = kt.astype(BF16)
    v = proj(5).astype(BF16)
    v_ref[...] = v
    g_ref[...] = proj(6)

    for c in (1, 0):
        for hh in range(HEADS):
            sbin_ref[c, hh] = sb_scr[hh].astype(BF16)
            kts = (kt[hh * DK:(hh + 1) * DK, c * CHUNK:(c + 1) * CHUNK] * tab_scr[0, hh]).astype(BF16)
            vc = v[c * CHUNK:(c + 1) * CHUNK, hh * DK:(hh + 1) * DK]
            sb_scr[hh] = sb_scr[hh] * tab_scr[1, hh] + _dot(kts, vc)

    @pl.when(is_ctx)
    def _():
        sbfin_ref[0] = sb_scr[...]


def _mix_b_body(tiles, xp_ref, xs_ref, mod_ref, q_ref, kt_ref, v_ref, g_ref, yconv_ref, sbin_ref,
                wout_ref, gpost_ref, gffn_ref, gn_ref, dec_ref, s0f_ref,
                xnew_ref, tok_ref, sffin_ref,
                sf_scr, tab_scr, ycat_scr):
    i = pl.program_id(0)
    is_ctx = tiles.is_ctx(i)
    _, t_pos = tiles.lat_pos(i)
    first = jnp.logical_or(is_ctx, t_pos == 0)

    @pl.when(i == 0)
    def _():
        lgf = _log_sigmoid(dec_ref[0])
        lgb = _log_sigmoid(dec_ref[1])
        row = lax.broadcasted_iota(I32, lgf.shape, 1)
        col = lax.broadcasted_iota(I32, lgf.shape, 2)
        d = (row - col).astype(F32)
        tab_scr[0] = (jnp.where(row >= col, jnp.exp(jnp.where(row >= col, d, 0.0) * lgf), 0.0)
                      + jnp.where(col >= row, jnp.exp(jnp.where(col >= row, -d, 0.0) * lgb), 0.0))
        tab_scr[1] = jnp.exp((row + 1).astype(F32) * lgf)
        tab_scr[2] = jnp.exp((CHUNK - row).astype(F32) * lgb)
        tab_scr[3] = jnp.exp((CHUNK - 1 - col).astype(F32) * lgf)
        tab_scr[4] = jnp.exp(float(CHUNK) * lgf)

    @pl.when(first)
    def _():
        sf_scr[...] = jnp.where(is_ctx, 0.0, s0f_ref[0])

    for c in range(TM // CHUNK):
        rows = slice(c * CHUNK, (c + 1) * CHUNK)
        for hh in range(HEADS):
            cols = slice(hh * DK, (hh + 1) * DK)
            qc = q_ref[rows, cols]
            ktc = kt_ref[cols, rows]
            vc = v_ref[rows, cols]
            att = (_dot(qc, ktc) * tab_scr[0, hh]).astype(BF16)
            o = (_dot(att, vc)
                 + tab_scr[1, hh] * _dot(qc, sf_scr[hh].astype(BF16))
                 + tab_scr[2, hh] * _dot(qc, sbin_ref[c, hh]))
            kts = (ktc.astype(F32) * tab_scr[3, hh]).astype(BF16)
            sf_scr[hh] = sf_scr[hh] * tab_scr[4, hh] + _dot(kts, vc)
            mu = jnp.mean(o, axis=-1, keepdims=True)
            dev = o - mu
            var = jnp.mean(dev * dev, axis=-1, keepdims=True)
            on = dev * lax.rsqrt(var + EPS) * gn_ref[:, cols]
            ycat_scr[rows, RET_W + hh * DK:RET_W + (hh + 1) * DK] = (_silu(g_ref[rows, cols]) * on).astype(BF16)
    ycat_scr[:, 0:CONV_W] = yconv_ref[...]

    @pl.when(is_ctx)
    def _():
        sffin_ref[0] = sf_scr[...]

    x = jnp.where(is_ctx, xp_ref[...], xs_ref[...])
    u = _dot(ycat_scr[...], wout_ref[...])
    xn = x + mod_ref[0, 2:3, :] * _rms(u, gpost_ref[...])
    xnew_ref[...] = xn
    tok_ref[...] = _rms(xn, gffn_ref[...]) * (1.0 + mod_ref[0, 4:5, :]) + mod_ref[0, 3:4, :]


def _token_mixer(tiles, xp2, xs2, mod3, g_pre, win_bf, conv_w, conv_b, dec, cos_t, sin_t,
                 s0f, s0b, wout_bf, g_post, g_ffn, gn_g):
    n_tok = tiles.n_tiles * TM
    n_ctx = tiles.n_ctx
    last_ctx = n_ctx - 1

    def full(shape):
        return pl.BlockSpec(shape, lambda i: (0,) * len(shape))

    def xp_spec(phys):
        return pl.BlockSpec((TM, D_MODEL), lambda i: (jnp.minimum(phys(i), last_ctx), 0))

    def xs_spec(phys):
        return pl.BlockSpec((TM, D_MODEL), lambda i: (jnp.maximum(phys(i) - n_ctx, 0), 0))

    mod_spec = pl.BlockSpec((1, 6, D_MODEL), lambda i: (tiles.mod_row(i), 0, 0))
    state_in = pl.BlockSpec((1, HEADS, DK, DK), lambda i: (tiles.lat_pos(i)[0], 0, 0, 0))
    state_out = pl.BlockSpec((1, HEADS, DK, DK), lambda i: (jnp.minimum(i, last_ctx), 0, 0, 0))

    rev = tiles.phys_reversed

    def rope_idx(i):
        _, t = tiles.lat_pos(i)
        return jnp.where(i < n_ctx, 0, 1 + (tiles.lat_tiles - 1 - t))

    rope_spec = pl.BlockSpec((1, TM, DK), lambda i: (rope_idx(i), 0, 0))

    def rows(width, phys):
        return pl.BlockSpec((TM, width), lambda i: (phys(i), 0))

    yconv, q, v, g, kt, sbin, sb_fin = pl.pallas_call(
        functools.partial(_mix_a_body, tiles),
        out_shape=(jax.ShapeDtypeStruct((n_tok, CONV_W), BF16),
                   jax.ShapeDtypeStruct((n_tok, RET_W), BF16),
                   jax.ShapeDtypeStruct((n_tok, RET_W), BF16),
                   jax.ShapeDtypeStruct((n_tok, RET_W), F32),
                   jax.ShapeDtypeStruct((RET_W, n_tok), BF16),
                   jax.ShapeDtypeStruct((n_tok // CHUNK, HEADS, DK, DK), BF16),
                   jax.ShapeDtypeStruct((n_ctx, HEADS, DK, DK), F32)),
        grid=(tiles.n_tiles,),
        in_specs=[xp_spec(rev), xs_spec(rev), mod_spec, full((1, D_MODEL)), full((D_MODEL, IN_COLS)),
                  full((3, CONV_W)), full((1, CONV_W)), full((2, HEADS, DK, DK)),
                  rope_spec, rope_spec, state_in],
        out_specs=(rows(CONV_W, rev), rows(RET_W, rev), rows(RET_W, rev), rows(RET_W, rev),
                   pl.BlockSpec((RET_W, TM), lambda i: (0, rev(i))),
                   pl.BlockSpec((TM // CHUNK, HEADS, DK, DK), lambda i: (rev(i), 0, 0, 0)),
                   state_out),
        scratch_shapes=[pltpu.VMEM((HEADS, DK, DK), F32), pltpu.VMEM((2, HEADS, DK, DK), F32)],
        compiler_params=_cparams(),
        name="mix_a",
    )(xp2, xs2, mod3, g_pre, win_bf, conv_w, conv_b, dec, cos_t, sin_t, s0b)

    ident = lambda i: i
    xnew, tok, sf_fin = pl.pallas_call(
        functools.partial(_mix_b_body, tiles),
        out_shape=(jax.ShapeDtypeStruct((n_tok, D_MODEL), F32),
                   jax.ShapeDtypeStruct((n_tok, D_MODEL), F32),
                   jax.ShapeDtypeStruct((n_ctx, HEADS, DK, DK), F32)),
        grid=(tiles.n_tiles,),
        in_specs=[xp_spec(ident), xs_spec(ident), mod_spec,
                  rows(RET_W, ident),
                  pl.BlockSpec((RET_W, TM), lambda i: (0, i)),
                  rows(RET_W, ident), rows(RET_W, ident), rows(CONV_W, ident),
                  pl.BlockSpec((TM // CHUNK, HEADS, DK, DK), lambda i: (i, 0, 0, 0)),
                  full((D_MODEL, D_MODEL)), full((1, D_MODEL)), full((1, D_MODEL)), full((1, RET_W)),
                  full((2, HEADS, DK, DK)), state_in],
        out_specs=(rows(D_MODEL, ident), rows(D_MODEL, ident), state_out),
        scratch_shapes=[pltpu.VMEM((HEADS, DK, DK), F32), pltpu.VMEM((5, HEADS, DK, DK), F32),
                        pltpu.VMEM((TM, D_MODEL), BF16)],
        compiler_params=_cparams(),
        name="mix_b",
    )(xp2, xs2, mod3, q, kt, v, g, yconv, sbin, wout_bf, g_post, g_ffn, gn_g, dec, s0f)
    return xnew, tok, sf_fin, sb_fin


def _route_body(tok_ref, rwt_ref, bias_ref, topi_ref, topw_ref):
    h = tok_ref[...].astype(BF16)
    logits = lax.dot_general(rwt_ref[...], h, (((1,), (1,)), ((), ())), preferred_element_type=F32)
    shape3 = (GROUP_SIZE, N_GROUPS, 128)
    member = lax.broadcasted_iota(I32, shape3, 0)
    group = lax.broadcasted_iota(I32, shape3, 1)
    expert = group * GROUP_SIZE + member
    group2 = lax.broadcasted_iota(I32, (N_GROUPS, 128), 0)
    neg = -jnp.inf
    for lb in range(RT // 128):
        scores = jax.nn.sigmoid(logits[:, lb * 128:(lb + 1) * 128]).reshape(shape3)
        biased = scores + bias_ref[...].reshape(shape3)
        m1 = jnp.max(biased, axis=0)
        first = jnp.min(jnp.where(biased == m1, member, GROUP_SIZE), axis=0)
        m2 = jnp.max(jnp.where(member == first, neg, biased), axis=0)
        gs = m1 + m2
        beaten = jnp.zeros(gs.shape, I32)
        for s in range(1, N_GROUPS):
            other = pltpu.roll(gs, s, 0)
            wins = (other > gs) | ((other == gs) & (group2 >= s))
            beaten = beaten + wins.astype(I32)
        keep = beaten < TOPK_GROUPS
        cand = jnp.where(keep, biased, neg)
        idx_rows, w_rows = [], []
        for _ in range(TOP_K):
            best = jnp.max(jnp.max(cand, axis=0), axis=0, keepdims=True)
            pick = jnp.min(jnp.min(jnp.where(cand == best, expert, N_EXPERTS), axis=0), axis=0, keepdims=True)
            hit = expert == pick
            w_rows.append(jnp.sum(jnp.sum(jnp.where(hit, scores, 0.0), axis=0), axis=0, keepdims=True))
            idx_rows.append(pick)
            cand = jnp.where(hit, neg, cand)
        w = jnp.concatenate(w_rows, axis=0)
        topi_ref[:, lb * 128:(lb + 1) * 128] = jnp.concatenate(idx_rows, axis=0)
        topw_ref[:, lb * 128:(lb + 1) * 128] = w / jnp.sum(w, axis=0, keepdims=True) * ROUTED_SCALE


def _route(tok, rwt_bf, bias_b):
    n_tok = tok.shape[0]
    return pl.pallas_call(
        _route_body,
        out_shape=(jax.ShapeDtypeStruct((TOP_K, n_tok), I32), jax.ShapeDtypeStruct((TOP_K, n_tok), F32)),
        grid=(n_tok // RT,),
        in_specs=[pl.BlockSpec((RT, D_MODEL), lambda i: (i, 0)),
                  pl.BlockSpec((N_EXPERTS, D_MODEL), lambda i: (0, 0)),
                  pl.BlockSpec((N_EXPERTS, 128), lambda i: (0, 0))],
        out_specs=(pl.BlockSpec((TOP_K, RT), lambda i: (0, i)), pl.BlockSpec((TOP_K, RT), lambda i: (0, i))),
        compiler_params=_cparams(),
        name="route",
    )(tok, rwt_bf, bias_b)


def _onehot(ids_row):
    e = lax.broadcasted_iota(I32, (N_EXPERTS, 256), 0)
    return e == ids_row


def _rank_body(topi_ref, rank_ref, counts_ref, run_scr):
    i = pl.program_id(0)

    @pl.when(i == 0)
    def _():
        run_scr[...] = jnp.zeros(run_scr.shape, F32)

    a0 = lax.broadcasted_iota(I32, (256, 256), 0)
    a1 = lax.broadcasted_iota(I32, (256, 256), 1)
    upper = (a0 <= a1).astype(BF16)
    ones = jnp.ones((256, 256), BF16)
    for k in range(TOP_K):
        for sb in range(RB // 256):
            lanes = slice(sb * 256, (sb + 1) * 256)
            oh = _onehot(topi_ref[k:k + 1, lanes])
            ohb = oh.astype(BF16)
            seen = _dot(ohb, upper) + run_scr[...]
            r = jnp.sum(jnp.where(oh, seen, 0.0), axis=0, keepdims=True) - 1.0
            rank_ref[k:k + 1, lanes] = r.astype(I32)
            run_scr[...] = run_scr[...] + _dot(ohb, ones)

    @pl.when(i == pl.num_programs(0) - 1)
    def _():
        counts_ref[...] = run_scr[:, 0:128]


def _dest_body(topi_ref, rank_ref, counts_ref, dest_ref, start_ref, start_scr):
    i = pl.program_id(0)

    @pl.when(i == 0)
    def _():
        nb = jnp.floor((counts_ref[...] + float(SUB - 1)) / float(SUB))
        hi = jnp.floor(nb / 16.0)
        lo = nb - hi * 16.0
        e0 = lax.broadcasted_iota(I32, (N_EXPERTS, N_EXPERTS), 0)
        e1 = lax.broadcasted_iota(I32, (N_EXPERTS, N_EXPERTS), 1)
        below = (e1 < e0).astype(BF16)
        first_blk = 16.0 * _dot(below, hi.astype(BF16)) + _dot(below, lo.astype(BF16))
        start_scr[...] = first_blk * float(SUB)
        start_ref[...] = start_scr[...]

    start = jnp.concatenate([start_scr[...], start_scr[...]], axis=1)
    for k in range(TOP_K):
        for sb in range(RB // 256):
            lanes = slice(sb * 256, (sb + 1) * 256)
            oh = _onehot(topi_ref[k:k + 1, lanes])
            base = jnp.sum(jnp.where(oh, start, 0.0), axis=0, keepdims=True)
            dest_ref[k:k + 1, lanes] = base.astype(I32) + rank_ref[k:k + 1, lanes]


def _dispatch_plan(topi):
    n_tok = topi.shape[1]
    blk = pl.BlockSpec((TOP_K, RB), lambda i: (0, i))
    whole = pl.BlockSpec((N_EXPERTS, 128), lambda i: (0, 0))
    rank, counts = pl.pallas_call(
        _rank_body,
        out_shape=(jax.ShapeDtypeStruct((TOP_K, n_tok), I32), jax.ShapeDtypeStruct((N_EXPERTS, 128), F32)),
        grid=(n_tok // RB,),
        in_specs=[blk],
        out_specs=(blk, whole),
        scratch_shapes=[pltpu.VMEM((N_EXPERTS, 256), F32)],
        compiler_params=_cparams(),
        name="rank",
    )(topi)
    dest, start = pl.pallas_call(
        _dest_body,
        out_shape=(jax.ShapeDtypeStruct((TOP_K, n_tok), I32), jax.ShapeDtypeStruct((N_EXPERTS, 128), F32)),
        grid=(n_tok // RB,),
        in_specs=[blk, blk, whole],
        out_specs=(blk, whole),
        scratch_shapes=[pltpu.VMEM((N_EXPERTS, 128), F32)],
        compiler_params=_cparams(),
        name="dest",
    )(topi, rank, counts)
    return dest, start, counts


ROW_TILES = D_MODEL // 128


def _rows_from_tiles(ref, lead, n_rows):
    return jnp.concatenate([ref[lead + (pl.ds(c, n_rows, stride=ROW_TILES), slice(None))]
                            for c in range(ROW_TILES)], axis=1)


def _rows_to_tiles(ref, lead, value):
    n_rows = value.shape[0]
    for c in range(ROW_TILES):
        ref[lead + (pl.ds(c, n_rows, stride=ROW_TILES), slice(None))] = value[:, c * 128:(c + 1) * 128]


def _start_row_gather(src_hbm, idx_of, dst, sem, n_rows):
    for r in range(n_rows):
        src_row = pl.multiple_of(idx_of(r) * ROW_TILES, ROW_TILES)
        pltpu.make_async_copy(src_hbm.at[pl.ds(src_row, ROW_TILES)], dst.at[pl.ds(r * ROW_TILES, ROW_TILES)],
                              sem).start(priority=r % 2)


INVERT_UNROLL = 16


def _invert_body(trips_ref, dest_hbm, zeros_hbm, out_hbm, dbuf, obuf, sem):
    k = pl.program_id(0)
    n_tok = dbuf.shape[0]

    @pl.when(k == 0)
    def _():
        init = pltpu.make_async_copy(zeros_hbm, obuf, sem.at[1])
        init.start()
        init.wait()

    fetch = pltpu.make_async_copy(dest_hbm.at[pl.ds(pl.multiple_of(k * n_tok, 128), n_tok)], dbuf, sem.at[0])
    fetch.start()
    fetch.wait()

    def body(tb, carry):
        for u in range(INVERT_UNROLL):
            t = tb * INVERT_UNROLL + u
            obuf[dbuf[t]] = t
        return carry

    lax.fori_loop(0, trips_ref[0], body, 0)

    @pl.when(k == pl.num_programs(0) - 1)
    def _():
        done = pltpu.make_async_copy(obuf, out_hbm, sem.at[1])
        done.start()
        done.wait()


def _invert(dest_flat, n_tok, n_slots):
    trips = jnp.full((1,), n_tok // INVERT_UNROLL, I32)
    return pl.pallas_call(
        _invert_body,
        out_shape=jax.ShapeDtypeStruct((n_slots,), I32),
        grid_spec=pltpu.PrefetchScalarGridSpec(
            num_scalar_prefetch=1,
            grid=(TOP_K,),
            in_specs=[pl.BlockSpec(memory_space=pl.ANY), pl.BlockSpec(memory_space=pl.ANY)],
            out_specs=pl.BlockSpec(memory_space=pl.ANY),
            scratch_shapes=[pltpu.SMEM((n_tok,), I32), pltpu.SMEM((n_slots,), I32),
                            pltpu.SemaphoreType.DMA((2,))]),
        compiler_params=_cparams(),
        name="invert",
    )(trips, dest_flat, jnp.zeros((n_slots,), I32))


N_XBUF = 3
LOOKAHEAD = 2


def _experts_body(start_ref, nsub_ref, slot_tok_ref,
                  tok_hbm, wg_ref, wu_ref, wd_ref, y_hbm,
                  xbuf, ybuf, wg_bf, wu_bf, wd_bf, cur, gsem, osem):
    e = pl.program_id(0)
    n_e = pl.num_programs(0)
    nsub = nsub_ref[e]
    sub_rows = SUB * ROW_TILES

    def next_nonempty(e0):
        return lax.while_loop(lambda x: jnp.logical_and(x < n_e, nsub_ref[jnp.minimum(x, n_e - 1)] == 0),
                              lambda x: x + 1, e0)

    def produce():
        pe = cur[0]

        @pl.when(pe < n_e)
        def _():
            pj = cur[1]
            pg = cur[2]
            base = start_ref[pe] + pj * SUB
            slot = lax.rem(pg, N_XBUF)
            _start_row_gather(tok_hbm, lambda r: slot_tok_ref[base + r], xbuf.at[slot], gsem.at[slot], SUB)
            last = pj + 1 >= nsub_ref[pe]
            cur[0] = jnp.where(last, next_nonempty(pe + 1), pe)
            cur[1] = jnp.where(last, 0, pj + 1)
            cur[2] = pg + 1

    def out_wait(slot):
        pltpu.make_async_copy(ybuf.at[slot], y_hbm.at[pl.ds(0, sub_rows)], osem.at[slot]).wait()

    @pl.when(e == 0)
    def _():
        cur[0] = next_nonempty(0)
        cur[1] = 0
        cur[2] = 0
        cur[3] = 0
        for _ in range(LOOKAHEAD):
            produce()

    wg_bf[...] = wg_ref[0].astype(BF16)
    wu_bf[...] = wu_ref[0].astype(BF16)
    wd_bf[...] = wd_ref[0].astype(BF16)

    def step(j, carry):
        g = cur[3]
        produce()
        slot = lax.rem(g, N_XBUF)
        yslot = g & 1
        pltpu.make_async_copy(tok_hbm.at[pl.ds(0, sub_rows)], xbuf.at[slot], gsem.at[slot]).wait()

        @pl.when(g >= 2)
        def _():
            out_wait(yslot)

        xb = _rows_from_tiles(xbuf, (slot,), SUB).astype(BF16)
        a = _dot(xb, wg_bf[...])
        b = _dot(xb, wu_bf[...])
        _rows_to_tiles(ybuf, (yslot,), _dot((_silu(a) * b).astype(BF16), wd_bf[...]))
        row0 = pl.multiple_of((start_ref[e] + j * SUB) * ROW_TILES, sub_rows)
        pltpu.make_async_copy(ybuf.at[yslot], y_hbm.at[pl.ds(row0, sub_rows)], osem.at[yslot]).start()
        cur[3] = g + 1
        return carry

    lax.fori_loop(0, nsub, step, 0)

    @pl.when(e == n_e - 1)
    def _():
        total = cur[3]

        @pl.when(total >= 2)
        def _():
            out_wait(total & 1)

        @pl.when(total >= 1)
        def _():
            out_wait((total - 1) & 1)

        ybuf[0] = jnp.zeros(ybuf.shape[1:], F32)
        used = start_ref[e] // SUB + nsub

        def tail_copy(sb):
            return pltpu.make_async_copy(ybuf.at[0], y_hbm.at[pl.ds(pl.multiple_of(sb * sub_rows, sub_rows), sub_rows)],
                                         osem.at[0])

        n_sub_total = y_hbm.shape[0] // sub_rows
        lax.fori_loop(used, n_sub_total, lambda sb, c: (tail_copy(sb).start(), c)[1], 0)
        lax.fori_loop(used, n_sub_total, lambda sb, c: (tail_copy(sb).wait(), c)[1], 0)


def _experts(start, nsub, slot_tok, tok_tiles, wg, wu, wd, n_slots):
    sub_rows = SUB * ROW_TILES
    return pl.pallas_call(
        _experts_body,
        out_shape=jax.ShapeDtypeStruct((n_slots * ROW_TILES, 128), F32),
        grid_spec=pltpu.PrefetchScalarGridSpec(
            num_scalar_prefetch=3,
            grid=(N_EXPERTS,),
            in_specs=[pl.BlockSpec(memory_space=pl.ANY),
                      pl.BlockSpec((1, D_MODEL, FF), lambda e, *_: (e, 0, 0)),
                      pl.BlockSpec((1, D_MODEL, FF), lambda e, *_: (e, 0, 0)),
                      pl.BlockSpec((1, FF, D_MODEL), lambda e, *_: (e, 0, 0))],
            out_specs=pl.BlockSpec(memory_space=pl.ANY),
            scratch_shapes=[pltpu.VMEM((N_XBUF, sub_rows, 128), F32), pltpu.VMEM((2, sub_rows, 128), F32),
                            pltpu.VMEM((D_MODEL, FF), BF16), pltpu.VMEM((D_MODEL, FF), BF16),
                            pltpu.VMEM((FF, D_MODEL), BF16), pltpu.SMEM((4,), I32),
                            pltpu.SemaphoreType.DMA((N_XBUF,)), pltpu.SemaphoreType.DMA((2,))]),
        compiler_params=_cparams(),
        name="experts",
    )(start, nsub, slot_tok, tok_tiles, wg, wu, wd)


def _combine_body(n_tok, n_ctx_tiles, dest_ref,
                  y_hbm, topw_ref, tok_ref, xnew_ref, mod_ref, sg_ref, su_ref, sd_ref, gpost_ref,
                  outp_ref, outs_ref, gbuf, gsem):
    i = pl.program_id(0)
    n = pl.num_programs(0)

    def gather(tile, slot):
        def per_choice(k, carry):
            base = k * n_tok + tile * TT
            _start_row_gather(y_hbm, lambda r: dest_ref[base + r], gbuf.at[slot, k], gsem.at[slot], TT)
            return carry
        lax.fori_loop(0, TOP_K, per_choice, 0)

    def gather_wait(slot):
        for k in range(TOP_K):
            pltpu.make_async_copy(y_hbm.at[pl.ds(0, TT * ROW_TILES)], gbuf.at[slot, k], gsem.at[slot]).wait()

    slot = i & 1

    @pl.when(i == 0)
    def _():
        gather(0, 0)

    @pl.when(i + 1 < n)
    def _():
        gather(i + 1, 1 - slot)

    w_t = jnp.concatenate([topw_ref[...], jnp.zeros((128 - TOP_K, TT), F32)], axis=0).T
    h = tok_ref[...].astype(BF16)
    shared = _dot((_silu(_dot(h, sg_ref[...])) * _dot(h, su_ref[...])).astype(BF16), sd_ref[...])

    gather_wait(slot)
    routed = _rows_from_tiles(gbuf, (slot, 0), TT) * w_t[:, 0:1]
    for k in range(1, TOP_K):
        routed = routed + _rows_from_tiles(gbuf, (slot, k), TT) * w_t[:, k:k + 1]
    out = xnew_ref[...] + mod_ref[0, 5:6, :] * _rms(routed + shared, gpost_ref[...])

    @pl.when(i < n_ctx_tiles)
    def _():
        outp_ref[...] = out

    @pl.when(i >= n_ctx_tiles)
    def _():
        outs_ref[...] = out


def _combine(tiles, dest_flat, ysorted, topw, tok, xnew, mod3, sg_bf, su_bf, sd_bf, g_post):
    n_tok = tok.shape[0]
    n_ctx_tok = tiles.n_ctx * TM
    n_ctx_tiles = n_ctx_tok // TT
    lat_tiles_per_seq = tiles.lat_tiles * TM // TT

    def mod_row(i):
        return jnp.where(i < n_ctx_tiles, 0, 1 + jnp.maximum(i - n_ctx_tiles, 0) // lat_tiles_per_seq)

    def full(shape):
        return pl.BlockSpec(shape, lambda i, *_: (0,) * len(shape))

    rows = pl.BlockSpec((TT, D_MODEL), lambda i, *_: (i, 0))
    return pl.pallas_call(
        functools.partial(_combine_body, n_tok, n_ctx_tiles),
        out_shape=(jax.ShapeDtypeStruct((n_ctx_tok, D_MODEL), F32),
                   jax.ShapeDtypeStruct((n_tok - n_ctx_tok, D_MODEL), F32)),
        grid_spec=pltpu.PrefetchScalarGridSpec(
            num_scalar_prefetch=1,
            grid=(n_tok // TT,),
            in_specs=[pl.BlockSpec(memory_space=pl.ANY),
                      pl.BlockSpec((TOP_K, TT), lambda i, *_: (0, i)),
                      rows, rows,
                      pl.BlockSpec((1, 6, D_MODEL), lambda i, *_: (mod_row(i), 0, 0)),
                      full((D_MODEL, FF)), full((D_MODEL, FF)), full((FF, D_MODEL)), full((1, D_MODEL))],
            out_specs=(pl.BlockSpec((TT, D_MODEL), lambda i, *_: (jnp.minimum(i, n_ctx_tiles - 1), 0)),
                       pl.BlockSpec((TT, D_MODEL), lambda i, *_: (jnp.maximum(i - n_ctx_tiles, 0), 0))),
            scratch_shapes=[pltpu.VMEM((2, TOP_K, TT * ROW_TILES, 128), F32), pltpu.SemaphoreType.DMA((2,))]),
        compiler_params=_cparams(),
        name="combine",
    )(dest_flat, ysorted, topw, tok, xnew, mod3, sg_bf, su_bf, sd_bf, g_post)


def _rope_tables(lat_len, lat_tiles):
    rows = lat_len // GRID_W
    row = jnp.repeat(jnp.arange(rows, dtype=F32), GRID_W)
    col = jnp.tile(jnp.arange(GRID_W, dtype=F32), rows)
    inv = ROPE_BASE ** (-jnp.arange(ROPE_FREQS, dtype=F32) / ROPE_FREQS)
    ang = jnp.concatenate([row[:, None] * inv[None, :]] * 2 + [col[:, None] * inv[None, :]] * 2, axis=1)
    sign = jnp.tile(jnp.concatenate([-jnp.ones((ROPE_FREQS,), F32), jnp.ones((ROPE_FREQS,), F32)]), 2)
    cos = jnp.cos(ang).reshape(lat_tiles, TM, DK)
    sin = (jnp.sin(ang) * sign[None, :]).reshape(lat_tiles, TM, DK)
    cos = jnp.concatenate([jnp.ones((1, TM, DK), F32), cos], axis=0)
    sin = jnp.concatenate([jnp.zeros((1, TM, DK), F32), sin], axis=0)
    return cos, sin


def kernel(x_prompt, x_sample, state_ret_fwd, state_ret_bwd, c, c_ctx, w_mod, b_mod, norm_mix_pre,
           norm_mix_post, norm_ffn_pre, norm_ffn_post, w_in, conv_w, conv_b, ret_decay_fwd,
           ret_decay_bwd, ret_gn_g, w_out, router_w, router_bias, expert_w_gate, expert_w_up,
           expert_w_down, shared_w_gate, shared_w_up, shared_w_down):
    bp, tp, d = x_prompt.shape
    bs, ts, _ = x_sample.shape
    depth = w_mod.shape[0]
    assert d == D_MODEL and tp == TM and ts % TM == 0 and bs + 1 <= 8
    tiles = _Tiles(bp, bs, ts)
    n_tok = tiles.n_tiles * TM
    n_slots = n_tok * TOP_K + N_EXPERTS * SUB
    cos_t, sin_t = _rope_tables(ts, tiles.lat_tiles)

    perm = (jnp.arange(N_EXPERTS) % N_GROUPS) * GROUP_SIZE + jnp.arange(N_EXPERTS) // N_GROUPS

    xp2 = x_prompt.reshape(bp * tp, d)
    xs2 = x_sample.reshape(bs * ts, d)
    new_f, new_b = [], []
    for l in range(depth):
        c_rows = jnp.concatenate([c_ctx[None, :], c, jnp.zeros((8 - 1 - bs, d), F32)], axis=0)
        mod3 = _modulation(c_rows, w_mod[l], b_mod[l][None, :]).reshape(8, 6, d)
        dec = jnp.broadcast_to(jnp.stack([ret_decay_fwd[l], ret_decay_bwd[l]])[:, :, None, None],
                               (2, HEADS, DK, DK)).astype(F32)
        xnew, tok, sf_fin, sb_fin = _token_mixer(
            tiles, xp2, xs2, mod3, norm_mix_pre[l][None, :], w_in[l].astype(BF16), conv_w[l],
            conv_b[l][None, :], dec, cos_t, sin_t, state_ret_fwd[:, l], state_ret_bwd[:, l],
            w_out[l].astype(BF16), norm_mix_post[l][None, :], norm_ffn_pre[l][None, :], ret_gn_g[l][None, :])
        new_f.append(sf_fin)
        new_b.append(sb_fin)

        rwt = router_w[l].T[perm].astype(BF16)
        bias_b = jnp.broadcast_to(router_bias[l][perm][:, None], (N_EXPERTS, 128)).astype(F32)
        topi, topw = _route(tok, rwt, bias_b)
        dest, start, counts = _dispatch_plan(topi)
        dest_flat = dest.reshape(-1)
        slot_tok = _invert(dest_flat, n_tok, n_slots)
        start_i = start[:, 0].astype(I32)
        nsub = ((counts[:, 0] + float(SUB - 1)) / float(SUB)).astype(I32)
        tok_tiles = tok.reshape(n_tok * ROW_TILES, 128)
        ysorted = _experts(start_i, nsub, slot_tok, tok_tiles, expert_w_gate[l], expert_w_up[l],
                           expert_w_down[l], n_slots)
        xp2, xs2 = _combine(tiles, dest_flat, ysorted, topw, tok, xnew, mod3,
                            shared_w_gate[l].astype(BF16), shared_w_up[l].astype(BF16),
                            shared_w_down[l].astype(BF16), norm_ffn_post[l][None, :])

    return (xp2.reshape(bp, tp, d), xs2.reshape(bs, ts, d),
            jnp.stack(new_f, axis=1), jnp.stack(new_b, axis=1))
```

```python
import functools

import jax
import jax.numpy as jnp
from jax import lax
from jax.experimental import pallas as pl
from jax.experimental.pallas import tpu as pltpu

F32 = jnp.float32
BF16 = jnp.bfloat16
I32 = jnp.int32

D_MODEL = 1024
CONV_W = 512
RET_W = 512
HEADS = 4
DK = 128
CHUNK = 128
GRID_W = 64
ROPE_FREQS = 32
ROPE_BASE = 10000.0
IN_COLS = 3 * CONV_W + 4 * RET_W
N_EXPERTS = 256
N_GROUPS = 8
GROUP_SIZE = N_EXPERTS // N_GROUPS
TOPK_GROUPS = 4
TOP_K = 8
FF = 256
ROUTED_SCALE = 2.5
EPS = 1e-6

TM = 256
SUB = 128
TT = 128
RT = 512
RB = 2048
VMEM_LIMIT = 56 * 1024 * 1024


def _cparams(n_axes=1, vmem=VMEM_LIMIT):
    return pltpu.CompilerParams(dimension_semantics=("arbitrary",) * n_axes,
                                vmem_limit_bytes=vmem)


def _silu(x):
    return x * jax.nn.sigmoid(x)


def _log_sigmoid(x):
    return jnp.minimum(x, 0.0) - jnp.log1p(jnp.exp(-jnp.abs(x)))


def _rms(x, g):
    return x * lax.rsqrt(jnp.mean(x * x, axis=-1, keepdims=True) + EPS) * g


def _dot(a, b):
    return jnp.dot(a, b, preferred_element_type=F32)


def _mod_body(c_ref, w_ref, b_ref, o_ref):
    s = _silu(c_ref[...]).astype(BF16)
    o_ref[...] = _dot(s, w_ref[...].astype(BF16)) + b_ref[...]


def _modulation(c_rows, w_mod, b_mod):
    n_col = w_mod.shape[1]
    blk = 1536
    return pl.pallas_call(
        _mod_body,
        out_shape=jax.ShapeDtypeStruct((8, n_col), F32),
        grid=(n_col // blk,),
        in_specs=[pl.BlockSpec((8, D_MODEL), lambda i: (0, 0)),
                  pl.BlockSpec((D_MODEL, blk), lambda i: (0, i)),
                  pl.BlockSpec((1, blk), lambda i: (0, i))],
        out_specs=pl.BlockSpec((8, blk), lambda i: (0, i)),
        compiler_params=_cparams(),
        name="mod",
    )(c_rows, w_mod, b_mod)


class _Tiles:
    def __init__(self, n_ctx_seq, n_lat_seq, lat_len):
        self.n_ctx = n_ctx_seq
        self.lat_tiles = lat_len // TM
        self.n_lat_seq = n_lat_seq
        self.n_tiles = n_ctx_seq + n_lat_seq * self.lat_tiles

    def is_ctx(self, i):
        return i < self.n_ctx

    def lat_pos(self, i):
        j = jnp.maximum(i - self.n_ctx, 0)
        return j // self.lat_tiles, j % self.lat_tiles

    def phys_reversed(self, i):
        b, t = self.lat_pos(i)
        return jnp.where(i < self.n_ctx, i, self.n_ctx + b * self.lat_tiles + (self.lat_tiles - 1 - t))

    def mod_row(self, i):
        b, _ = self.lat_pos(i)
        return jnp.where(i < self.n_ctx, 0, 1 + b)


def _rope(x, cos, sin_signed):
    lane = lax.broadcasted_iota(I32, x.shape, 1)
    partner = jnp.where((lane & 63) < 32, pltpu.roll(x, 96, 1), pltpu.roll(x, 32, 1))
    return x * cos + partner * sin_signed


def _mix_a_body(tiles, xp_ref, xs_ref, mod_ref, gpre_ref, win_ref, cw_ref, cb_ref, dec_ref,
                cos_ref, sin_ref, s0b_ref,
                yconv_ref, q_ref, v_ref, g_ref, kt_ref, sbin_ref, sbfin_ref,
                sb_scr, tab_scr):
    i = pl.program_id(0)
    is_ctx = tiles.is_ctx(i)
    _, t_rev = tiles.lat_pos(i)
    first = jnp.logical_or(is_ctx, t_rev == 0)

    @pl.when(i == 0)
    def _():
        lg = _log_sigmoid(dec_ref[1])
        col = lax.broadcasted_iota(I32, lg.shape, 2).astype(F32)
        tab_scr[0] = jnp.exp(col * lg)
        tab_scr[1] = jnp.exp(float(CHUNK) * lg)

    @pl.when(first)
    def _():
        sb_scr[...] = jnp.where(is_ctx, 0.0, s0b_ref[0])

    x = jnp.where(is_ctx, xp_ref[...], xs_ref[...])
    h = (_rms(x, gpre_ref[...]) * (1.0 + mod_ref[0, 1:2, :]) + mod_ref[0, 0:1, :]).astype(BF16)

    def proj(k):
        return _dot(h, win_ref[:, k * 512:(k + 1) * 512])

    z = proj(1) * proj(2)
    row = lax.broadcasted_iota(I32, z.shape, 0)
    period = jnp.where(is_ctx, TM, GRID_W)
    pos = row & (period - 1)
    left = jnp.where(pos == 0, 0.0, pltpu.roll(z, 1, 0))
    right = jnp.where(pos == period - 1, 0.0, pltpu.roll(z, TM - 1, 0))
    zc = left * cw_ref[0:1, :] + z * cw_ref[1:2, :] + right * cw_ref[2:3, :] + cb_ref[...]
    yconv_ref[...] = (proj(0) * zc).astype(BF16)

    cos = cos_ref[0]
    sin = sin_ref[0]
    q = proj(3)
    k = proj(4)
    q = jnp.concatenate([_rope(q[:, hh * DK:(hh + 1) * DK], cos, sin) for hh in range(HEADS)], axis=1)
    k = jnp.concatenate([_rope(k[:, hh * DK:(hh + 1) * DK], cos, sin) for hh in range(HEADS)], axis=1)
    q_ref[...] = (q * (DK ** -0.5)).astype(BF16)
    kt = k.T
    kt_ref[...] = kt.astype(BF16)
    v = proj(5).astype(BF16)
    v_ref[...] = v
    g_ref[...] = proj(6)

    for c in (1, 0):
        for hh in range(HEADS):
            sbin_ref[c, hh] = sb_scr[hh].astype(BF16)
            kts = (kt[hh * DK:(hh + 1) * DK, c * CHUNK:(c + 1) * CHUNK] * tab_scr[0, hh]).astype(BF16)
            vc = v[c * CHUNK:(c + 1) * CHUNK, hh * DK:(hh + 1) * DK]
            sb_scr[hh] = sb_scr[hh] * tab_scr[1, hh] + _dot(kts, vc)

    @pl.when(is_ctx)
    def _():
        sbfin_ref[0] = sb_scr[...]


def _mix_b_body(tiles, xp_ref, xs_ref, mod_ref, q_ref, kt_ref, v_ref, g_ref, yconv_ref, sbin_ref,
                wout_ref, gpost_ref, gffn_ref, gn_ref, dec_ref, s0f_ref,
                xnew_ref, tok_ref, sffin_ref,
                sf_scr, tab_scr, ycat_scr):
    i = pl.program_id(0)
    is_ctx = tiles.is_ctx(i)
    _, t_pos = tiles.lat_pos(i)
    first = jnp.logical_or(is_ctx, t_pos == 0)

    @pl.when(i == 0)
    def _():
        lgf = _log_sigmoid(dec_ref[0])
        lgb = _log_sigmoid(dec_ref[1])
        row = lax.broadcasted_iota(I32, lgf.shape, 1)
        col = lax.broadcasted_iota(I32, lgf.shape, 2)
        d = (row - col).astype(F32)
        tab_scr[0] = (jnp.where(row >= col, jnp.exp(jnp.where(row >= col, d, 0.0) * lgf), 0.0)
                      + jnp.where(col >= row, jnp.exp(jnp.where(col >= row, -d, 0.0) * lgb), 0.0))
        tab_scr[1] = jnp.exp((row + 1).astype(F32) * lgf)
        tab_scr[2] = jnp.exp((CHUNK - row).astype(F32) * lgb)
        tab_scr[3] = jnp.exp((CHUNK - 1 - col).astype(F32) * lgf)
        tab_scr[4] = jnp.exp(float(CHUNK) * lgf)

    @pl.when(first)
    def _():
        sf_scr[...] = jnp.where(is_ctx, 0.0, s0f_ref[0])

    for c in range(TM // CHUNK):
        rows = slice(c * CHUNK, (c + 1) * CHUNK)
        for hh in range(HEADS):
            cols = slice(hh * DK, (hh + 1) * DK)
            qc = q_ref[rows, cols]
            ktc = kt_ref[cols, rows]
            vc = v_ref[rows, cols]
            att = (_dot(qc, ktc) * tab_scr[0, hh]).astype(BF16)
            o = (_dot(att, vc)
                 + tab_scr[1, hh] * _dot(qc, sf_scr[hh].astype(BF16))
                 + tab_scr[2, hh] * _dot(qc, sbin_ref[c, hh]))
            kts = (ktc.astype(F32) * tab_scr[3, hh]).astype(BF16)
            sf_scr[hh] = sf_scr[hh] * tab_scr[4, hh] + _dot(kts, vc)
            mu = jnp.mean(o, axis=-1, keepdims=True)
            dev = o - mu
            var = jnp.mean(dev * dev, axis=-1, keepdims=True)
            on = dev * lax.rsqrt(var + EPS) * gn_ref[:, cols]
            ycat_scr[rows, RET_W + hh * DK:RET_W + (hh + 1) * DK] = (_silu(g_ref[rows, cols]) * on).astype(BF16)
    ycat_scr[:, 0:CONV_W] = yconv_ref[...]

    @pl.when(is_ctx)
    def _():
        sffin_ref[0] = sf_scr[...]

    x = jnp.where(is_ctx, xp_ref[...], xs_ref[...])
    u = _dot(ycat_scr[...], wout_ref[...])
    xn = x + mod_ref[0, 2:3, :] * _rms(u, gpost_ref[...])
    xnew_ref[...] = xn
    tok_ref[...] = _rms(xn, gffn_ref[...]) * (1.0 + mod_ref[0, 4:5, :]) + mod_ref[0, 3:4, :]


def _token_mixer(tiles, xp2, xs2, mod3, g_pre, win_bf, conv_w, conv_b, dec, cos_t, sin_t,
                 s0f, s0b, wout_bf, g_post, g_ffn, gn_g):
    n_tok = tiles.n_tiles * TM
    n_ctx = tiles.n_ctx
    last_ctx = n_ctx - 1

    def full(shape):
        return pl.BlockSpec(shape, lambda i: (0,) * len(shape))

    def xp_spec(phys):
        return pl.BlockSpec((TM, D_MODEL), lambda i: (jnp.minimum(phys(i), last_ctx), 0))

    def xs_spec(phys):
        return pl.BlockSpec((TM, D_MODEL), lambda i: (jnp.maximum(phys(i) - n_ctx, 0), 0))

    mod_spec = pl.BlockSpec((1, 6, D_MODEL), lambda i: (tiles.mod_row(i), 0, 0))
    state_in = pl.BlockSpec((1, HEADS, DK, DK), lambda i: (tiles.lat_pos(i)[0], 0, 0, 0))
    state_out = pl.BlockSpec((1, HEADS, DK, DK), lambda i: (jnp.minimum(i, last_ctx), 0, 0, 0))

    rev = tiles.phys_reversed

    def rope_idx(i):
        _, t = tiles.lat_pos(i)
        return jnp.where(i < n_ctx, 0, 1 + (tiles.lat_tiles - 1 - t))

    rope_spec = pl.BlockSpec((1, TM, DK), lambda i: (rope_idx(i), 0, 0))

    def rows(width, phys):
        return pl.BlockSpec((TM, width), lambda i: (phys(i), 0))

    yconv, q, v, g, kt, sbin, sb_fin = pl.pallas_call(
        functools.partial(_mix_a_body, tiles),
        out_shape=(jax.ShapeDtypeStruct((n_tok, CONV_W), BF16),
                   jax.ShapeDtypeStruct((n_tok, RET_W), BF16),
                   jax.ShapeDtypeStruct((n_tok, RET_W), BF16),
                   jax.ShapeDtypeStruct((n_tok, RET_W), F32),
                   jax.ShapeDtypeStruct((RET_W, n_tok), BF16),
                   jax.ShapeDtypeStruct((n_tok // CHUNK, HEADS, DK, DK), BF16),
                   jax.ShapeDtypeStruct((n_ctx, HEADS, DK, DK), F32)),
        grid=(tiles.n_tiles,),
        in_specs=[xp_spec(rev), xs_spec(rev), mod_spec, full((1, D_MODEL)), full((D_MODEL, IN_COLS)),
                  full((3, CONV_W)), full((1, CONV_W)), full((2, HEADS, DK, DK)),
                  rope_spec, rope_spec, state_in],
        out_specs=(rows(CONV_W, rev), rows(RET_W, rev), rows(RET_W, rev), rows(RET_W, rev),
                   pl.BlockSpec((RET_W, TM), lambda i: (0, rev(i))),
                   pl.BlockSpec((TM // CHUNK, HEADS, DK, DK), lambda i: (rev(i), 0, 0, 0)),
                   state_out),
        scratch_shapes=[pltpu.VMEM((HEADS, DK, DK), F32), pltpu.VMEM((2, HEADS, DK, DK), F32)],
        compiler_params=_cparams(),
        name="mix_a",
    )(xp2, xs2, mod3, g_pre, win_bf, conv_w, conv_b, dec, cos_t, sin_t, s0b)

    ident = lambda i: i
    xnew, tok, sf_fin = pl.pallas_call(
        functools.partial(_mix_b_body, tiles),
        out_shape=(jax.ShapeDtypeStruct((n_tok, D_MODEL), F32),
                   jax.ShapeDtypeStruct((n_tok, D_MODEL), F32),
                   jax.ShapeDtypeStruct((n_ctx, HEADS, DK, DK), F32)),
        grid=(tiles.n_tiles,),
        in_specs=[xp_spec(ident), xs_spec(ident), mod_spec,
                  rows(RET_W, ident),
                  pl.BlockSpec((RET_W, TM), lambda i: (0, i)),
                  rows(RET_W, ident), rows(RET_W, ident), rows(CONV_W, ident),
                  pl.BlockSpec((TM // CHUNK, HEADS, DK, DK), lambda i: (i, 0, 0, 0)),
                  full((D_MODEL, D_MODEL)), full((1, D_MODEL)), full((1, D_MODEL)), full((1, RET_W)),
                  full((2, HEADS, DK, DK)), state_in],
        out_specs=(rows(D_MODEL, ident), rows(D_MODEL, ident), state_out),
        scratch_shapes=[pltpu.VMEM((HEADS, DK, DK), F32), pltpu.VMEM((5, HEADS, DK, DK), F32),
                        pltpu.VMEM((TM, D_MODEL), BF16)],
        compiler_params=_cparams(),
        name="mix_b",
    )(xp2, xs2, mod3, q, kt, v, g, yconv, sbin, wout_bf, g_post, g_ffn, gn_g, dec, s0f)
    return xnew, tok, sf_fin, sb_fin


def _route_body(tok_ref, rwt_ref, bias_ref, topi_ref, topw_ref):
    h = tok_ref[...].astype(BF16)
    logits = lax.dot_general(rwt_ref[...], h, (((1,), (1,)), ((), ())), preferred_element_type=F32)
    shape3 = (GROUP_SIZE, N_GROUPS, 128)
    member = lax.broadcasted_iota(I32, shape3, 0)
    group = lax.broadcasted_iota(I32, shape3, 1)
    expert = group * GROUP_SIZE + member
    group2 = lax.broadcasted_iota(I32, (N_GROUPS, 128), 0)
    neg = -jnp.inf
    for lb in range(RT // 128):
        scores = jax.nn.sigmoid(logits[:, lb * 128:(lb + 1) * 128]).reshape(shape3)
        biased = scores + bias_ref[...].reshape(shape3)
        m1 = jnp.max(biased, axis=0)
        first = jnp.min(jnp.where(biased == m1, member, GROUP_SIZE), axis=0)
        m2 = jnp.max(jnp.where(member == first, neg, biased), axis=0)
        gs = m1 + m2
        beaten = jnp.zeros(gs.shape, I32)
        for s in range(1, N_GROUPS):
            other = pltpu.roll(gs, s, 0)
            wins = (other > gs) | ((other == gs) & (group2 >= s))
            beaten = beaten + wins.astype(I32)
        keep = beaten < TOPK_GROUPS
        cand = jnp.where(keep, biased, neg)
        idx_rows, w_rows = [], []
        for _ in range(TOP_K):
            best = jnp.max(jnp.max(cand, axis=0), axis=0, keepdims=True)
            pick = jnp.min(jnp.min(jnp.where(cand == best, expert, N_EXPERTS), axis=0), axis=0, keepdims=True)
            hit = expert == pick
            w_rows.append(jnp.sum(jnp.sum(jnp.where(hit, scores, 0.0), axis=0), axis=0, keepdims=True))
            idx_rows.append(pick)
            cand = jnp.where(hit, neg, cand)
        w = jnp.concatenate(w_rows, axis=0)
        topi_ref[:, lb * 128:(lb + 1) * 128] = jnp.concatenate(idx_rows, axis=0)
        topw_ref[:, lb * 128:(lb + 1) * 128] = w / jnp.sum(w, axis=0, keepdims=True) * ROUTED_SCALE


def _route(tok, rwt_bf, bias_b):
    n_tok = tok.shape[0]
    return pl.pallas_call(
        _route_body,
        out_shape=(jax.ShapeDtypeStruct((TOP_K, n_tok), I32), jax.ShapeDtypeStruct((TOP_K, n_tok), F32)),
        grid=(n_tok // RT,),
        in_specs=[pl.BlockSpec((RT, D_MODEL), lambda i: (i, 0)),
                  pl.BlockSpec((N_EXPERTS, D_MODEL), lambda i: (0, 0)),
                  pl.BlockSpec((N_EXPERTS, 128), lambda i: (0, 0))],
        out_specs=(pl.BlockSpec((TOP_K, RT), lambda i: (0, i)), pl.BlockSpec((TOP_K, RT), lambda i: (0, i))),
        compiler_params=_cparams(),
        name="route",
    )(tok, rwt_bf, bias_b)


def _onehot(ids_row):
    e = lax.broadcasted_iota(I32, (N_EXPERTS, 256), 0)
    return e == ids_row


def _rank_body(topi_ref, rank_ref, counts_ref, run_scr):
    i = pl.program_id(0)

    @pl.when(i == 0)
    def _():
        run_scr[...] = jnp.zeros(run_scr.shape, F32)

    a0 = lax.broadcasted_iota(I32, (256, 256), 0)
    a1 = lax.broadcasted_iota(I32, (256, 256), 1)
    upper = (a0 <= a1).astype(BF16)
    ones = jnp.ones((256, 256), BF16)
    for k in range(TOP_K):
        for sb in range(RB // 256):
            lanes = slice(sb * 256, (sb + 1) * 256)
            oh = _onehot(topi_ref[k:k + 1, lanes])
            ohb = oh.astype(BF16)
            seen = _dot(ohb, upper) + run_scr[...]
            r = jnp.sum(jnp.where(oh, seen, 0.0), axis=0, keepdims=True) - 1.0
            rank_ref[k:k + 1, lanes] = r.astype(I32)
            run_scr[...] = run_scr[...] + _dot(ohb, ones)

    @pl.when(i == pl.num_programs(0) - 1)
    def _():
        counts_ref[...] = run_scr[:, 0:128]


def _dest_body(topi_ref, rank_ref, counts_ref, dest_ref, start_ref, start_scr):
    i = pl.program_id(0)

    @pl.when(i == 0)
    def _():
        nb = jnp.floor((counts_ref[...] + float(SUB - 1)) / float(SUB))
        hi = jnp.floor(nb / 16.0)
        lo = nb - hi * 16.0
        e0 = lax.broadcasted_iota(I32, (N_EXPERTS, N_EXPERTS), 0)
        e1 = lax.broadcasted_iota(I32, (N_EXPERTS, N_EXPERTS), 1)
        below = (e1 < e0).astype(BF16)
        first_blk = 16.0 * _dot(below, hi.astype(BF16)) + _dot(below, lo.astype(BF16))
        start_scr[...] = first_blk * float(SUB)
        start_ref[...] = start_scr[...]

    start = jnp.concatenate([start_scr[...], start_scr[...]], axis=1)
    for k in range(TOP_K):
        for sb in range(RB // 256):
            lanes = slice(sb * 256, (sb + 1) * 256)
            oh = _onehot(topi_ref[k:k + 1, lanes])
            base = jnp.sum(jnp.where(oh, start, 0.0), axis=0, keepdims=True)
            dest_ref[k:k + 1, lanes] = base.astype(I32) + rank_ref[k:k + 1, lanes]


def _dispatch_plan(topi):
    n_tok = topi.shape[1]
    blk = pl.BlockSpec((TOP_K, RB), lambda i: (0, i))
    whole = pl.BlockSpec((N_EXPERTS, 128), lambda i: (0, 0))
    rank, counts = pl.pallas_call(
        _rank_body,
        out_shape=(jax.ShapeDtypeStruct((TOP_K, n_tok), I32), jax.ShapeDtypeStruct((N_EXPERTS, 128), F32)),
        grid=(n_tok // RB,),
        in_specs=[blk],
        out_specs=(blk, whole),
        scratch_shapes=[pltpu.VMEM((N_EXPERTS, 256), F32)],
        compiler_params=_cparams(),
        name="rank",
    )(topi)
    dest, start = pl.pallas_call(
        _dest_body,
        out_shape=(jax.ShapeDtypeStruct((TOP_K, n_tok), I32), jax.ShapeDtypeStruct((N_EXPERTS, 128), F32)),
        grid=(n_tok // RB,),
        in_specs=[blk, blk, whole],
        out_specs=(blk, whole),
        scratch_shapes=[pltpu.VMEM((N_EXPERTS, 128), F32)],
        compiler_params=_cparams(),
        name="dest",
    )(topi, rank, counts)
    return dest, start, counts


ROW_TILES = D_MODEL // 128


def _rows_from_tiles(ref, lead, n_rows):
    return jnp.concatenate([ref[lead + (pl.ds(c, n_rows, stride=ROW_TILES), slice(None))]
                            for c in range(ROW_TILES)], axis=1)


def _rows_to_tiles(ref, lead, value):
    n_rows = value.shape[0]
    for c in range(ROW_TILES):
        ref[lead + (pl.ds(c, n_rows, stride=ROW_TILES), slice(None))] = value[:, c * 128:(c + 1) * 128]


def _start_row_gather(src_hbm, idx_of, dst, sem, n_rows, priority_of=lambda r: r % 2):
    for r in range(n_rows):
        src_row = pl.multiple_of(idx_of(r) * ROW_TILES, ROW_TILES)
        pltpu.make_async_copy(src_hbm.at[pl.ds(src_row, ROW_TILES)], dst.at[pl.ds(r * ROW_TILES, ROW_TILES)],
                              sem).start(priority=priority_of(r))


INVERT_UNROLL = 16


def _invert_body(trips_ref, dest_hbm, zeros_hbm, out_hbm, dbuf, obuf, sem):
    k = pl.program_id(0)
    n_tok = dbuf.shape[0]

    @pl.when(k == 0)
    def _():
        init = pltpu.make_async_copy(zeros_hbm, obuf, sem.at[1])
        init.start()
        init.wait()

    fetch = pltpu.make_async_copy(dest_hbm.at[pl.ds(pl.multiple_of(k * n_tok, 128), n_tok)], dbuf, sem.at[0])
    fetch.start()
    fetch.wait()

    def body(tb, carry):
        for u in range(INVERT_UNROLL):
            t = tb * INVERT_UNROLL + u
            obuf[dbuf[t]] = t
        return carry

    lax.fori_loop(0, trips_ref[0], body, 0)

    @pl.when(k == pl.num_programs(0) - 1)
    def _():
        done = pltpu.make_async_copy(obuf, out_hbm, sem.at[1])
        done.start()
        done.wait()


def _invert(dest_flat, n_tok, n_slots):
    trips = jnp.full((1,), n_tok // INVERT_UNROLL, I32)
    return pl.pallas_call(
        _invert_body,
        out_shape=jax.ShapeDtypeStruct((n_slots,), I32),
        grid_spec=pltpu.PrefetchScalarGridSpec(
            num_scalar_prefetch=1,
            grid=(TOP_K,),
            in_specs=[pl.BlockSpec(memory_space=pl.ANY), pl.BlockSpec(memory_space=pl.ANY)],
            out_specs=pl.BlockSpec(memory_space=pl.ANY),
            scratch_shapes=[pltpu.SMEM((n_tok,), I32), pltpu.SMEM((n_slots,), I32),
                            pltpu.SemaphoreType.DMA((2,))]),
        compiler_params=_cparams(),
        name="invert",
    )(trips, dest_flat, jnp.zeros((n_slots,), I32))


N_XBUF = 4
LOOKAHEAD = 3
GATHER_PRIORITY = 0
BULK_PRIORITY = 1


def _experts_body(start_ref, nsub_ref, slot_tok_ref,
                  tok_hbm, wg_hbm, wu_hbm, wd_hbm, y_hbm,
                  xbuf, ybuf, wg_f32, wu_f32, wd_f32, wg_bf, wu_bf, wd_bf, cur, gsem, osem, wsem):
    e = pl.program_id(0)
    n_e = pl.num_programs(0)
    nsub = nsub_ref[e]
    sub_rows = SUB * ROW_TILES

    def weight_copies(ex, slot):
        return [pltpu.make_async_copy(src.at[ex], dst.at[slot], wsem.at[slot, n])
                for n, (src, dst) in enumerate(((wg_hbm, wg_f32), (wu_hbm, wu_f32), (wd_hbm, wd_f32)))]

    def next_nonempty(e0):
        return lax.while_loop(lambda x: jnp.logical_and(x < n_e, nsub_ref[jnp.minimum(x, n_e - 1)] == 0),
                              lambda x: x + 1, e0)

    def produce():
        pe = cur[0]

        @pl.when(pe < n_e)
        def _():
            pj = cur[1]
            pg = cur[2]
            base = start_ref[pe] + pj * SUB
            slot = lax.rem(pg, N_XBUF)
            _start_row_gather(tok_hbm, lambda r: slot_tok_ref[base + r], xbuf.at[slot], gsem.at[slot], SUB,
                              priority_of=lambda r: GATHER_PRIORITY)
            last = pj + 1 >= nsub_ref[pe]
            cur[0] = jnp.where(last, next_nonempty(pe + 1), pe)
            cur[1] = jnp.where(last, 0, pj + 1)
            cur[2] = pg + 1

    def out_wait(slot):
        pltpu.make_async_copy(ybuf.at[slot], y_hbm.at[pl.ds(0, sub_rows)], osem.at[slot]).wait()

    @pl.when(e == 0)
    def _():
        cur[0] = next_nonempty(0)
        cur[1] = 0
        cur[2] = 0
        cur[3] = 0
        for cp in weight_copies(0, 0):
            cp.start(priority=BULK_PRIORITY)
        for _ in range(LOOKAHEAD):
            produce()

    wslot = e & 1

    @pl.when(e + 1 < n_e)
    def _():
        for cp in weight_copies(e + 1, 1 - wslot):
            cp.start(priority=BULK_PRIORITY)

    for cp in weight_copies(e, wslot):
        cp.wait()
    wg_bf[...] = wg_f32[wslot].astype(BF16)
    wu_bf[...] = wu_f32[wslot].astype(BF16)
    wd_bf[...] = wd_f32[wslot].astype(BF16)

    def step(j, carry):
        g = cur[3]
        produce()
        slot = lax.rem(g, N_XBUF)
        yslot = g & 1
        pltpu.make_async_copy(tok_hbm.at[pl.ds(0, sub_rows)], xbuf.at[slot], gsem.at[slot]).wait()

        @pl.when(g >= 2)
        def _():
            out_wait(yslot)

        xb = _rows_from_tiles(xbuf, (slot,), SUB).astype(BF16)
        a = _dot(xb, wg_bf[...])
        b = _dot(xb, wu_bf[...])
        _rows_to_tiles(ybuf, (yslot,), _dot((_silu(a) * b).astype(BF16), wd_bf[...]))
        row0 = pl.multiple_of((start_ref[e] + j * SUB) * ROW_TILES, sub_rows)
        pltpu.make_async_copy(ybuf.at[yslot], y_hbm.at[pl.ds(row0, sub_rows)],
                              osem.at[yslot]).start(priority=BULK_PRIORITY)
        cur[3] = g + 1
        return carry

    lax.fori_loop(0, nsub, step, 0)

    @pl.when(e == n_e - 1)
    def _():
        total = cur[3]

        @pl.when(total >= 2)
        def _():
            out_wait(total & 1)

        @pl.when(total >= 1)
        def _():
            out_wait((total - 1) & 1)

        ybuf[0] = jnp.zeros(ybuf.shape[1:], F32)
        used = start_ref[e] // SUB + nsub

        def tail_copy(sb):
            return pltpu.make_async_copy(ybuf.at[0], y_hbm.at[pl.ds(pl.multiple_of(sb * sub_rows, sub_rows), sub_rows)],
                                         osem.at[0])

        n_sub_total = y_hbm.shape[0] // sub_rows
        lax.fori_loop(used, n_sub_total, lambda sb, c: (tail_copy(sb).start(), c)[1], 0)
        lax.fori_loop(used, n_sub_total, lambda sb, c: (tail_copy(sb).wait(), c)[1], 0)


def _experts(start, nsub, slot_tok, tok_tiles, wg, wu, wd, n_slots):
    sub_rows = SUB * ROW_TILES
    return pl.pallas_call(
        _experts_body,
        out_shape=jax.ShapeDtypeStruct((n_slots * ROW_TILES, 128), F32),
        grid_spec=pltpu.PrefetchScalarGridSpec(
            num_scalar_prefetch=3,
            grid=(N_EXPERTS,),
            in_specs=[pl.BlockSpec(memory_space=pl.ANY)] * 4,
            out_specs=pl.BlockSpec(memory_space=pl.ANY),
            scratch_shapes=[pltpu.VMEM((N_XBUF, sub_rows, 128), F32), pltpu.VMEM((2, sub_rows, 128), F32),
                            pltpu.VMEM((2, D_MODEL, FF), F32), pltpu.VMEM((2, D_MODEL, FF), F32),
                            pltpu.VMEM((2, FF, D_MODEL), F32),
                            pltpu.VMEM((D_MODEL, FF), BF16), pltpu.VMEM((D_MODEL, FF), BF16),
                            pltpu.VMEM((FF, D_MODEL), BF16), pltpu.SMEM((4,), I32),
                            pltpu.SemaphoreType.DMA((N_XBUF,)), pltpu.SemaphoreType.DMA((2,)),
                            pltpu.SemaphoreType.DMA((2, 3))]),
        compiler_params=_cparams(),
        name="experts",
    )(start, nsub, slot_tok, tok_tiles, wg, wu, wd)


def _combine_body(n_tok, n_ctx_tiles, dest_ref,
                  y_hbm, topw_ref, tok_ref, xnew_ref, mod_ref, sg_ref, su_ref, sd_ref, gpost_ref,
                  outp_ref, outs_ref, gbuf, wcol, fbuf, gsem):
    i = pl.program_id(0)
    n = pl.num_programs(0)

    def gather(tile, slot):
        def per_choice(k, carry):
            base = k * n_tok + tile * TT
            _start_row_gather(y_hbm, lambda r: dest_ref[base + r], gbuf.at[slot, k], gsem.at[slot], TT)
            return carry
        lax.fori_loop(0, TOP_K, per_choice, 0)

    def gather_wait(slot):
        for k in range(TOP_K):
            pltpu.make_async_copy(y_hbm.at[pl.ds(0, TT * ROW_TILES)], gbuf.at[slot, k], gsem.at[slot]).wait()

    slot = i & 1

    @pl.when(i == 0)
    def _():
        gather(0, 0)

    @pl.when(i + 1 < n)
    def _():
        gather(i + 1, 1 - slot)

    w_t = jnp.concatenate([topw_ref[...], jnp.zeros((128 - TOP_K, TT), F32)], axis=0).T
    for k in range(TOP_K):
        wcol[k] = jnp.broadcast_to(w_t[:, k:k + 1], (TT, 128))
    h = tok_ref[...].astype(BF16)
    fbuf[...] = _dot((_silu(_dot(h, sg_ref[...])) * _dot(h, su_ref[...])).astype(BF16), sd_ref[...])

    gather_wait(slot)
    sq = jnp.zeros((TT, 1), F32)
    for c in range(ROW_TILES):
        cols = slice(c * 128, (c + 1) * 128)
        f = fbuf[:, cols]
        for k in range(TOP_K):
            f = f + gbuf[slot, k, pl.ds(c, TT, stride=ROW_TILES), :] * wcol[k]
        sq = sq + jnp.sum(f * f, axis=-1, keepdims=True)
        fbuf[:, cols] = f
    normed = fbuf[...] * lax.rsqrt(sq / float(D_MODEL) + EPS) * gpost_ref[...]
    out = xnew_ref[...] + mod_ref[0, 5:6, :] * normed

    @pl.when(i < n_ctx_tiles)
    def _():
        outp_ref[...] = out

    @pl.when(i >= n_ctx_tiles)
    def _():
        outs_ref[...] = out


def _combine(tiles, dest_flat, ysorted, topw, tok, xnew, mod3, sg_bf, su_bf, sd_bf, g_post):
    n_tok = tok.shape[0]
    n_ctx_tok = tiles.n_ctx * TM
    n_ctx_tiles = n_ctx_tok // TT
    lat_tiles_per_seq = tiles.lat_tiles * TM // TT

    def mod_row(i):
        return jnp.where(i < n_ctx_tiles, 0, 1 + jnp.maximum(i - n_ctx_tiles, 0) // lat_tiles_per_seq)

    def full(shape):
        return pl.BlockSpec(shape, lambda i, *_: (0,) * len(shape))

    rows = pl.BlockSpec((TT, D_MODEL), lambda i, *_: (i, 0))
    return pl.pallas_call(
        functools.partial(_combine_body, n_tok, n_ctx_tiles),
        out_shape=(jax.ShapeDtypeStruct((n_ctx_tok, D_MODEL), F32),
                   jax.ShapeDtypeStruct((n_tok - n_ctx_tok, D_MODEL), F32)),
        grid_spec=pltpu.PrefetchScalarGridSpec(
            num_scalar_prefetch=1,
            grid=(n_tok // TT,),
            in_specs=[pl.BlockSpec(memory_space=pl.ANY),
                      pl.BlockSpec((TOP_K, TT), lambda i, *_: (0, i)),
                      rows, rows,
                      pl.BlockSpec((1, 6, D_MODEL), lambda i, *_: (mod_row(i), 0, 0)),
                      full((D_MODEL, FF)), full((D_MODEL, FF)), full((FF, D_MODEL)), full((1, D_MODEL))],
            out_specs=(pl.BlockSpec((TT, D_MODEL), lambda i, *_: (jnp.minimum(i, n_ctx_tiles - 1), 0)),
                       pl.BlockSpec((TT, D_MODEL), lambda i, *_: (jnp.maximum(i - n_ctx_tiles, 0), 0))),
            scratch_shapes=[pltpu.VMEM((2, TOP_K, TT * ROW_TILES, 128), F32), pltpu.VMEM((TOP_K, TT, 128), F32),
                            pltpu.VMEM((TT, D_MODEL), F32), pltpu.SemaphoreType.DMA((2,))]),
        compiler_params=_cparams(),
        name="combine",
    )(dest_flat, ysorted, topw, tok, xnew, mod3, sg_bf, su_bf, sd_bf, g_post)


def _rope_tables(lat_len, lat_tiles):
    rows = lat_len // GRID_W
    row = jnp.repeat(jnp.arange(rows, dtype=F32), GRID_W)
    col = jnp.tile(jnp.arange(GRID_W, dtype=F32), rows)
    inv = ROPE_BASE ** (-jnp.arange(ROPE_FREQS, dtype=F32) / ROPE_FREQS)
    ang = jnp.concatenate([row[:, None] * inv[None, :]] * 2 + [col[:, None] * inv[None, :]] * 2, axis=1)
    sign = jnp.tile(jnp.concatenate([-jnp.ones((ROPE_FREQS,), F32), jnp.ones((ROPE_FREQS,), F32)]), 2)
    cos = jnp.cos(ang).reshape(lat_tiles, TM, DK)
    sin = (jnp.sin(ang) * sign[None, :]).reshape(lat_tiles, TM, DK)
    cos = jnp.concatenate([jnp.ones((1, TM, DK), F32), cos], axis=0)
    sin = jnp.concatenate([jnp.zeros((1, TM, DK), F32), sin], axis=0)
    return cos, sin


def kernel(x_prompt, x_sample, state_ret_fwd, state_ret_bwd, c, c_ctx, w_mod, b_mod, norm_mix_pre,
           norm_mix_post, norm_ffn_pre, norm_ffn_post, w_in, conv_w, conv_b, ret_decay_fwd,
           ret_decay_bwd, ret_gn_g, w_out, router_w, router_bias, expert_w_gate, expert_w_up,
           expert_w_down, shared_w_gate, shared_w_up, shared_w_down):
    bp, tp, d = x_prompt.shape
    bs, ts, _ = x_sample.shape
    depth = w_mod.shape[0]
    assert d == D_MODEL and tp == TM and ts % TM == 0 and bs + 1 <= 8
    tiles = _Tiles(bp, bs, ts)
    n_tok = tiles.n_tiles * TM
    n_slots = n_tok * TOP_K + N_EXPERTS * SUB
    cos_t, sin_t = _rope_tables(ts, tiles.lat_tiles)

    perm = (jnp.arange(N_EXPERTS) % N_GROUPS) * GROUP_SIZE + jnp.arange(N_EXPERTS) // N_GROUPS

    xp2 = x_prompt.reshape(bp * tp, d)
    xs2 = x_sample.reshape(bs * ts, d)
    new_f, new_b = [], []
    for l in range(depth):
        c_rows = jnp.concatenate([c_ctx[None, :], c, jnp.zeros((8 - 1 - bs, d), F32)], axis=0)
        mod3 = _modulation(c_rows, w_mod[l], b_mod[l][None, :]).reshape(8, 6, d)
        dec = jnp.broadcast_to(jnp.stack([ret_decay_fwd[l], ret_decay_bwd[l]])[:, :, None, None],
                               (2, HEADS, DK, DK)).astype(F32)
        xnew, tok, sf_fin, sb_fin = _token_mixer(
            tiles, xp2, xs2, mod3, norm_mix_pre[l][None, :], w_in[l].astype(BF16), conv_w[l],
            conv_b[l][None, :], dec, cos_t, sin_t, state_ret_fwd[:, l], state_ret_bwd[:, l],
            w_out[l].astype(BF16), norm_mix_post[l][None, :], norm_ffn_pre[l][None, :], ret_gn_g[l][None, :])
        new_f.append(sf_fin)
        new_b.append(sb_fin)

        rwt = router_w[l].T[perm].astype(BF16)
        bias_b = jnp.broadcast_to(router_bias[l][perm][:, None], (N_EXPERTS, 128)).astype(F32)
        topi, topw = _route(tok, rwt, bias_b)
        dest, start, counts = _dispatch_plan(topi)
        dest_flat = dest.reshape(-1)
        slot_tok = _invert(dest_flat, n_tok, n_slots)
        start_i = start[:, 0].astype(I32)
        nsub = ((counts[:, 0] + float(SUB - 1)) / float(SUB)).astype(I32)
        tok_tiles = tok.reshape(n_tok * ROW_TILES, 128)
        ysorted = _experts(start_i, nsub, slot_tok, tok_tiles, expert_w_gate[l], expert_w_up[l],
                           expert_w_down[l], n_slots)
        xp2, xs2 = _combine(tiles, dest_flat, ysorted, topw, tok, xnew, mod3,
                            shared_w_gate[l].astype(BF16), shared_w_up[l].astype(BF16),
                            shared_w_down[l].astype(BF16), norm_ffn_post[l][None, :])

    return (xp2.reshape(bp, tp, d), xs2.reshape(bs, ts, d),
            jnp.stack(new_f, axis=1), jnp.stack(new_b, axis=1))
```

```python
import functools

import jax
import jax.numpy as jnp
from jax import lax
from jax.experimental import pallas as pl
from jax.experimental.pallas import tpu as pltpu

F32 = jnp.float32
BF16 = jnp.bfloat16
I32 = jnp.int32

D_MODEL = 1024
CONV_W = 512
RET_W = 512
HEADS = 4
DK = 128
CHUNK = 128
GRID_W = 64
ROPE_FREQS = 32
ROPE_BASE = 10000.0
IN_COLS = 3 * CONV_W + 4 * RET_W
N_EXPERTS = 256
N_GROUPS = 8
GROUP_SIZE = N_EXPERTS // N_GROUPS
TOPK_GROUPS = 4
TOP_K = 8
FF = 256
ROUTED_SCALE = 2.5
EPS = 1e-6

TM = 256
SUB = 128
TT = 128
RT = 512
RB = 2048
VMEM_LIMIT = 56 * 1024 * 1024


def _cparams(n_axes=1, vmem=VMEM_LIMIT):
    return pltpu.CompilerParams(dimension_semantics=("arbitrary",) * n_axes,
                                vmem_limit_bytes=vmem)


def _silu(x):
    return x * jax.nn.sigmoid(x)


def _log_sigmoid(x):
    return jnp.minimum(x, 0.0) - jnp.log1p(jnp.exp(-jnp.abs(x)))


def _rms(x, g):
    return x * lax.rsqrt(jnp.mean(x * x, axis=-1, keepdims=True) + EPS) * g


def _dot(a, b):
    return jnp.dot(a, b, preferred_element_type=F32)


def _mod_body(c_ref, w_ref, b_ref, o_ref):
    s = _silu(c_ref[...]).astype(BF16)
    o_ref[...] = _dot(s, w_ref[...].astype(BF16)) + b_ref[...]


def _modulation(c_rows, w_mod, b_mod):
    n_col = w_mod.shape[1]
    blk = 1536
    return pl.pallas_call(
        _mod_body,
        out_shape=jax.ShapeDtypeStruct((8, n_col), F32),
        grid=(n_col // blk,),
        in_specs=[pl.BlockSpec((8, D_MODEL), lambda i: (0, 0)),
                  pl.BlockSpec((D_MODEL, blk), lambda i: (0, i)),
                  pl.BlockSpec((1, blk), lambda i: (0, i))],
        out_specs=pl.BlockSpec((8, blk), lambda i: (0, i)),
        compiler_params=_cparams(),
        name="mod",
    )(c_rows, w_mod, b_mod)


class _Tiles:
    def __init__(self, n_ctx_seq, n_lat_seq, lat_len):
        self.n_ctx = n_ctx_seq
        self.lat_tiles = lat_len // TM
        self.n_lat_seq = n_lat_seq
        self.n_tiles = n_ctx_seq + n_lat_seq * self.lat_tiles

    def is_ctx(self, i):
        return i < self.n_ctx

    def lat_pos(self, i):
        j = jnp.maximum(i - self.n_ctx, 0)
        return j // self.lat_tiles, j % self.lat_tiles

    def phys_reversed(self, i):
        b, t = self.lat_pos(i)
        return jnp.where(i < self.n_ctx, i, self.n_ctx + b * self.lat_tiles + (self.lat_tiles - 1 - t))

    def mod_row(self, i):
        b, _ = self.lat_pos(i)
        return jnp.where(i < self.n_ctx, 0, 1 + b)


def _rope(x, cos, sin_signed):
    lane = lax.broadcasted_iota(I32, x.shape, 1)
    partner = jnp.where((lane & 63) < 32, pltpu.roll(x, 96, 1), pltpu.roll(x, 32, 1))
    return x * cos + partner * sin_signed


def _mix_a_body(tiles, xp_ref, xs_ref, mod_ref, gpre_ref, win_ref, cw_ref, cb_ref, dec_ref,
                cos_ref, sin_ref, s0b_ref,
                yconv_ref, q_ref, v_ref, g_ref, kt_ref, sbin_ref, sbfin_ref,
                sb_scr, tab_scr):
    i = pl.program_id(0)
    is_ctx = tiles.is_ctx(i)
    _, t_rev = tiles.lat_pos(i)
    first = jnp.logical_or(is_ctx, t_rev == 0)

    @pl.when(i == 0)
    def _():
        lg = _log_sigmoid(dec_ref[1])
        col = lax.broadcasted_iota(I32, lg.shape, 2).astype(F32)
        tab_scr[0] = jnp.exp(col * lg)
        tab_scr[1] = jnp.exp(float(CHUNK) * lg)

    @pl.when(first)
    def _():
        sb_scr[...] = jnp.where(is_ctx, 0.0, s0b_ref[0])

    x = jnp.where(is_ctx, xp_ref[...], xs_ref[...])
    h = (_rms(x, gpre_ref[...]) * (1.0 + mod_ref[0, 1:2, :]) + mod_ref[0, 0:1, :]).astype(BF16)

    def proj(k):
        return _dot(h, win_ref[:, k * 512:(k + 1) * 512])

    z = proj(1) * proj(2)
    row = lax.broadcasted_iota(I32, z.shape, 0)
    period = jnp.where(is_ctx, TM, GRID_W)
    pos = row & (period - 1)
    left = jnp.where(pos == 0, 0.0, pltpu.roll(z, 1, 0))
    right = jnp.where(pos == period - 1, 0.0, pltpu.roll(z, TM - 1, 0))
    zc = left * cw_ref[0:1, :] + z * cw_ref[1:2, :] + right * cw_ref[2:3, :] + cb_ref[...]
    yconv_ref[...] = (proj(0) * zc).astype(BF16)

    cos = cos_ref[0]
    sin = sin_ref[0]
    q = proj(3)
    k = proj(4)
    q = jnp.concatenate([_rope(q[:, hh * DK:(hh + 1) * DK], cos, sin) for hh in range(HEADS)], axis=1)
    k = jnp.concatenate([_rope(k[:, hh * DK:(hh + 1) * DK], cos, sin) for hh in range(HEADS)], axis=1)
    q_ref[...] = (q * (DK ** -0.5)).astype(BF16)
    kt = k.T
    kt_ref[...] = kt.astype(BF16)
    v = proj(5).astype(BF16)
    v_ref[...] = v
    g_ref[...] = proj(6)

    for c in (1, 0):
        for hh in range(HEADS):
            sbin_ref[c, hh] = sb_scr[hh].astype(BF16)
            kts = (kt[hh * DK:(hh + 1) * DK, c * CHUNK:(c + 1) * CHUNK] * tab_scr[0, hh]).astype(BF16)
            vc = v[c * CHUNK:(c + 1) * CHUNK, hh * DK:(hh + 1) * DK]
            sb_scr[hh] = sb_scr[hh] * tab_scr[1, hh] + _dot(kts, vc)

    @pl.when(is_ctx)
    def _():
        sbfin_ref[0] = sb_scr[...]


def _mix_b_body(tiles, xp_ref, xs_ref, mod_ref, q_ref, kt_ref, v_ref, g_ref, yconv_ref, sbin_ref,
                wout_ref, gpost_ref, gffn_ref, gn_ref, dec_ref, s0f_ref,
                xnew_ref, tok_ref, sffin_ref,
                sf_scr, tab_scr, ycat_scr):
    i = pl.program_id(0)
    is_ctx = tiles.is_ctx(i)
    _, t_pos = tiles.lat_pos(i)
    first = jnp.logical_or(is_ctx, t_pos == 0)

    @pl.when(i == 0)
    def _():
        lgf = _log_sigmoid(dec_ref[0])
        lgb = _log_sigmoid(dec_ref[1])
        row = lax.broadcasted_iota(I32, lgf.shape, 1)
        col = lax.broadcasted_iota(I32, lgf.shape, 2)
        d = (row - col).astype(F32)
        tab_scr[0] = (jnp.where(row >= col, jnp.exp(jnp.where(row >= col, d, 0.0) * lgf), 0.0)
                      + jnp.where(col >= row, jnp.exp(jnp.where(col >= row, -d, 0.0) * lgb), 0.0))
        tab_scr[1] = jnp.exp((row + 1).astype(F32) * lgf)
        tab_scr[2] = jnp.exp((CHUNK - row).astype(F32) * lgb)
        tab_scr[3] = jnp.exp((CHUNK - 1 - col).astype(F32) * lgf)
        tab_scr[4] = jnp.exp(float(CHUNK) * lgf)

    @pl.when(first)
    def _():
        sf_scr[...] = jnp.where(is_ctx, 0.0, s0f_ref[0])

    for c in range(TM // CHUNK):
        rows = slice(c * CHUNK, (c + 1) * CHUNK)
        for hh in range(HEADS):
            cols = slice(hh * DK, (hh + 1) * DK)
            qc = q_ref[rows, cols]
            ktc = kt_ref[cols, rows]
            vc = v_ref[rows, cols]
            att = (_dot(qc, ktc) * tab_scr[0, hh]).astype(BF16)
            o = (_dot(att, vc)
                 + tab_scr[1, hh] * _dot(qc, sf_scr[hh].astype(BF16))
                 + tab_scr[2, hh] * _dot(qc, sbin_ref[c, hh]))
            kts = (ktc.astype(F32) * tab_scr[3, hh]).astype(BF16)
            sf_scr[hh] = sf_scr[hh] * tab_scr[4, hh] + _dot(kts, vc)
            mu = jnp.mean(o, axis=-1, keepdims=True)
            dev = o - mu
            var = jnp.mean(dev * dev, axis=-1, keepdims=True)
            on = dev * lax.rsqrt(var + EPS) * gn_ref[:, cols]
            ycat_scr[rows, RET_W + hh * DK:RET_W + (hh + 1) * DK] = (_silu(g_ref[rows, cols]) * on).astype(BF16)
    ycat_scr[:, 0:CONV_W] = yconv_ref[...]

    @pl.when(is_ctx)
    def _():
        sffin_ref[0] = sf_scr[...]

    x = jnp.where(is_ctx, xp_ref[...], xs_ref[...])
    u = _dot(ycat_scr[...], wout_ref[...])
    xn = x + mod_ref[0, 2:3, :] * _rms(u, gpost_ref[...])
    xnew_ref[...] = xn
    tok_ref[...] = _rms(xn, gffn_ref[...]) * (1.0 + mod_ref[0, 4:5, :]) + mod_ref[0, 3:4, :]


def _token_mixer(tiles, xp2, xs2, mod3, g_pre, win_bf, conv_w, conv_b, dec, cos_t, sin_t,
                 s0f, s0b, wout_bf, g_post, g_ffn, gn_g):
    n_tok = tiles.n_tiles * TM
    n_ctx = tiles.n_ctx
    last_ctx = n_ctx - 1

    def full(shape):
        return pl.BlockSpec(shape, lambda i: (0,) * len(shape))

    def xp_spec(phys):
        return pl.BlockSpec((TM, D_MODEL), lambda i: (jnp.minimum(phys(i), last_ctx), 0))

    def xs_spec(phys):
        return pl.BlockSpec((TM, D_MODEL), lambda i: (jnp.maximum(phys(i) - n_ctx, 0), 0))

    mod_spec = pl.BlockSpec((1, 6, D_MODEL), lambda i: (tiles.mod_row(i), 0, 0))
    state_in = pl.BlockSpec((1, HEADS, DK, DK), lambda i: (tiles.lat_pos(i)[0], 0, 0, 0))
    state_out = pl.BlockSpec((1, HEADS, DK, DK), lambda i: (jnp.minimum(i, last_ctx), 0, 0, 0))

    rev = tiles.phys_reversed

    def rope_idx(i):
        _, t = tiles.lat_pos(i)
        return jnp.where(i < n_ctx, 0, 1 + (tiles.lat_tiles - 1 - t))

    rope_spec = pl.BlockSpec((1, TM, DK), lambda i: (rope_idx(i), 0, 0))

    def rows(width, phys):
        return pl.BlockSpec((TM, width), lambda i: (phys(i), 0))

    yconv, q, v, g, kt, sbin, sb_fin = pl.pallas_call(
        functools.partial(_mix_a_body, tiles),
        out_shape=(jax.ShapeDtypeStruct((n_tok, CONV_W), BF16),
                   jax.ShapeDtypeStruct((n_tok, RET_W), BF16),
                   jax.ShapeDtypeStruct((n_tok, RET_W), BF16),
                   jax.ShapeDtypeStruct((n_tok, RET_W), F32),
                   jax.ShapeDtypeStruct((RET_W, n_tok), BF16),
                   jax.ShapeDtypeStruct((n_tok // CHUNK, HEADS, DK, DK), BF16),
                   jax.ShapeDtypeStruct((n_ctx, HEADS, DK, DK), F32)),
        grid=(tiles.n_tiles,),
        in_specs=[xp_spec(rev), xs_spec(rev), mod_spec, full((1, D_MODEL)), full((D_MODEL, IN_COLS)),
                  full((3, CONV_W)), full((1, CONV_W)), full((2, HEADS, DK, DK)),
                  rope_spec, rope_spec, state_in],
        out_specs=(rows(CONV_W, rev), rows(RET_W, rev), rows(RET_W, rev), rows(RET_W, rev),
                   pl.BlockSpec((RET_W, TM), lambda i: (0, rev(i))),
                   pl.BlockSpec((TM // CHUNK, HEADS, DK, DK), lambda i: (rev(i), 0, 0, 0)),
                   state_out),
        scratch_shapes=[pltpu.VMEM((HEADS, DK, DK), F32), pltpu.VMEM((2, HEADS, DK, DK), F32)],
        compiler_params=_cparams(),
        name="mix_a",
    )(xp2, xs2, mod3, g_pre, win_bf, conv_w, conv_b, dec, cos_t, sin_t, s0b)

    ident = lambda i: i
    xnew, tok, sf_fin = pl.pallas_call(
        functools.partial(_mix_b_body, tiles),
        out_shape=(jax.ShapeDtypeStruct((n_tok, D_MODEL), F32),
                   jax.ShapeDtypeStruct((n_tok, D_MODEL), F32),
                   jax.ShapeDtypeStruct((n_ctx, HEADS, DK, DK), F32)),
        grid=(tiles.n_tiles,),
        in_specs=[xp_spec(ident), xs_spec(ident), mod_spec,
                  rows(RET_W, ident),
                  pl.BlockSpec((RET_W, TM), lambda i: (0, i)),
                  rows(RET_W, ident), rows(RET_W, ident), rows(CONV_W, ident),
                  pl.BlockSpec((TM // CHUNK, HEADS, DK, DK), lambda i: (i, 0, 0, 0)),
                  full((D_MODEL, D_MODEL)), full((1, D_MODEL)), full((1, D_MODEL)), full((1, RET_W)),
                  full((2, HEADS, DK, DK)), state_in],
        out_specs=(rows(D_MODEL, ident), rows(D_MODEL, ident), state_out),
        scratch_shapes=[pltpu.VMEM((HEADS, DK, DK), F32), pltpu.VMEM((5, HEADS, DK, DK), F32),
                        pltpu.VMEM((TM, D_MODEL), BF16)],
        compiler_params=_cparams(),
        name="mix_b",
    )(xp2, xs2, mod3, q, kt, v, g, yconv, sbin, wout_bf, g_post, g_ffn, gn_g, dec, s0f)
    return xnew, tok, sf_fin, sb_fin


def _route_body(tok_ref, rwt_ref, bias_ref, topi_ref, topw_ref):
    h = tok_ref[...].astype(BF16)
    logits = lax.dot_general(rwt_ref[...], h, (((1,), (1,)), ((), ())), preferred_element_type=F32)
    shape3 = (GROUP_SIZE, N_GROUPS, 128)
    member = lax.broadcasted_iota(I32, shape3, 0)
    group = lax.broadcasted_iota(I32, shape3, 1)
    expert = group * GROUP_SIZE + member
    group2 = lax.broadcasted_iota(I32, (N_GROUPS, 128), 0)
    neg = -jnp.inf
    for lb in range(RT // 128):
        scores = jax.nn.sigmoid(logits[:, lb * 128:(lb + 1) * 128]).reshape(shape3)
        biased = scores + bias_ref[...].reshape(shape3)
        m1 = jnp.max(biased, axis=0)
        first = jnp.min(jnp.where(biased == m1, member, GROUP_SIZE), axis=0)
        m2 = jnp.max(jnp.where(member == first, neg, biased), axis=0)
        gs = m1 + m2
        beaten = jnp.zeros(gs.shape, I32)
        for s in range(1, N_GROUPS):
            other = pltpu.roll(gs, s, 0)
            wins = (other > gs) | ((other == gs) & (group2 >= s))
            beaten = beaten + wins.astype(I32)
        keep = beaten < TOPK_GROUPS
        cand = jnp.where(keep, biased, neg)
        idx_rows, w_rows = [], []
        for _ in range(TOP_K):
            best = jnp.max(jnp.max(cand, axis=0), axis=0, keepdims=True)
            pick = jnp.min(jnp.min(jnp.where(cand == best, expert, N_EXPERTS), axis=0), axis=0, keepdims=True)
            hit = expert == pick
            w_rows.append(jnp.sum(jnp.sum(jnp.where(hit, scores, 0.0), axis=0), axis=0, keepdims=True))
            idx_rows.append(pick)
            cand = jnp.where(hit, neg, cand)
        w = jnp.concatenate(w_rows, axis=0)
        topi_ref[:, lb * 128:(lb + 1) * 128] = jnp.concatenate(idx_rows, axis=0)
        topw_ref[:, lb * 128:(lb + 1) * 128] = w / jnp.sum(w, axis=0, keepdims=True) * ROUTED_SCALE


def _route(tok, rwt_bf, bias_b):
    n_tok = tok.shape[0]
    return pl.pallas_call(
        _route_body,
        out_shape=(jax.ShapeDtypeStruct((TOP_K, n_tok), I32), jax.ShapeDtypeStruct((TOP_K, n_tok), F32)),
        grid=(n_tok // RT,),
        in_specs=[pl.BlockSpec((RT, D_MODEL), lambda i: (i, 0)),
                  pl.BlockSpec((N_EXPERTS, D_MODEL), lambda i: (0, 0)),
                  pl.BlockSpec((N_EXPERTS, 128), lambda i: (0, 0))],
        out_specs=(pl.BlockSpec((TOP_K, RT), lambda i: (0, i)), pl.BlockSpec((TOP_K, RT), lambda i: (0, i))),
        compiler_params=_cparams(),
        name="route",
    )(tok, rwt_bf, bias_b)


def _onehot(ids_row):
    e = lax.broadcasted_iota(I32, (N_EXPERTS, 256), 0)
    return e == ids_row


def _rank_body(topi_ref, rank_ref, counts_ref, run_scr):
    i = pl.program_id(0)

    @pl.when(i == 0)
    def _():
        run_scr[...] = jnp.zeros(run_scr.shape, F32)

    a0 = lax.broadcasted_iota(I32, (256, 256), 0)
    a1 = lax.broadcasted_iota(I32, (256, 256), 1)
    upper = (a0 <= a1).astype(BF16)
    ones = jnp.ones((256, 256), BF16)
    for k in range(TOP_K):
        for sb in range(RB // 256):
            lanes = slice(sb * 256, (sb + 1) * 256)
            oh = _onehot(topi_ref[k:k + 1, lanes])
            ohb = oh.astype(BF16)
            seen = _dot(ohb, upper) + run_scr[...]
            r = jnp.sum(jnp.where(oh, seen, 0.0), axis=0, keepdims=True) - 1.0
            rank_ref[k:k + 1, lanes] = r.astype(I32)
            run_scr[...] = run_scr[...] + _dot(ohb, ones)

    @pl.when(i == pl.num_programs(0) - 1)
    def _():
        counts_ref[...] = run_scr[:, 0:128]


def _dest_body(topi_ref, rank_ref, counts_ref, dest_ref, start_ref, start_scr):
    i = pl.program_id(0)

    @pl.when(i == 0)
    def _():
        nb = jnp.floor((counts_ref[...] + float(SUB - 1)) / float(SUB))
        hi = jnp.floor(nb / 16.0)
        lo = nb - hi * 16.0
        e0 = lax.broadcasted_iota(I32, (N_EXPERTS, N_EXPERTS), 0)
        e1 = lax.broadcasted_iota(I32, (N_EXPERTS, N_EXPERTS), 1)
        below = (e1 < e0).astype(BF16)
        first_blk = 16.0 * _dot(below, hi.astype(BF16)) + _dot(below, lo.astype(BF16))
        start_scr[...] = first_blk * float(SUB)
        start_ref[...] = start_scr[...]

    start = jnp.concatenate([start_scr[...], start_scr[...]], axis=1)
    for k in range(TOP_K):
        for sb in range(RB // 256):
            lanes = slice(sb * 256, (sb + 1) * 256)
            oh = _onehot(topi_ref[k:k + 1, lanes])
            base = jnp.sum(jnp.where(oh, start, 0.0), axis=0, keepdims=True)
            dest_ref[k:k + 1, lanes] = base.astype(I32) + rank_ref[k:k + 1, lanes]


def _dispatch_plan(topi):
    n_tok = topi.shape[1]
    blk = pl.BlockSpec((TOP_K, RB), lambda i: (0, i))
    whole = pl.BlockSpec((N_EXPERTS, 128), lambda i: (0, 0))
    rank, counts = pl.pallas_call(
        _rank_body,
        out_shape=(jax.ShapeDtypeStruct((TOP_K, n_tok), I32), jax.ShapeDtypeStruct((N_EXPERTS, 128), F32)),
        grid=(n_tok // RB,),
        in_specs=[blk],
        out_specs=(blk, whole),
        scratch_shapes=[pltpu.VMEM((N_EXPERTS, 256), F32)],
        compiler_params=_cparams(),
        name="rank",
    )(topi)
    dest, start = pl.pallas_call(
        _dest_body,
        out_shape=(jax.ShapeDtypeStruct((TOP_K, n_tok), I32), jax.ShapeDtypeStruct((N_EXPERTS, 128), F32)),
        grid=(n_tok // RB,),
        in_specs=[blk, blk, whole],
        out_specs=(blk, whole),
        scratch_shapes=[pltpu.VMEM((N_EXPERTS, 128), F32)],
        compiler_params=_cparams(),
        name="dest",
    )(topi, rank, counts)
    return dest, start, counts


ROW_TILES = D_MODEL // 128


def _transpose8(vs):
    sub = lax.broadcasted_iota(I32, (8, 128), 0)
    for d in (4, 2, 1):
        keep = (sub & d) == 0
        out = list(vs)
        for i in range(8):
            if i & d == 0:
                a, b = vs[i], vs[i + d]
                out[i] = jnp.where(keep, a, pltpu.roll(b, d, 0))
                out[i + d] = jnp.where(keep, pltpu.roll(a, 8 - d, 0), b)
        vs = out
    return vs


def _rows_from_tiles(tiles):
    n_rows = tiles.shape[0] // ROW_TILES
    groups = [_transpose8([tiles[(g * 8 + r) * ROW_TILES:(g * 8 + r + 1) * ROW_TILES] for r in range(8)])
              for g in range(n_rows // 8)]
    return jnp.concatenate([jnp.concatenate([grp[c] for grp in groups], axis=0) for c in range(ROW_TILES)], axis=1)


def _rows_to_tiles(value):
    n_rows = value.shape[0]
    pieces = []
    for g in range(n_rows // 8):
        pieces += _transpose8([value[g * 8:(g + 1) * 8, c * 128:(c + 1) * 128] for c in range(ROW_TILES)])
    return jnp.concatenate(pieces, axis=0)


def _start_row_gather(src_hbm, idx_of, dst, sem, n_rows, priority_of=lambda r: r % 2):
    for r in range(n_rows):
        src_row = pl.multiple_of(idx_of(r) * ROW_TILES, ROW_TILES)
        pltpu.make_async_copy(src_hbm.at[pl.ds(src_row, ROW_TILES)], dst.at[pl.ds(r * ROW_TILES, ROW_TILES)],
                              sem).start(priority=priority_of(r))


INVERT_UNROLL = 16


def _invert_body(trips_ref, dest_hbm, zeros_hbm, out_hbm, dbuf, obuf, sem):
    k = pl.program_id(0)
    n_tok = dbuf.shape[0]

    @pl.when(k == 0)
    def _():
        init = pltpu.make_async_copy(zeros_hbm, obuf, sem.at[1])
        init.start()
        init.wait()

    fetch = pltpu.make_async_copy(dest_hbm.at[pl.ds(pl.multiple_of(k * n_tok, 128), n_tok)], dbuf, sem.at[0])
    fetch.start()
    fetch.wait()

    def body(tb, carry):
        for u in range(INVERT_UNROLL):
            t = tb * INVERT_UNROLL + u
            obuf[dbuf[t]] = t
        return carry

    lax.fori_loop(0, trips_ref[0], body, 0)

    @pl.when(k == pl.num_programs(0) - 1)
    def _():
        done = pltpu.make_async_copy(obuf, out_hbm, sem.at[1])
        done.start()
        done.wait()


def _invert(dest_flat, n_tok, n_slots):
    trips = jnp.full((1,), n_tok // INVERT_UNROLL, I32)
    return pl.pallas_call(
        _invert_body,
        out_shape=jax.ShapeDtypeStruct((n_slots,), I32),
        grid_spec=pltpu.PrefetchScalarGridSpec(
            num_scalar_prefetch=1,
            grid=(TOP_K,),
            in_specs=[pl.BlockSpec(memory_space=pl.ANY), pl.BlockSpec(memory_space=pl.ANY)],
            out_specs=pl.BlockSpec(memory_space=pl.ANY),
            scratch_shapes=[pltpu.SMEM((n_tok,), I32), pltpu.SMEM((n_slots,), I32),
                            pltpu.SemaphoreType.DMA((2,))]),
        compiler_params=_cparams(),
        name="invert",
    )(trips, dest_flat, jnp.zeros((n_slots,), I32))


N_XBUF = 4
LOOKAHEAD = 3
GATHER_PRIORITY = 0
BULK_PRIORITY = 1


def _experts_body(start_ref, nsub_ref, slot_tok_ref,
                  tok_hbm, wg_hbm, wu_hbm, wd_hbm, y_hbm,
                  xbuf, ybuf, wg_f32, wu_f32, wd_f32, wg_bf, wu_bf, wd_bf, cur, gsem, osem, wsem):
    e = pl.program_id(0)
    n_e = pl.num_programs(0)
    nsub = nsub_ref[e]
    sub_rows = SUB * ROW_TILES

    def weight_copies(ex, slot):
        return [pltpu.make_async_copy(src.at[ex], dst.at[slot], wsem.at[slot, n])
                for n, (src, dst) in enumerate(((wg_hbm, wg_f32), (wu_hbm, wu_f32), (wd_hbm, wd_f32)))]

    def next_nonempty(e0):
        return lax.while_loop(lambda x: jnp.logical_and(x < n_e, nsub_ref[jnp.minimum(x, n_e - 1)] == 0),
                              lambda x: x + 1, e0)

    def produce():
        pe = cur[0]

        @pl.when(pe < n_e)
        def _():
            pj = cur[1]
            pg = cur[2]
            base = start_ref[pe] + pj * SUB
            slot = lax.rem(pg, N_XBUF)
            _start_row_gather(tok_hbm, lambda r: slot_tok_ref[base + r], xbuf.at[slot], gsem.at[slot], SUB,
                              priority_of=lambda r: GATHER_PRIORITY)
            last = pj + 1 >= nsub_ref[pe]
            cur[0] = jnp.where(last, next_nonempty(pe + 1), pe)
            cur[1] = jnp.where(last, 0, pj + 1)
            cur[2] = pg + 1

    def out_wait(slot):
        pltpu.make_async_copy(ybuf.at[slot], y_hbm.at[pl.ds(0, sub_rows)], osem.at[slot]).wait()

    @pl.when(e == 0)
    def _():
        cur[0] = next_nonempty(0)
        cur[1] = 0
        cur[2] = 0
        cur[3] = 0
        for cp in weight_copies(0, 0):
            cp.start(priority=BULK_PRIORITY)
        for _ in range(LOOKAHEAD):
            produce()

    wslot = e & 1

    @pl.when(e + 1 < n_e)
    def _():
        for cp in weight_copies(e + 1, 1 - wslot):
            cp.start(priority=BULK_PRIORITY)

    for cp in weight_copies(e, wslot):
        cp.wait()
    wg_bf[...] = wg_f32[wslot].astype(BF16)
    wu_bf[...] = wu_f32[wslot].astype(BF16)
    wd_bf[...] = wd_f32[wslot].astype(BF16)

    def step(j, carry):
        g = cur[3]
        produce()
        slot = lax.rem(g, N_XBUF)
        yslot = g & 1
        pltpu.make_async_copy(tok_hbm.at[pl.ds(0, sub_rows)], xbuf.at[slot], gsem.at[slot]).wait()

        @pl.when(g >= 2)
        def _():
            out_wait(yslot)

        xb = _rows_from_tiles(xbuf[slot]).astype(BF16)
        a = _dot(xb, wg_bf[...])
        b = _dot(xb, wu_bf[...])
        ybuf[yslot] = _rows_to_tiles(_dot((_silu(a) * b).astype(BF16), wd_bf[...]))
        row0 = pl.multiple_of((start_ref[e] + j * SUB) * ROW_TILES, sub_rows)
        pltpu.make_async_copy(ybuf.at[yslot], y_hbm.at[pl.ds(row0, sub_rows)],
                              osem.at[yslot]).start(priority=BULK_PRIORITY)
        cur[3] = g + 1
        return carry

    lax.fori_loop(0, nsub, step, 0)

    @pl.when(e == n_e - 1)
    def _():
        total = cur[3]

        @pl.when(total >= 2)
        def _():
            out_wait(total & 1)

        @pl.when(total >= 1)
        def _():
            out_wait((total - 1) & 1)

        ybuf[0] = jnp.zeros(ybuf.shape[1:], F32)
        used = start_ref[e] // SUB + nsub

        def tail_copy(sb):
            return pltpu.make_async_copy(ybuf.at[0], y_hbm.at[pl.ds(pl.multiple_of(sb * sub_rows, sub_rows), sub_rows)],
                                         osem.at[0])

        n_sub_total = y_hbm.shape[0] // sub_rows
        lax.fori_loop(used, n_sub_total, lambda sb, c: (tail_copy(sb).start(), c)[1], 0)
        lax.fori_loop(used, n_sub_total, lambda sb, c: (tail_copy(sb).wait(), c)[1], 0)


def _experts(start, nsub, slot_tok, tok_tiles, wg, wu, wd, n_slots):
    sub_rows = SUB * ROW_TILES
    return pl.pallas_call(
        _experts_body,
        out_shape=jax.ShapeDtypeStruct((n_slots * ROW_TILES, 128), F32),
        grid_spec=pltpu.PrefetchScalarGridSpec(
            num_scalar_prefetch=3,
            grid=(N_EXPERTS,),
            in_specs=[pl.BlockSpec(memory_space=pl.ANY)] * 4,
            out_specs=pl.BlockSpec(memory_space=pl.ANY),
            scratch_shapes=[pltpu.VMEM((N_XBUF, sub_rows, 128), F32), pltpu.VMEM((2, sub_rows, 128), F32),
                            pltpu.VMEM((2, D_MODEL, FF), F32), pltpu.VMEM((2, D_MODEL, FF), F32),
                            pltpu.VMEM((2, FF, D_MODEL), F32),
                            pltpu.VMEM((D_MODEL, FF), BF16), pltpu.VMEM((D_MODEL, FF), BF16),
                            pltpu.VMEM((FF, D_MODEL), BF16), pltpu.SMEM((4,), I32),
                            pltpu.SemaphoreType.DMA((N_XBUF,)), pltpu.SemaphoreType.DMA((2,)),
                            pltpu.SemaphoreType.DMA((2, 3))]),
        compiler_params=_cparams(),
        name="experts",
    )(start, nsub, slot_tok, tok_tiles, wg, wu, wd)


def _combine_body(n_tok, n_ctx_tiles, dest_ref,
                  y_hbm, topw_ref, tok_ref, xnew_ref, mod_ref, sg_ref, su_ref, sd_ref, gpost_ref,
                  outp_ref, outs_ref, gbuf, wcol, fbuf, gsem):
    i = pl.program_id(0)
    n = pl.num_programs(0)

    def gather(tile, slot):
        def per_choice(k, carry):
            base = k * n_tok + tile * TT
            _start_row_gather(y_hbm, lambda r: dest_ref[base + r], gbuf.at[slot, k], gsem.at[slot], TT)
            return carry
        lax.fori_loop(0, TOP_K, per_choice, 0)

    def gather_wait(slot):
        for k in range(TOP_K):
            pltpu.make_async_copy(y_hbm.at[pl.ds(0, TT * ROW_TILES)], gbuf.at[slot, k], gsem.at[slot]).wait()

    slot = i & 1

    @pl.when(i == 0)
    def _():
        gather(0, 0)

    @pl.when(i + 1 < n)
    def _():
        gather(i + 1, 1 - slot)

    w_t = jnp.concatenate([topw_ref[...], jnp.zeros((128 - TOP_K, TT), F32)], axis=0).T
    for k in range(TOP_K):
        wcol[k] = jnp.broadcast_to(w_t[:, k:k + 1], (TT, 128))
    h = tok_ref[...].astype(BF16)
    fbuf[...] = _dot((_silu(_dot(h, sg_ref[...])) * _dot(h, su_ref[...])).astype(BF16), sd_ref[...])

    gather_wait(slot)
    for g in range(TT // 8):
        rows = slice(g * 8, (g + 1) * 8)
        tiles = []
        for r in range(8):
            t = g * 8 + r
            acc = None
            for k in range(TOP_K):
                w = jnp.broadcast_to(wcol[k, t:t + 1, :], (8, 128))
                term = gbuf[slot, k, t * ROW_TILES:(t + 1) * ROW_TILES, :] * w
                acc = term if acc is None else acc + term
            tiles.append(acc)
        f = fbuf[rows, :] + jnp.concatenate(_transpose8(tiles), axis=1)
        fbuf[rows, :] = xnew_ref[rows, :] + mod_ref[0, 5:6, :] * _rms(f, gpost_ref[...])

    @pl.when(i < n_ctx_tiles)
    def _():
        outp_ref[...] = fbuf[...]

    @pl.when(i >= n_ctx_tiles)
    def _():
        outs_ref[...] = fbuf[...]


def _combine(tiles, dest_flat, ysorted, topw, tok, xnew, mod3, sg_bf, su_bf, sd_bf, g_post):
    n_tok = tok.shape[0]
    n_ctx_tok = tiles.n_ctx * TM
    n_ctx_tiles = n_ctx_tok // TT
    lat_tiles_per_seq = tiles.lat_tiles * TM // TT

    def mod_row(i):
        return jnp.where(i < n_ctx_tiles, 0, 1 + jnp.maximum(i - n_ctx_tiles, 0) // lat_tiles_per_seq)

    def full(shape):
        return pl.BlockSpec(shape, lambda i, *_: (0,) * len(shape))

    rows = pl.BlockSpec((TT, D_MODEL), lambda i, *_: (i, 0))
    return pl.pallas_call(
        functools.partial(_combine_body, n_tok, n_ctx_tiles),
        out_shape=(jax.ShapeDtypeStruct((n_ctx_tok, D_MODEL), F32),
                   jax.ShapeDtypeStruct((n_tok - n_ctx_tok, D_MODEL), F32)),
        grid_spec=pltpu.PrefetchScalarGridSpec(
            num_scalar_prefetch=1,
            grid=(n_tok // TT,),
            in_specs=[pl.BlockSpec(memory_space=pl.ANY),
                      pl.BlockSpec((TOP_K, TT), lambda i, *_: (0, i)),
                      rows, rows,
                      pl.BlockSpec((1, 6, D_MODEL), lambda i, *_: (mod_row(i), 0, 0)),
                      full((D_MODEL, FF)), full((D_MODEL, FF)), full((FF, D_MODEL)), full((1, D_MODEL))],
            out_specs=(pl.BlockSpec((TT, D_MODEL), lambda i, *_: (jnp.minimum(i, n_ctx_tiles - 1), 0)),
                       pl.BlockSpec((TT, D_MODEL), lambda i, *_: (jnp.maximum(i - n_ctx_tiles, 0), 0))),
            scratch_shapes=[pltpu.VMEM((2, TOP_K, TT * ROW_TILES, 128), F32), pltpu.VMEM((TOP_K, TT, 128), F32),
                            pltpu.VMEM((TT, D_MODEL), F32), pltpu.SemaphoreType.DMA((2,))]),
        compiler_params=_cparams(),
        name="combine",
    )(dest_flat, ysorted, topw, tok, xnew, mod3, sg_bf, su_bf, sd_bf, g_post)


def _rope_tables(lat_len, lat_tiles):
    rows = lat_len // GRID_W
    row = jnp.repeat(jnp.arange(rows, dtype=F32), GRID_W)
    col = jnp.tile(jnp.arange(GRID_W, dtype=F32), rows)
    inv = ROPE_BASE ** (-jnp.arange(ROPE_FREQS, dtype=F32) / ROPE_FREQS)
    ang = jnp.concatenate([row[:, None] * inv[None, :]] * 2 + [col[:, None] * inv[None, :]] * 2, axis=1)
    sign = jnp.tile(jnp.concatenate([-jnp.ones((ROPE_FREQS,), F32), jnp.ones((ROPE_FREQS,), F32)]), 2)
    cos = jnp.cos(ang).reshape(lat_tiles, TM, DK)
    sin = (jnp.sin(ang) * sign[None, :]).reshape(lat_tiles, TM, DK)
    cos = jnp.concatenate([jnp.ones((1, TM, DK), F32), cos], axis=0)
    sin = jnp.concatenate([jnp.zeros((1, TM, DK), F32), sin], axis=0)
    return cos, sin


def kernel(x_prompt, x_sample, state_ret_fwd, state_ret_bwd, c, c_ctx, w_mod, b_mod, norm_mix_pre,
           norm_mix_post, norm_ffn_pre, norm_ffn_post, w_in, conv_w, conv_b, ret_decay_fwd,
           ret_decay_bwd, ret_gn_g, w_out, router_w, router_bias, expert_w_gate, expert_w_up,
           expert_w_down, shared_w_gate, shared_w_up, shared_w_down):
    bp, tp, d = x_prompt.shape
    bs, ts, _ = x_sample.shape
    depth = w_mod.shape[0]
    assert d == D_MODEL and tp == TM and ts % TM == 0 and bs + 1 <= 8
    tiles = _Tiles(bp, bs, ts)
    n_tok = tiles.n_tiles * TM
    n_slots = n_tok * TOP_K + N_EXPERTS * SUB
    cos_t, sin_t = _rope_tables(ts, tiles.lat_tiles)

    perm = (jnp.arange(N_EXPERTS) % N_GROUPS) * GROUP_SIZE + jnp.arange(N_EXPERTS) // N_GROUPS

    xp2 = x_prompt.reshape(bp * tp, d)
    xs2 = x_sample.reshape(bs * ts, d)
    new_f, new_b = [], []
    for l in range(depth):
        c_rows = jnp.concatenate([c_ctx[None, :], c, jnp.zeros((8 - 1 - bs, d), F32)], axis=0)
        mod3 = _modulation(c_rows, w_mod[l], b_mod[l][None, :]).reshape(8, 6, d)
        dec = jnp.broadcast_to(jnp.stack([ret_decay_fwd[l], ret_decay_bwd[l]])[:, :, None, None],
                               (2, HEADS, DK, DK)).astype(F32)
        xnew, tok, sf_fin, sb_fin = _token_mixer(
            tiles, xp2, xs2, mod3, norm_mix_pre[l][None, :], w_in[l].astype(BF16), conv_w[l],
            conv_b[l][None, :], dec, cos_t, sin_t, state_ret_fwd[:, l], state_ret_bwd[:, l],
            w_out[l].astype(BF16), norm_mix_post[l][None, :], norm_ffn_pre[l][None, :], ret_gn_g[l][None, :])
        new_f.append(sf_fin)
        new_b.append(sb_fin)

        rwt = router_w[l].T[perm].astype(BF16)
        bias_b = jnp.broadcast_to(router_bias[l][perm][:, None], (N_EXPERTS, 128)).astype(F32)
        topi, topw = _route(tok, rwt, bias_b)
        dest, start, counts = _dispatch_plan(topi)
        dest_flat = dest.reshape(-1)
        slot_tok = _invert(dest_flat, n_tok, n_slots)
        start_i = start[:, 0].astype(I32)
        nsub = ((counts[:, 0] + float(SUB - 1)) / float(SUB)).astype(I32)
        tok_tiles = tok.reshape(n_tok * ROW_TILES, 128)
        ysorted = _experts(start_i, nsub, slot_tok, tok_tiles, expert_w_gate[l], expert_w_up[l],
                           expert_w_down[l], n_slots)
        xp2, xs2 = _combine(tiles, dest_flat, ysorted, topw, tok, xnew, mod3,
                            shared_w_gate[l].astype(BF16), shared_w_up[l].astype(BF16),
                            shared_w_down[l].astype(BF16), norm_ffn_post[l][None, :])

    return (xp2.reshape(bp, tp, d), xs2.reshape(bs, ts, d),
            jnp.stack(new_f, axis=1), jnp.stack(new_b, axis=1))
```

```python
import functools

import jax
import jax.numpy as jnp
from jax import lax
from jax.experimental import pallas as pl
from jax.experimental.pallas import tpu as pltpu

F32 = jnp.float32
BF16 = jnp.bfloat16
I32 = jnp.int32

D_MODEL = 1024
CONV_W = 512
RET_W = 512
HEADS = 4
DK = 128
CHUNK = 128
GRID_W = 64
ROPE_FREQS = 32
ROPE_BASE = 10000.0
IN_COLS = 3 * CONV_W + 4 * RET_W
N_EXPERTS = 256
N_GROUPS = 8
GROUP_SIZE = N_EXPERTS // N_GROUPS
TOPK_GROUPS = 4
TOP_K = 8
FF = 256
ROUTED_SCALE = 2.5
EPS = 1e-6

TM = 256
SUB = 128
TT = 128
RT = 512
RB = 2048
VMEM_LIMIT = 56 * 1024 * 1024


def _cparams(n_axes=1, vmem=VMEM_LIMIT):
    return pltpu.CompilerParams(dimension_semantics=("arbitrary",) * n_axes,
                                vmem_limit_bytes=vmem)


def _silu(x):
    return x * jax.nn.sigmoid(x)


def _log_sigmoid(x):
    return jnp.minimum(x, 0.0) - jnp.log1p(jnp.exp(-jnp.abs(x)))


def _rms(x, g):
    return x * lax.rsqrt(jnp.mean(x * x, axis=-1, keepdims=True) + EPS) * g


def _dot(a, b):
    return jnp.dot(a, b, preferred_element_type=F32)


def _mod_body(c_ref, w_ref, b_ref, o_ref):
    s = _silu(c_ref[...]).astype(BF16)
    o_ref[...] = _dot(s, w_ref[...].astype(BF16)) + b_ref[...]


def _modulation(c_rows, w_mod, b_mod):
    n_col = w_mod.shape[1]
    blk = 1536
    return pl.pallas_call(
        _mod_body,
        out_shape=jax.ShapeDtypeStruct((8, n_col), F32),
        grid=(n_col // blk,),
        in_specs=[pl.BlockSpec((8, D_MODEL), lambda i: (0, 0)),
                  pl.BlockSpec((D_MODEL, blk), lambda i: (0, i)),
                  pl.BlockSpec((1, blk), lambda i: (0, i))],
        out_specs=pl.BlockSpec((8, blk), lambda i: (0, i)),
        compiler_params=_cparams(),
        name="mod",
    )(c_rows, w_mod, b_mod)


class _Tiles:
    def __init__(self, n_ctx_seq, n_lat_seq, lat_len):
        self.n_ctx = n_ctx_seq
        self.lat_tiles = lat_len // TM
        self.n_lat_seq = n_lat_seq
        self.n_tiles = n_ctx_seq + n_lat_seq * self.lat_tiles

    def is_ctx(self, i):
        return i < self.n_ctx

    def lat_pos(self, i):
        j = jnp.maximum(i - self.n_ctx, 0)
        return j // self.lat_tiles, j % self.lat_tiles

    def phys_reversed(self, i):
        b, t = self.lat_pos(i)
        return jnp.where(i < self.n_ctx, i, self.n_ctx + b * self.lat_tiles + (self.lat_tiles - 1 - t))

    def mod_row(self, i):
        b, _ = self.lat_pos(i)
        return jnp.where(i < self.n_ctx, 0, 1 + b)


def _rope(x, cos, sin_signed):
    lane = lax.broadcasted_iota(I32, x.shape, 1)
    partner = jnp.where((lane & 63) < 32, pltpu.roll(x, 96, 1), pltpu.roll(x, 32, 1))
    return x * cos + partner * sin_signed


def _mix_a_body(tiles, xp_ref, xs_ref, mod_ref, gpre_ref, win_ref, cw_ref, cb_ref, dec_ref,
                cos_ref, sin_ref, s0b_ref,
                yconv_ref, q_ref, v_ref, g_ref, kt_ref, sbin_ref, sbfin_ref,
                sb_scr, tab_scr):
    i = pl.program_id(0)
    is_ctx = tiles.is_ctx(i)
    _, t_rev = tiles.lat_pos(i)
    first = jnp.logical_or(is_ctx, t_rev == 0)

    @pl.when(i == 0)
    def _():
        lg = _log_sigmoid(dec_ref[1])
        col = lax.broadcasted_iota(I32, lg.shape, 2).astype(F32)
        tab_scr[0] = jnp.exp(col * lg)
        tab_scr[1] = jnp.exp(float(CHUNK) * lg)

    @pl.when(first)
    def _():
        sb_scr[...] = jnp.where(is_ctx, 0.0, s0b_ref[0])

    x = jnp.where(is_ctx, xp_ref[...], xs_ref[...])
    h = (_rms(x, gpre_ref[...]) * (1.0 + mod_ref[0, 1:2, :]) + mod_ref[0, 0:1, :]).astype(BF16)

    def proj(k):
        return _dot(h, win_ref[:, k * 512:(k + 1) * 512])

    z = proj(1) * proj(2)
    row = lax.broadcasted_iota(I32, z.shape, 0)
    period = jnp.where(is_ctx, TM, GRID_W)
    pos = row & (period - 1)
    left = jnp.where(pos == 0, 0.0, pltpu.roll(z, 1, 0))
    right = jnp.where(pos == period - 1, 0.0, pltpu.roll(z, TM - 1, 0))
    zc = left * cw_ref[0:1, :] + z * cw_ref[1:2, :] + right * cw_ref[2:3, :] + cb_ref[...]
    yconv_ref[...] = (proj(0) * zc).astype(BF16)

    cos = cos_ref[0]
    sin = sin_ref[0]
    q = proj(3)
    k = proj(4)
    q = jnp.concatenate([_rope(q[:, hh * DK:(hh + 1) * DK], cos, sin) for hh in range(HEADS)], axis=1)
    k = jnp.concatenate([_rope(k[:, hh * DK:(hh + 1) * DK], cos, sin) for hh in range(HEADS)], axis=1)
    q_ref[...] = (q * (DK ** -0.5)).astype(BF16)
    kt = k.T
    kt_ref[...] = kt.astype(BF16)
    v = proj(5).astype(BF16)
    v_ref[...] = v
    g_ref[...] = proj(6)

    for c in (1, 0):
        for hh in range(HEADS):
            sbin_ref[c, hh] = sb_scr[hh].astype(BF16)
            kts = (kt[hh * DK:(hh + 1) * DK, c * CHUNK:(c + 1) * CHUNK] * tab_scr[0, hh]).astype(BF16)
            vc = v[c * CHUNK:(c + 1) * CHUNK, hh * DK:(hh + 1) * DK]
            sb_scr[hh] = sb_scr[hh] * tab_scr[1, hh] + _dot(kts, vc)

    @pl.when(is_ctx)
    def _():
        sbfin_ref[0] = sb_scr[...]


def _mix_b_body(tiles, xp_ref, xs_ref, mod_ref, q_ref, kt_ref, v_ref, g_ref, yconv_ref, sbin_ref,
                wout_ref, gpost_ref, gffn_ref, gn_ref, dec_ref, s0f_ref,
                xnew_ref, tok_ref, sffin_ref,
                sf_scr, tab_scr, ycat_scr):
    i = pl.program_id(0)
    is_ctx = tiles.is_ctx(i)
    _, t_pos = tiles.lat_pos(i)
    first = jnp.logical_or(is_ctx, t_pos == 0)

    @pl.when(i == 0)
    def _():
        lgf = _log_sigmoid(dec_ref[0])
        lgb = _log_sigmoid(dec_ref[1])
        row = lax.broadcasted_iota(I32, lgf.shape, 1)
        col = lax.broadcasted_iota(I32, lgf.shape, 2)
        d = (row - col).astype(F32)
        tab_scr[0] = (jnp.where(row >= col, jnp.exp(jnp.where(row >= col, d, 0.0) * lgf), 0.0)
                      + jnp.where(col >= row, jnp.exp(jnp.where(col >= row, -d, 0.0) * lgb), 0.0))
        tab_scr[1] = jnp.exp((row + 1).astype(F32) * lgf)
        tab_scr[2] = jnp.exp((CHUNK - row).astype(F32) * lgb)
        tab_scr[3] = jnp.exp((CHUNK - 1 - col).astype(F32) * lgf)
        tab_scr[4] = jnp.exp(float(CHUNK) * lgf)

    @pl.when(first)
    def _():
        sf_scr[...] = jnp.where(is_ctx, 0.0, s0f_ref[0])

    for c in range(TM // CHUNK):
        rows = slice(c * CHUNK, (c + 1) * CHUNK)
        for hh in range(HEADS):
            cols = slice(hh * DK, (hh + 1) * DK)
            qc = q_ref[rows, cols]
            ktc = kt_ref[cols, rows]
            vc = v_ref[rows, cols]
            att = (_dot(qc, ktc) * tab_scr[0, hh]).astype(BF16)
            o = (_dot(att, vc)
                 + tab_scr[1, hh] * _dot(qc, sf_scr[hh].astype(BF16))
                 + tab_scr[2, hh] * _dot(qc, sbin_ref[c, hh]))
            kts = (ktc.astype(F32) * tab_scr[3, hh]).astype(BF16)
            sf_scr[hh] = sf_scr[hh] * tab_scr[4, hh] + _dot(kts, vc)
            mu = jnp.mean(o, axis=-1, keepdims=True)
            dev = o - mu
            var = jnp.mean(dev * dev, axis=-1, keepdims=True)
            on = dev * lax.rsqrt(var + EPS) * gn_ref[:, cols]
            ycat_scr[rows, RET_W + hh * DK:RET_W + (hh + 1) * DK] = (_silu(g_ref[rows, cols]) * on).astype(BF16)
    ycat_scr[:, 0:CONV_W] = yconv_ref[...]

    @pl.when(is_ctx)
    def _():
        sffin_ref[0] = sf_scr[...]

    x = jnp.where(is_ctx, xp_ref[...], xs_ref[...])
    u = _dot(ycat_scr[...], wout_ref[...])
    xn = x + mod_ref[0, 2:3, :] * _rms(u, gpost_ref[...])
    xnew_ref[...] = xn
    tok_ref[...] = _rms(xn, gffn_ref[...]) * (1.0 + mod_ref[0, 4:5, :]) + mod_ref[0, 3:4, :]


def _token_mixer(tiles, xp2, xs2, mod3, g_pre, win_bf, conv_w, conv_b, dec, cos_t, sin_t,
                 s0f, s0b, wout_bf, g_post, g_ffn, gn_g):
    n_tok = tiles.n_tiles * TM
    n_ctx = tiles.n_ctx
    last_ctx = n_ctx - 1

    def full(shape):
        return pl.BlockSpec(shape, lambda i: (0,) * len(shape))

    def xp_spec(phys):
        return pl.BlockSpec((TM, D_MODEL), lambda i: (jnp.minimum(phys(i), last_ctx), 0))

    def xs_spec(phys):
        return pl.BlockSpec((TM, D_MODEL), lambda i: (jnp.maximum(phys(i) - n_ctx, 0), 0))

    mod_spec = pl.BlockSpec((1, 6, D_MODEL), lambda i: (tiles.mod_row(i), 0, 0))
    state_in = pl.BlockSpec((1, HEADS, DK, DK), lambda i: (tiles.lat_pos(i)[0], 0, 0, 0))
    state_out = pl.BlockSpec((1, HEADS, DK, DK), lambda i: (jnp.minimum(i, last_ctx), 0, 0, 0))

    rev = tiles.phys_reversed

    def rope_idx(i):
        _, t = tiles.lat_pos(i)
        return jnp.where(i < n_ctx, 0, 1 + (tiles.lat_tiles - 1 - t))

    rope_spec = pl.BlockSpec((1, TM, DK), lambda i: (rope_idx(i), 0, 0))

    def rows(width, phys):
        return pl.BlockSpec((TM, width), lambda i: (phys(i), 0))

    yconv, q, v, g, kt, sbin, sb_fin = pl.pallas_call(
        functools.partial(_mix_a_body, tiles),
        out_shape=(jax.ShapeDtypeStruct((n_tok, CONV_W), BF16),
                   jax.ShapeDtypeStruct((n_tok, RET_W), BF16),
                   jax.ShapeDtypeStruct((n_tok, RET_W), BF16),
                   jax.ShapeDtypeStruct((n_tok, RET_W), F32),
                   jax.ShapeDtypeStruct((RET_W, n_tok), BF16),
                   jax.ShapeDtypeStruct((n_tok // CHUNK, HEADS, DK, DK), BF16),
                   jax.ShapeDtypeStruct((n_ctx, HEADS, DK, DK), F32)),
        grid=(tiles.n_tiles,),
        in_specs=[xp_spec(rev), xs_spec(rev), mod_spec, full((1, D_MODEL)), full((D_MODEL, IN_COLS)),
                  full((3, CONV_W)), full((1, CONV_W)), full((2, HEADS, DK, DK)),
                  rope_spec, rope_spec, state_in],
        out_specs=(rows(CONV_W, rev), rows(RET_W, rev), rows(RET_W, rev), rows(RET_W, rev),
                   pl.BlockSpec((RET_W, TM), lambda i: (0, rev(i))),
                   pl.BlockSpec((TM // CHUNK, HEADS, DK, DK), lambda i: (rev(i), 0, 0, 0)),
                   state_out),
        scratch_shapes=[pltpu.VMEM((HEADS, DK, DK), F32), pltpu.VMEM((2, HEADS, DK, DK), F32)],
        compiler_params=_cparams(),
        name="mix_a",
    )(xp2, xs2, mod3, g_pre, win_bf, conv_w, conv_b, dec, cos_t, sin_t, s0b)

    ident = lambda i: i
    xnew, tok, sf_fin = pl.pallas_call(
        functools.partial(_mix_b_body, tiles),
        out_shape=(jax.ShapeDtypeStruct((n_tok, D_MODEL), F32),
                   jax.ShapeDtypeStruct((n_tok, D_MODEL), F32),
                   jax.ShapeDtypeStruct((n_ctx, HEADS, DK, DK), F32)),
        grid=(tiles.n_tiles,),
        in_specs=[xp_spec(ident), xs_spec(ident), mod_spec,
                  rows(RET_W, ident),
                  pl.BlockSpec((RET_W, TM), lambda i: (0, i)),
                  rows(RET_W, ident), rows(RET_W, ident), rows(CONV_W, ident),
                  pl.BlockSpec((TM // CHUNK, HEADS, DK, DK), lambda i: (i, 0, 0, 0)),
                  full((D_MODEL, D_MODEL)), full((1, D_MODEL)), full((1, D_MODEL)), full((1, RET_W)),
                  full((2, HEADS, DK, DK)), state_in],
        out_specs=(rows(D_MODEL, ident), rows(D_MODEL, ident), state_out),
        scratch_shapes=[pltpu.VMEM((HEADS, DK, DK), F32), pltpu.VMEM((5, HEADS, DK, DK), F32),
                        pltpu.VMEM((TM, D_MODEL), BF16)],
        compiler_params=_cparams(),
        name="mix_b",
    )(xp2, xs2, mod3, q, kt, v, g, yconv, sbin, wout_bf, g_post, g_ffn, gn_g, dec, s0f)
    return xnew, tok, sf_fin, sb_fin


def _route_body(tok_ref, rwt_ref, bias_ref, topi_ref, topw_ref):
    h = tok_ref[...].astype(BF16)
    logits = lax.dot_general(rwt_ref[...], h, (((1,), (1,)), ((), ())), preferred_element_type=F32)
    shape3 = (GROUP_SIZE, N_GROUPS, 128)
    member = lax.broadcasted_iota(I32, shape3, 0)
    group = lax.broadcasted_iota(I32, shape3, 1)
    expert = group * GROUP_SIZE + member
    group2 = lax.broadcasted_iota(I32, (N_GROUPS, 128), 0)
    neg = -jnp.inf
    for lb in range(RT // 128):
        scores = jax.nn.sigmoid(logits[:, lb * 128:(lb + 1) * 128]).reshape(shape3)
        biased = scores + bias_ref[...].reshape(shape3)
        m1 = jnp.max(biased, axis=0)
        first = jnp.min(jnp.where(biased == m1, member, GROUP_SIZE), axis=0)
        m2 = jnp.max(jnp.where(member == first, neg, biased), axis=0)
        gs = m1 + m2
        beaten = jnp.zeros(gs.shape, I32)
        for s in range(1, N_GROUPS):
            other = pltpu.roll(gs, s, 0)
            wins = (other > gs) | ((other == gs) & (group2 >= s))
            beaten = beaten + wins.astype(I32)
        keep = beaten < TOPK_GROUPS
        cand = jnp.where(keep, biased, neg)
        idx_rows, w_rows = [], []
        for _ in range(TOP_K):
            best = jnp.max(jnp.max(cand, axis=0), axis=0, keepdims=True)
            pick = jnp.min(jnp.min(jnp.where(cand == best, expert, N_EXPERTS), axis=0), axis=0, keepdims=True)
            hit = expert == pick
            w_rows.append(jnp.sum(jnp.sum(jnp.where(hit, scores, 0.0), axis=0), axis=0, keepdims=True))
            idx_rows.append(pick)
            cand = jnp.where(hit, neg, cand)
        w = jnp.concatenate(w_rows, axis=0)
        topi_ref[:, lb * 128:(lb + 1) * 128] = jnp.concatenate(idx_rows, axis=0)
        topw_ref[:, lb * 128:(lb + 1) * 128] = w / jnp.sum(w, axis=0, keepdims=True) * ROUTED_SCALE


def _route(tok, rwt_bf, bias_b):
    n_tok = tok.shape[0]
    return pl.pallas_call(
        _route_body,
        out_shape=(jax.ShapeDtypeStruct((TOP_K, n_tok), I32), jax.ShapeDtypeStruct((TOP_K, n_tok), F32)),
        grid=(n_tok // RT,),
        in_specs=[pl.BlockSpec((RT, D_MODEL), lambda i: (i, 0)),
                  pl.BlockSpec((N_EXPERTS, D_MODEL), lambda i: (0, 0)),
                  pl.BlockSpec((N_EXPERTS, 128), lambda i: (0, 0))],
        out_specs=(pl.BlockSpec((TOP_K, RT), lambda i: (0, i)), pl.BlockSpec((TOP_K, RT), lambda i: (0, i))),
        compiler_params=_cparams(),
        name="route",
    )(tok, rwt_bf, bias_b)


def _onehot(ids_row):
    e = lax.broadcasted_iota(I32, (N_EXPERTS, 256), 0)
    return e == ids_row


def _rank_body(topi_ref, rank_ref, counts_ref, run_scr):
    i = pl.program_id(0)

    @pl.when(i == 0)
    def _():
        run_scr[...] = jnp.zeros(run_scr.shape, F32)

    a0 = lax.broadcasted_iota(I32, (256, 256), 0)
    a1 = lax.broadcasted_iota(I32, (256, 256), 1)
    upper = (a0 <= a1).astype(BF16)
    ones = jnp.ones((256, 256), BF16)
    for k in range(TOP_K):
        for sb in range(RB // 256):
            lanes = slice(sb * 256, (sb + 1) * 256)
            oh = _onehot(topi_ref[k:k + 1, lanes])
            ohb = oh.astype(BF16)
            seen = _dot(ohb, upper) + run_scr[...]
            r = jnp.sum(jnp.where(oh, seen, 0.0), axis=0, keepdims=True) - 1.0
            rank_ref[k:k + 1, lanes] = r.astype(I32)
            run_scr[...] = run_scr[...] + _dot(ohb, ones)

    @pl.when(i == pl.num_programs(0) - 1)
    def _():
        counts_ref[...] = run_scr[:, 0:128]


def _dest_body(topi_ref, rank_ref, counts_ref, dest_ref, start_ref, start_scr):
    i = pl.program_id(0)

    @pl.when(i == 0)
    def _():
        nb = jnp.floor((counts_ref[...] + float(SUB - 1)) / float(SUB))
        hi = jnp.floor(nb / 16.0)
        lo = nb - hi * 16.0
        e0 = lax.broadcasted_iota(I32, (N_EXPERTS, N_EXPERTS), 0)
        e1 = lax.broadcasted_iota(I32, (N_EXPERTS, N_EXPERTS), 1)
        below = (e1 < e0).astype(BF16)
        first_blk = 16.0 * _dot(below, hi.astype(BF16)) + _dot(below, lo.astype(BF16))
        start_scr[...] = first_blk * float(SUB)
        start_ref[...] = start_scr[...]

    start = jnp.concatenate([start_scr[...], start_scr[...]], axis=1)
    for k in range(TOP_K):
        for sb in range(RB // 256):
            lanes = slice(sb * 256, (sb + 1) * 256)
            oh = _onehot(topi_ref[k:k + 1, lanes])
            base = jnp.sum(jnp.where(oh, start, 0.0), axis=0, keepdims=True)
            dest_ref[k:k + 1, lanes] = base.astype(I32) + rank_ref[k:k + 1, lanes]


def _dispatch_plan(topi):
    n_tok = topi.shape[1]
    blk = pl.BlockSpec((TOP_K, RB), lambda i: (0, i))
    whole = pl.BlockSpec((N_EXPERTS, 128), lambda i: (0, 0))
    rank, counts = pl.pallas_call(
        _rank_body,
        out_shape=(jax.ShapeDtypeStruct((TOP_K, n_tok), I32), jax.ShapeDtypeStruct((N_EXPERTS, 128), F32)),
        grid=(n_tok // RB,),
        in_specs=[blk],
        out_specs=(blk, whole),
        scratch_shapes=[pltpu.VMEM((N_EXPERTS, 256), F32)],
        compiler_params=_cparams(),
        name="rank",
    )(topi)
    dest, start = pl.pallas_call(
        _dest_body,
        out_shape=(jax.ShapeDtypeStruct((TOP_K, n_tok), I32), jax.ShapeDtypeStruct((N_EXPERTS, 128), F32)),
        grid=(n_tok // RB,),
        in_specs=[blk, blk, whole],
        out_specs=(blk, whole),
        scratch_shapes=[pltpu.VMEM((N_EXPERTS, 128), F32)],
        compiler_params=_cparams(),
        name="dest",
    )(topi, rank, counts)
    return dest, start, counts


ROW_TILES = D_MODEL // 128


def _transpose8(vs):
    sub = lax.broadcasted_iota(I32, (8, 128), 0)
    for d in (4, 2, 1):
        keep = (sub & d) == 0
        out = list(vs)
        for i in range(8):
            if i & d == 0:
                a, b = vs[i], vs[i + d]
                out[i] = jnp.where(keep, a, pltpu.roll(b, d, 0))
                out[i + d] = jnp.where(keep, pltpu.roll(a, 8 - d, 0), b)
        vs = out
    return vs


def _rows_from_tiles(tiles):
    n_rows = tiles.shape[0] // ROW_TILES
    groups = [_transpose8([tiles[(g * 8 + r) * ROW_TILES:(g * 8 + r + 1) * ROW_TILES] for r in range(8)])
              for g in range(n_rows // 8)]
    return jnp.concatenate([jnp.concatenate([grp[c] for grp in groups], axis=0) for c in range(ROW_TILES)], axis=1)


def _rows_to_tiles(value):
    n_rows = value.shape[0]
    pieces = []
    for g in range(n_rows // 8):
        pieces += _transpose8([value[g * 8:(g + 1) * 8, c * 128:(c + 1) * 128] for c in range(ROW_TILES)])
    return jnp.concatenate(pieces, axis=0)


def _start_row_gather(src_hbm, idx_of, dst, sem, n_rows, priority_of=lambda r: r % 2):
    for r in range(n_rows):
        src_row = pl.multiple_of(idx_of(r) * ROW_TILES, ROW_TILES)
        pltpu.make_async_copy(src_hbm.at[pl.ds(src_row, ROW_TILES)], dst.at[pl.ds(r * ROW_TILES, ROW_TILES)],
                              sem).start(priority=priority_of(r))


INVERT_UNROLL = 16


def _invert_body(trips_ref, dest_hbm, zeros_hbm, out_hbm, dbuf, obuf, sem):
    k = pl.program_id(0)
    n_tok = dbuf.shape[0]

    @pl.when(k == 0)
    def _():
        init = pltpu.make_async_copy(zeros_hbm, obuf, sem.at[1])
        init.start()
        init.wait()

    fetch = pltpu.make_async_copy(dest_hbm.at[pl.ds(pl.multiple_of(k * n_tok, 128), n_tok)], dbuf, sem.at[0])
    fetch.start()
    fetch.wait()

    def body(tb, carry):
        for u in range(INVERT_UNROLL):
            t = tb * INVERT_UNROLL + u
            obuf[dbuf[t]] = t
        return carry

    lax.fori_loop(0, trips_ref[0], body, 0)

    @pl.when(k == pl.num_programs(0) - 1)
    def _():
        done = pltpu.make_async_copy(obuf, out_hbm, sem.at[1])
        done.start()
        done.wait()


def _invert(dest_flat, n_tok, n_slots):
    trips = jnp.full((1,), n_tok // INVERT_UNROLL, I32)
    return pl.pallas_call(
        _invert_body,
        out_shape=jax.ShapeDtypeStruct((n_slots,), I32),
        grid_spec=pltpu.PrefetchScalarGridSpec(
            num_scalar_prefetch=1,
            grid=(TOP_K,),
            in_specs=[pl.BlockSpec(memory_space=pl.ANY), pl.BlockSpec(memory_space=pl.ANY)],
            out_specs=pl.BlockSpec(memory_space=pl.ANY),
            scratch_shapes=[pltpu.SMEM((n_tok,), I32), pltpu.SMEM((n_slots,), I32),
                            pltpu.SemaphoreType.DMA((2,))]),
        compiler_params=_cparams(),
        name="invert",
    )(trips, dest_flat, jnp.zeros((n_slots,), I32))


N_XBUF = 4
LOOKAHEAD = 3
GATHER_PRIORITY = 0
BULK_PRIORITY = 1


def _experts_body(start_ref, nsub_ref, slot_tok_ref,
                  tok_hbm, wg_hbm, wu_hbm, wd_hbm, y_hbm,
                  xbuf, ybuf, wg_f32, wu_f32, wd_f32, wg_bf, wu_bf, wd_bf, cur, nxt, gsem, osem, wsem):
    e = pl.program_id(0)
    n_e = pl.num_programs(0)
    nsub = nsub_ref[e]
    sub_rows = SUB * ROW_TILES
    n_sub_total = y_hbm.shape[0] // sub_rows

    def weight_copies(ex, slot):
        return [pltpu.make_async_copy(src.at[ex], dst.at[slot], wsem.at[slot, n])
                for n, (src, dst) in enumerate(((wg_hbm, wg_f32), (wu_hbm, wu_f32), (wd_hbm, wd_f32)))]

    def produce():
        pe = cur[0]
        pj = cur[1]
        pg = cur[2]
        live = pe < n_e
        pe_c = jnp.minimum(pe, n_e - 1)
        base = jnp.where(live, start_ref[pe_c] + pj * SUB, 0)
        slot = lax.rem(pg, N_XBUF)
        _start_row_gather(tok_hbm, lambda r: slot_tok_ref[base + r], xbuf.at[slot], gsem.at[slot], SUB,
                          priority_of=lambda r: GATHER_PRIORITY)
        last = pj + 1 >= nsub_ref[pe_c]
        cur[0] = jnp.where(jnp.logical_and(live, last), nxt[pe_c], pe)
        cur[1] = jnp.where(last, 0, pj + 1)
        cur[2] = pg + 1

    def gather_wait(slot):
        pltpu.make_async_copy(tok_hbm.at[pl.ds(0, sub_rows)], xbuf.at[slot], gsem.at[slot]).wait()

    def out_copy(sb, slot):
        return pltpu.make_async_copy(ybuf.at[slot], y_hbm.at[pl.ds(pl.multiple_of(sb * sub_rows, sub_rows), sub_rows)],
                                     osem.at[slot])

    @pl.when(e == 0)
    def _():
        def fill(i, following):
            x = N_EXPERTS - 1 - i
            nxt[x] = following
            return jnp.where(nsub_ref[x] > 0, x, following)

        cur[0] = lax.fori_loop(0, N_EXPERTS, fill, N_EXPERTS)
        cur[1] = 0
        cur[2] = 0
        cur[3] = 0
        for cp in weight_copies(0, 0):
            cp.start(priority=BULK_PRIORITY)
        ybuf[...] = jnp.zeros(ybuf.shape, F32)
        for s in range(2):
            out_copy(n_sub_total - 1 - s, s).start(priority=BULK_PRIORITY)
        for _ in range(LOOKAHEAD):
            produce()

    wslot = e & 1

    @pl.when(e + 1 < n_e)
    def _():
        for cp in weight_copies(e + 1, 1 - wslot):
            cp.start(priority=BULK_PRIORITY)

    for cp in weight_copies(e, wslot):
        cp.wait()
    wg_bf[...] = wg_f32[wslot].astype(BF16)
    wu_bf[...] = wu_f32[wslot].astype(BF16)
    wd_bf[...] = wd_f32[wslot].astype(BF16)

    def step(j, carry):
        g = cur[3]
        produce()
        slot = lax.rem(g, N_XBUF)
        yslot = g & 1
        gather_wait(slot)
        out_copy(0, yslot).wait()
        xb = _rows_from_tiles(xbuf[slot]).astype(BF16)
        a = _dot(xb, wg_bf[...])
        b = _dot(xb, wu_bf[...])
        ybuf[yslot] = _rows_to_tiles(_dot((_silu(a) * b).astype(BF16), wd_bf[...]))
        out_copy(start_ref[e] // SUB + j, yslot).start(priority=BULK_PRIORITY)
        cur[3] = g + 1
        return carry

    lax.fori_loop(0, nsub, step, 0)

    @pl.when(e == n_e - 1)
    def _():
        total = cur[3]
        for s in range(2):
            out_copy(0, s).wait()
        for ahead in range(LOOKAHEAD):
            gather_wait(lax.rem(total + ahead, N_XBUF))
        ybuf[0] = jnp.zeros(ybuf.shape[1:], F32)
        used = start_ref[e] // SUB + nsub
        lax.fori_loop(used, n_sub_total - 2, lambda sb, c: (out_copy(sb, 0).start(), c)[1], 0)
        lax.fori_loop(used, n_sub_total - 2, lambda sb, c: (out_copy(sb, 0).wait(), c)[1], 0)


def _experts(start, nsub, slot_tok, tok_tiles, wg, wu, wd, n_slots):
    sub_rows = SUB * ROW_TILES
    return pl.pallas_call(
        _experts_body,
        out_shape=jax.ShapeDtypeStruct((n_slots * ROW_TILES, 128), F32),
        grid_spec=pltpu.PrefetchScalarGridSpec(
            num_scalar_prefetch=3,
            grid=(N_EXPERTS,),
            in_specs=[pl.BlockSpec(memory_space=pl.ANY)] * 4,
            out_specs=pl.BlockSpec(memory_space=pl.ANY),
            scratch_shapes=[pltpu.VMEM((N_XBUF, sub_rows, 128), F32), pltpu.VMEM((2, sub_rows, 128), F32),
                            pltpu.VMEM((2, D_MODEL, FF), F32), pltpu.VMEM((2, D_MODEL, FF), F32),
                            pltpu.VMEM((2, FF, D_MODEL), F32),
                            pltpu.VMEM((D_MODEL, FF), BF16), pltpu.VMEM((D_MODEL, FF), BF16),
                            pltpu.VMEM((FF, D_MODEL), BF16), pltpu.SMEM((4,), I32), pltpu.SMEM((N_EXPERTS,), I32),
                            pltpu.SemaphoreType.DMA((N_XBUF,)), pltpu.SemaphoreType.DMA((2,)),
                            pltpu.SemaphoreType.DMA((2, 3))]),
        compiler_params=_cparams(),
        name="experts",
    )(start, nsub, slot_tok, tok_tiles, wg, wu, wd)


def _combine_body(n_tok, n_ctx_tiles, dest_ref,
                  y_hbm, topw_ref, tok_ref, xnew_ref, mod_ref, sg_ref, su_ref, sd_ref, gpost_ref,
                  outp_ref, outs_ref, gbuf, wcol, fbuf, gsem):
    i = pl.program_id(0)
    n = pl.num_programs(0)

    def gather(tile, slot):
        def per_choice(k, carry):
            base = k * n_tok + tile * TT
            _start_row_gather(y_hbm, lambda r: dest_ref[base + r], gbuf.at[slot, k], gsem.at[slot], TT)
            return carry
        lax.fori_loop(0, TOP_K, per_choice, 0)

    def gather_wait(slot):
        for k in range(TOP_K):
            pltpu.make_async_copy(y_hbm.at[pl.ds(0, TT * ROW_TILES)], gbuf.at[slot, k], gsem.at[slot]).wait()

    slot = i & 1

    @pl.when(i == 0)
    def _():
        gather(0, 0)

    nxt_tile = jnp.minimum(i + 1, n - 1)
    for k in range(TOP_K):
        base = k * n_tok + nxt_tile * TT
        _start_row_gather(y_hbm, lambda r: dest_ref[base + r], gbuf.at[1 - slot, k], gsem.at[1 - slot], TT)

    w_t = jnp.concatenate([topw_ref[...], jnp.zeros((128 - TOP_K, TT), F32)], axis=0).T
    for k in range(TOP_K):
        wcol[k] = jnp.broadcast_to(w_t[:, k:k + 1], (TT, 128))
    h = tok_ref[...].astype(BF16)
    fbuf[...] = _dot((_silu(_dot(h, sg_ref[...])) * _dot(h, su_ref[...])).astype(BF16), sd_ref[...])

    gather_wait(slot)
    for g in range(TT // 8):
        rows = slice(g * 8, (g + 1) * 8)
        tiles = []
        for r in range(8):
            t = g * 8 + r
            acc = None
            for k in range(TOP_K):
                w = jnp.broadcast_to(wcol[k, t:t + 1, :], (8, 128))
                term = gbuf[slot, k, t * ROW_TILES:(t + 1) * ROW_TILES, :] * w
                acc = term if acc is None else acc + term
            tiles.append(acc)
        f = fbuf[rows, :] + jnp.concatenate(_transpose8(tiles), axis=1)
        fbuf[rows, :] = xnew_ref[rows, :] + mod_ref[0, 5:6, :] * _rms(f, gpost_ref[...])

    @pl.when(i == n - 1)
    def _():
        gather_wait(1 - slot)

    @pl.when(i < n_ctx_tiles)
    def _():
        outp_ref[...] = fbuf[...]

    @pl.when(i >= n_ctx_tiles)
    def _():
        outs_ref[...] = fbuf[...]


def _combine(tiles, dest_flat, ysorted, topw, tok, xnew, mod3, sg_bf, su_bf, sd_bf, g_post):
    n_tok = tok.shape[0]
    n_ctx_tok = tiles.n_ctx * TM
    n_ctx_tiles = n_ctx_tok // TT
    lat_tiles_per_seq = tiles.lat_tiles * TM // TT

    def mod_row(i):
        return jnp.where(i < n_ctx_tiles, 0, 1 + jnp.maximum(i - n_ctx_tiles, 0) // lat_tiles_per_seq)

    def full(shape):
        return pl.BlockSpec(shape, lambda i, *_: (0,) * len(shape))

    rows = pl.BlockSpec((TT, D_MODEL), lambda i, *_: (i, 0))
    return pl.pallas_call(
        functools.partial(_combine_body, n_tok, n_ctx_tiles),
        out_shape=(jax.ShapeDtypeStruct((n_ctx_tok, D_MODEL), F32),
                   jax.ShapeDtypeStruct((n_tok - n_ctx_tok, D_MODEL), F32)),
        grid_spec=pltpu.PrefetchScalarGridSpec(
            num_scalar_prefetch=1,
            grid=(n_tok // TT,),
            in_specs=[pl.BlockSpec(memory_space=pl.ANY),
                      pl.BlockSpec((TOP_K, TT), lambda i, *_: (0, i)),
                      rows, rows,
                      pl.BlockSpec((1, 6, D_MODEL), lambda i, *_: (mod_row(i), 0, 0)),
                      full((D_MODEL, FF)), full((D_MODEL, FF)), full((FF, D_MODEL)), full((1, D_MODEL))],
            out_specs=(pl.BlockSpec((TT, D_MODEL), lambda i, *_: (jnp.minimum(i, n_ctx_tiles - 1), 0)),
                       pl.BlockSpec((TT, D_MODEL), lambda i, *_: (jnp.maximum(i - n_ctx_tiles, 0), 0))),
            scratch_shapes=[pltpu.VMEM((2, TOP_K, TT * ROW_TILES, 128), F32), pltpu.VMEM((TOP_K, TT, 128), F32),
                            pltpu.VMEM((TT, D_MODEL), F32), pltpu.SemaphoreType.DMA((2,))]),
        compiler_params=_cparams(),
        name="combine",
    )(dest_flat, ysorted, topw, tok, xnew, mod3, sg_bf, su_bf, sd_bf, g_post)


def _rope_tables(lat_len, lat_tiles):
    rows = lat_len // GRID_W
    row = jnp.repeat(jnp.arange(rows, dtype=F32), GRID_W)
    col = jnp.tile(jnp.arange(GRID_W, dtype=F32), rows)
    inv = ROPE_BASE ** (-jnp.arange(ROPE_FREQS, dtype=F32) / ROPE_FREQS)
    ang = jnp.concatenate([row[:, None] * inv[None, :]] * 2 + [col[:, None] * inv[None, :]] * 2, axis=1)
    sign = jnp.tile(jnp.concatenate([-jnp.ones((ROPE_FREQS,), F32), jnp.ones((ROPE_FREQS,), F32)]), 2)
    cos = jnp.cos(ang).reshape(lat_tiles, TM, DK)
    sin = (jnp.sin(ang) * sign[None, :]).reshape(lat_tiles, TM, DK)
    cos = jnp.concatenate([jnp.ones((1, TM, DK), F32), cos], axis=0)
    sin = jnp.concatenate([jnp.zeros((1, TM, DK), F32), sin], axis=0)
    return cos, sin


def kernel(x_prompt, x_sample, state_ret_fwd, state_ret_bwd, c, c_ctx, w_mod, b_mod, norm_mix_pre,
           norm_mix_post, norm_ffn_pre, norm_ffn_post, w_in, conv_w, conv_b, ret_decay_fwd,
           ret_decay_bwd, ret_gn_g, w_out, router_w, router_bias, expert_w_gate, expert_w_up,
           expert_w_down, shared_w_gate, shared_w_up, shared_w_down):
    bp, tp, d = x_prompt.shape
    bs, ts, _ = x_sample.shape
    depth = w_mod.shape[0]
    assert d == D_MODEL and tp == TM and ts % TM == 0 and bs + 1 <= 8
    tiles = _Tiles(bp, bs, ts)
    n_tok = tiles.n_tiles * TM
    n_slots = n_tok * TOP_K + N_EXPERTS * SUB
    cos_t, sin_t = _rope_tables(ts, tiles.lat_tiles)

    perm = (jnp.arange(N_EXPERTS) % N_GROUPS) * GROUP_SIZE + jnp.arange(N_EXPERTS) // N_GROUPS

    xp2 = x_prompt.reshape(bp * tp, d)
    xs2 = x_sample.reshape(bs * ts, d)
    new_f, new_b = [], []
    for l in range(depth):
        c_rows = jnp.concatenate([c_ctx[None, :], c, jnp.zeros((8 - 1 - bs, d), F32)], axis=0)
        mod3 = _modulation(c_rows, w_mod[l], b_mod[l][None, :]).reshape(8, 6, d)
        dec = jnp.broadcast_to(jnp.stack([ret_decay_fwd[l], ret_decay_bwd[l]])[:, :, None, None],
                               (2, HEADS, DK, DK)).astype(F32)
        xnew, tok, sf_fin, sb_fin = _token_mixer(
            tiles, xp2, xs2, mod3, norm_mix_pre[l][None, :], w_in[l].astype(BF16), conv_w[l],
            conv_b[l][None, :], dec, cos_t, sin_t, state_ret_fwd[:, l], state_ret_bwd[:, l],
            w_out[l].astype(BF16), norm_mix_post[l][None, :], norm_ffn_pre[l][None, :], ret_gn_g[l][None, :])
        new_f.append(sf_fin)
        new_b.append(sb_fin)

        rwt = router_w[l].T[perm].astype(BF16)
        bias_b = jnp.broadcast_to(router_bias[l][perm][:, None], (N_EXPERTS, 128)).astype(F32)
        topi, topw = _route(tok, rwt, bias_b)
        dest, start, counts = _dispatch_plan(topi)
        dest_flat = dest.reshape(-1)
        slot_tok = _invert(dest_flat, n_tok, n_slots)
        start_i = start[:, 0].astype(I32)
        nsub = ((counts[:, 0] + float(SUB - 1)) / float(SUB)).astype(I32)
        tok_tiles = tok.reshape(n_tok * ROW_TILES, 128)
        ysorted = _experts(start_i, nsub, slot_tok, tok_tiles, expert_w_gate[l], expert_w_up[l],
                           expert_w_down[l], n_slots)
        xp2, xs2 = _combine(tiles, dest_flat, ysorted, topw, tok, xnew, mod3,
                            shared_w_gate[l].astype(BF16), shared_w_up[l].astype(BF16),
                            shared_w_down[l].astype(BF16), norm_ffn_post[l][None, :])

    return (xp2.reshape(bp, tp, d), xs2.reshape(bs, ts, d),
            jnp.stack(new_f, axis=1), jnp.stack(new_b, axis=1))
```

```python
import functools

import jax
import jax.numpy as jnp
from jax import lax
from jax.experimental import pallas as pl
from jax.experimental.pallas import tpu as pltpu

F32 = jnp.float32
BF16 = jnp.bfloat16
I32 = jnp.int32

D_MODEL = 1024
CONV_W = 512
RET_W = 512
HEADS = 4
DK = 128
CHUNK = 128
GRID_W = 64
ROPE_FREQS = 32
ROPE_BASE = 10000.0
IN_COLS = 3 * CONV_W + 4 * RET_W
N_EXPERTS = 256
N_GROUPS = 8
GROUP_SIZE = N_EXPERTS // N_GROUPS
TOPK_GROUPS = 4
TOP_K = 8
FF = 256
ROUTED_SCALE = 2.5
EPS = 1e-6

TM = 256
SUB = 128
TT = 128
RT = 512
RB = 2048
VMEM_LIMIT = 56 * 1024 * 1024


def _cparams(n_axes=1, vmem=VMEM_LIMIT):
    return pltpu.CompilerParams(dimension_semantics=("arbitrary",) * n_axes,
                                vmem_limit_bytes=vmem)


def _silu(x):
    return x * jax.nn.sigmoid(x)


def _log_sigmoid(x):
    return jnp.minimum(x, 0.0) - jnp.log1p(jnp.exp(-jnp.abs(x)))


def _rms(x, g):
    return x * lax.rsqrt(jnp.mean(x * x, axis=-1, keepdims=True) + EPS) * g


def _dot(a, b):
    return jnp.dot(a, b, preferred_element_type=F32)


def _mod_body(c_ref, w_ref, b_ref, o_ref):
    s = _silu(c_ref[...]).astype(BF16)
    o_ref[...] = _dot(s, w_ref[...].astype(BF16)) + b_ref[...]


def _modulation(c_rows, w_mod, b_mod):
    n_col = w_mod.shape[1]
    blk = 1536
    return pl.pallas_call(
        _mod_body,
        out_shape=jax.ShapeDtypeStruct((8, n_col), F32),
        grid=(n_col // blk,),
        in_specs=[pl.BlockSpec((8, D_MODEL), lambda i: (0, 0)),
                  pl.BlockSpec((D_MODEL, blk), lambda i: (0, i)),
                  pl.BlockSpec((1, blk), lambda i: (0, i))],
        out_specs=pl.BlockSpec((8, blk), lambda i: (0, i)),
        compiler_params=_cparams(),
        name="mod",
    )(c_rows, w_mod, b_mod)


class _Tiles:
    def __init__(self, n_ctx_seq, n_lat_seq, lat_len):
        self.n_ctx = n_ctx_seq
        self.lat_tiles = lat_len // TM
        self.n_lat_seq = n_lat_seq
        self.n_tiles = n_ctx_seq + n_lat_seq * self.lat_tiles

    def is_ctx(self, i):
        return i < self.n_ctx

    def lat_pos(self, i):
        j = jnp.maximum(i - self.n_ctx, 0)
        return j // self.lat_tiles, j % self.lat_tiles

    def phys_reversed(self, i):
        b, t = self.lat_pos(i)
        return jnp.where(i < self.n_ctx, i, self.n_ctx + b * self.lat_tiles + (self.lat_tiles - 1 - t))

    def mod_row(self, i):
        b, _ = self.lat_pos(i)
        return jnp.where(i < self.n_ctx, 0, 1 + b)


def _rope(x, cos, sin_signed):
    lane = lax.broadcasted_iota(I32, x.shape, 1)
    partner = jnp.where((lane & 63) < 32, pltpu.roll(x, 96, 1), pltpu.roll(x, 32, 1))
    return x * cos + partner * sin_signed


def _mix_a_body(tiles, xp_ref, xs_ref, mod_ref, gpre_ref, win_ref, cw_ref, cb_ref, dec_ref,
                cos_ref, sin_ref, s0b_ref,
                yconv_ref, q_ref, v_ref, g_ref, kt_ref, sbin_ref, sbfin_ref,
                sb_scr, tab_scr):
    i = pl.program_id(0)
    is_ctx = tiles.is_ctx(i)
    _, t_rev = tiles.lat_pos(i)
    first = jnp.logical_or(is_ctx, t_rev == 0)

    @pl.when(i == 0)
    def _():
        lg = _log_sigmoid(dec_ref[1])
        col = lax.broadcasted_iota(I32, lg.shape, 2).astype(F32)
        tab_scr[0] = jnp.exp(col * lg)
        tab_scr[1] = jnp.exp(float(CHUNK) * lg)

    @pl.when(first)
    def _():
        sb_scr[...] = jnp.where(is_ctx, 0.0, s0b_ref[0])

    x = jnp.where(is_ctx, xp_ref[...], xs_ref[...])
    h = (_rms(x, gpre_ref[...]) * (1.0 + mod_ref[0, 1:2, :]) + mod_ref[0, 0:1, :]).astype(BF16)

    def proj(k):
        return _dot(h, win_ref[:, k * 512:(k + 1) * 512])

    z = proj(1) * proj(2)
    row = lax.broadcasted_iota(I32, z.shape, 0)
    period = jnp.where(is_ctx, TM, GRID_W)
    pos = row & (period - 1)
    left = jnp.where(pos == 0, 0.0, pltpu.roll(z, 1, 0))
    right = jnp.where(pos == period - 1, 0.0, pltpu.roll(z, TM - 1, 0))
    zc = left * cw_ref[0:1, :] + z * cw_ref[1:2, :] + right * cw_ref[2:3, :] + cb_ref[...]
    yconv_ref[...] = (proj(0) * zc).astype(BF16)

    cos = cos_ref[0]
    sin = sin_ref[0]
    q = proj(3)
    k = proj(4)
    q = jnp.concatenate([_rope(q[:, hh * DK:(hh + 1) * DK], cos, sin) for hh in range(HEADS)], axis=1)
    k = jnp.concatenate([_rope(k[:, hh * DK:(hh + 1) * DK], cos, sin) for hh in range(HEADS)], axis=1)
    q_ref[...] = (q * (DK ** -0.5)).astype(BF16)
    kt = k.T
    kt_ref[...] = kt.astype(BF16)
    v = proj(5).astype(BF16)
    v_ref[...] = v
    g_ref[...] = proj(6)

    for c in (1, 0):
        for hh in range(HEADS):
            sbin_ref[c, hh] = sb_scr[hh].astype(BF16)
            kts = (kt[hh * DK:(hh + 1) * DK, c * CHUNK:(c + 1) * CHUNK] * tab_scr[0, hh]).astype(BF16)
            vc = v[c * CHUNK:(c + 1) * CHUNK, hh * DK:(hh + 1) * DK]
            sb_scr[hh] = sb_scr[hh] * tab_scr[1, hh] + _dot(kts, vc)

    @pl.when(is_ctx)
    def _():
        sbfin_ref[0] = sb_scr[...]


def _mix_b_body(tiles, xp_ref, xs_ref, mod_ref, q_ref, kt_ref, v_ref, g_ref, yconv_ref, sbin_ref,
                wout_ref, gpost_ref, gffn_ref, gn_ref, dec_ref, s0f_ref,
                xnew_ref, tok_ref, sffin_ref,
                sf_scr, tab_scr, ycat_scr):
    i = pl.program_id(0)
    is_ctx = tiles.is_ctx(i)
    _, t_pos = tiles.lat_pos(i)
    first = jnp.logical_or(is_ctx, t_pos == 0)

    @pl.when(i == 0)
    def _():
        lgf = _log_sigmoid(dec_ref[0])
        lgb = _log_sigmoid(dec_ref[1])
        row = lax.broadcasted_iota(I32, lgf.shape, 1)
        col = lax.broadcasted_iota(I32, lgf.shape, 2)
        d = (row - col).astype(F32)
        tab_scr[0] = (jnp.where(row >= col, jnp.exp(jnp.where(row >= col, d, 0.0) * lgf), 0.0)
                      + jnp.where(col >= row, jnp.exp(jnp.where(col >= row, -d, 0.0) * lgb), 0.0))
        tab_scr[1] = jnp.exp((row + 1).astype(F32) * lgf)
        tab_scr[2] = jnp.exp((CHUNK - row).astype(F32) * lgb)
        tab_scr[3] = jnp.exp((CHUNK - 1 - col).astype(F32) * lgf)
        tab_scr[4] = jnp.exp(float(CHUNK) * lgf)

    @pl.when(first)
    def _():
        sf_scr[...] = jnp.where(is_ctx, 0.0, s0f_ref[0])

    for c in range(TM // CHUNK):
        rows = slice(c * CHUNK, (c + 1) * CHUNK)
        for hh in range(HEADS):
            cols = slice(hh * DK, (hh + 1) * DK)
            qc = q_ref[rows, cols]
            ktc = kt_ref[cols, rows]
            vc = v_ref[rows, cols]
            att = (_dot(qc, ktc) * tab_scr[0, hh]).astype(BF16)
            o = (_dot(att, vc)
                 + tab_scr[1, hh] * _dot(qc, sf_scr[hh].astype(BF16))
                 + tab_scr[2, hh] * _dot(qc, sbin_ref[c, hh]))
            kts = (ktc.astype(F32) * tab_scr[3, hh]).astype(BF16)
            sf_scr[hh] = sf_scr[hh] * tab_scr[4, hh] + _dot(kts, vc)
            mu = jnp.mean(o, axis=-1, keepdims=True)
            dev = o - mu
            var = jnp.mean(dev * dev, axis=-1, keepdims=True)
            on = dev * lax.rsqrt(var + EPS) * gn_ref[:, cols]
            ycat_scr[rows, RET_W + hh * DK:RET_W + (hh + 1) * DK] = (_silu(g_ref[rows, cols]) * on).astype(BF16)
    ycat_scr[:, 0:CONV_W] = yconv_ref[...]

    @pl.when(is_ctx)
    def _():
        sffin_ref[0] = sf_scr[...]

    x = jnp.where(is_ctx, xp_ref[...], xs_ref[...])
    u = _dot(ycat_scr[...], wout_ref[...])
    xn = x + mod_ref[0, 2:3, :] * _rms(u, gpost_ref[...])
    xnew_ref[...] = xn
    tok_ref[...] = _rms(xn, gffn_ref[...]) * (1.0 + mod_ref[0, 4:5, :]) + mod_ref[0, 3:4, :]


def _token_mixer(tiles, xp2, xs2, mod3, g_pre, win_bf, conv_w, conv_b, dec, cos_t, sin_t,
                 s0f, s0b, wout_bf, g_post, g_ffn, gn_g):
    n_tok = tiles.n_tiles * TM
    n_ctx = tiles.n_ctx
    last_ctx = n_ctx - 1

    def full(shape):
        return pl.BlockSpec(shape, lambda i: (0,) * len(shape))

    def xp_spec(phys):
        return pl.BlockSpec((TM, D_MODEL), lambda i: (jnp.minimum(phys(i), last_ctx), 0))

    def xs_spec(phys):
        return pl.BlockSpec((TM, D_MODEL), lambda i: (jnp.maximum(phys(i) - n_ctx, 0), 0))

    mod_spec = pl.BlockSpec((1, 6, D_MODEL), lambda i: (tiles.mod_row(i), 0, 0))
    state_in = pl.BlockSpec((1, HEADS, DK, DK), lambda i: (tiles.lat_pos(i)[0], 0, 0, 0))
    state_out = pl.BlockSpec((1, HEADS, DK, DK), lambda i: (jnp.minimum(i, last_ctx), 0, 0, 0))

    rev = tiles.phys_reversed

    def rope_idx(i):
        _, t = tiles.lat_pos(i)
        return jnp.where(i < n_ctx, 0, 1 + (tiles.lat_tiles - 1 - t))

    rope_spec = pl.BlockSpec((1, TM, DK), lambda i: (rope_idx(i), 0, 0))

    def rows(width, phys):
        return pl.BlockSpec((TM, width), lambda i: (phys(i), 0))

    yconv, q, v, g, kt, sbin, sb_fin = pl.pallas_call(
        functools.partial(_mix_a_body, tiles),
        out_shape=(jax.ShapeDtypeStruct((n_tok, CONV_W), BF16),
                   jax.ShapeDtypeStruct((n_tok, RET_W), BF16),
                   jax.ShapeDtypeStruct((n_tok, RET_W), BF16),
                   jax.ShapeDtypeStruct((n_tok, RET_W), F32),
                   jax.ShapeDtypeStruct((RET_W, n_tok), BF16),
                   jax.ShapeDtypeStruct((n_tok // CHUNK, HEADS, DK, DK), BF16),
                   jax.ShapeDtypeStruct((n_ctx, HEADS, DK, DK), F32)),
        grid=(tiles.n_tiles,),
        in_specs=[xp_spec(rev), xs_spec(rev), mod_spec, full((1, D_MODEL)), full((D_MODEL, IN_COLS)),
                  full((3, CONV_W)), full((1, CONV_W)), full((2, HEADS, DK, DK)),
                  rope_spec, rope_spec, state_in],
        out_specs=(rows(CONV_W, rev), rows(RET_W, rev), rows(RET_W, rev), rows(RET_W, rev),
                   pl.BlockSpec((RET_W, TM), lambda i: (0, rev(i))),
                   pl.BlockSpec((TM // CHUNK, HEADS, DK, DK), lambda i: (rev(i), 0, 0, 0)),
                   state_out),
        scratch_shapes=[pltpu.VMEM((HEADS, DK, DK), F32), pltpu.VMEM((2, HEADS, DK, DK), F32)],
        compiler_params=_cparams(),
        name="mix_a",
    )(xp2, xs2, mod3, g_pre, win_bf, conv_w, conv_b, dec, cos_t, sin_t, s0b)

    ident = lambda i: i
    xnew, tok, sf_fin = pl.pallas_call(
        functools.partial(_mix_b_body, tiles),
        out_shape=(jax.ShapeDtypeStruct((n_tok, D_MODEL), F32),
                   jax.ShapeDtypeStruct((n_tok, D_MODEL), F32),
                   jax.ShapeDtypeStruct((n_ctx, HEADS, DK, DK), F32)),
        grid=(tiles.n_tiles,),
        in_specs=[xp_spec(ident), xs_spec(ident), mod_spec,
                  rows(RET_W, ident),
                  pl.BlockSpec((RET_W, TM), lambda i: (0, i)),
                  rows(RET_W, ident), rows(RET_W, ident), rows(CONV_W, ident),
                  pl.BlockSpec((TM // CHUNK, HEADS, DK, DK), lambda i: (i, 0, 0, 0)),
                  full((D_MODEL, D_MODEL)), full((1, D_MODEL)), full((1, D_MODEL)), full((1, RET_W)),
                  full((2, HEADS, DK, DK)), state_in],
        out_specs=(rows(D_MODEL, ident), rows(D_MODEL, ident), state_out),
        scratch_shapes=[pltpu.VMEM((HEADS, DK, DK), F32), pltpu.VMEM((5, HEADS, DK, DK), F32),
                        pltpu.VMEM((TM, D_MODEL), BF16)],
        compiler_params=_cparams(),
        name="mix_b",
    )(xp2, xs2, mod3, q, kt, v, g, yconv, sbin, wout_bf, g_post, g_ffn, gn_g, dec, s0f)
    return xnew, tok, sf_fin, sb_fin


def _route_body(tok_ref, rwt_ref, bias_ref, topi_ref, topw_ref):
    h = tok_ref[...].astype(BF16)
    logits = lax.dot_general(rwt_ref[...], h, (((1,), (1,)), ((), ())), preferred_element_type=F32)
    shape3 = (GROUP_SIZE, N_GROUPS, 128)
    member = lax.broadcasted_iota(I32, shape3, 0)
    group = lax.broadcasted_iota(I32, shape3, 1)
    expert = group * GROUP_SIZE + member
    group2 = lax.broadcasted_iota(I32, (N_GROUPS, 128), 0)
    neg = -jnp.inf
    for lb in range(RT // 128):
        scores = jax.nn.sigmoid(logits[:, lb * 128:(lb + 1) * 128]).reshape(shape3)
        biased = scores + bias_ref[...].reshape(shape3)
        m1 = jnp.max(biased, axis=0)
        first = jnp.min(jnp.where(biased == m1, member, GROUP_SIZE), axis=0)
        m2 = jnp.max(jnp.where(member == first, neg, biased), axis=0)
        gs = m1 + m2
        beaten = jnp.zeros(gs.shape, I32)
        for s in range(1, N_GROUPS):
            other = pltpu.roll(gs, s, 0)
            wins = (other > gs) | ((other == gs) & (group2 >= s))
            beaten = beaten + wins.astype(I32)
        keep = beaten < TOPK_GROUPS
        cand = jnp.where(keep, biased, neg)
        idx_rows, w_rows = [], []
        for _ in range(TOP_K):
            best = jnp.max(jnp.max(cand, axis=0), axis=0, keepdims=True)
            pick = jnp.min(jnp.min(jnp.where(cand == best, expert, N_EXPERTS), axis=0), axis=0, keepdims=True)
            hit = expert == pick
            w_rows.append(jnp.sum(jnp.sum(jnp.where(hit, scores, 0.0), axis=0), axis=0, keepdims=True))
            idx_rows.append(pick)
            cand = jnp.where(hit, neg, cand)
        w = jnp.concatenate(w_rows, axis=0)
        topi_ref[:, lb * 128:(lb + 1) * 128] = jnp.concatenate(idx_rows, axis=0)
        topw_ref[:, lb * 128:(lb + 1) * 128] = w / jnp.sum(w, axis=0, keepdims=True) * ROUTED_SCALE


def _route(tok, rwt_bf, bias_b):
    n_tok = tok.shape[0]
    return pl.pallas_call(
        _route_body,
        out_shape=(jax.ShapeDtypeStruct((TOP_K, n_tok), I32), jax.ShapeDtypeStruct((TOP_K, n_tok), F32)),
        grid=(n_tok // RT,),
        in_specs=[pl.BlockSpec((RT, D_MODEL), lambda i: (i, 0)),
                  pl.BlockSpec((N_EXPERTS, D_MODEL), lambda i: (0, 0)),
                  pl.BlockSpec((N_EXPERTS, 128), lambda i: (0, 0))],
        out_specs=(pl.BlockSpec((TOP_K, RT), lambda i: (0, i)), pl.BlockSpec((TOP_K, RT), lambda i: (0, i))),
        compiler_params=_cparams(),
        name="route",
    )(tok, rwt_bf, bias_b)


def _onehot(ids_row):
    e = lax.broadcasted_iota(I32, (N_EXPERTS, 256), 0)
    return e == ids_row


def _rank_body(topi_ref, rank_ref, counts_ref, run_scr):
    i = pl.program_id(0)

    @pl.when(i == 0)
    def _():
        run_scr[...] = jnp.zeros(run_scr.shape, F32)

    a0 = lax.broadcasted_iota(I32, (256, 256), 0)
    a1 = lax.broadcasted_iota(I32, (256, 256), 1)
    upper = (a0 <= a1).astype(BF16)
    ones = jnp.ones((256, 256), BF16)
    for k in range(TOP_K):
        for sb in range(RB // 256):
            lanes = slice(sb * 256, (sb + 1) * 256)
            oh = _onehot(topi_ref[k:k + 1, lanes])
            ohb = oh.astype(BF16)
            seen = _dot(ohb, upper) + run_scr[...]
            r = jnp.sum(jnp.where(oh, seen, 0.0), axis=0, keepdims=True) - 1.0
            rank_ref[k:k + 1, lanes] = r.astype(I32)
            run_scr[...] = run_scr[...] + _dot(ohb, ones)

    @pl.when(i == pl.num_programs(0) - 1)
    def _():
        counts_ref[...] = run_scr[:, 0:128]


def _dest_body(topi_ref, rank_ref, counts_ref, dest_ref, start_ref, start_scr):
    i = pl.program_id(0)

    @pl.when(i == 0)
    def _():
        c = counts_ref[...]
        d2 = jnp.floor(c / 16384.0)
        rem = c - d2 * 16384.0
        d1 = jnp.floor(rem / 128.0)
        d0 = rem - d1 * 128.0
        e0 = lax.broadcasted_iota(I32, (N_EXPERTS, N_EXPERTS), 0)
        e1 = lax.broadcasted_iota(I32, (N_EXPERTS, N_EXPERTS), 1)
        below = (e1 < e0).astype(BF16)
        start_scr[...] = (16384.0 * _dot(below, d2.astype(BF16)) + 128.0 * _dot(below, d1.astype(BF16))
                          + _dot(below, d0.astype(BF16)))
        start_ref[...] = start_scr[...]

    start = jnp.concatenate([start_scr[...], start_scr[...]], axis=1)
    for k in range(TOP_K):
        for sb in range(RB // 256):
            lanes = slice(sb * 256, (sb + 1) * 256)
            oh = _onehot(topi_ref[k:k + 1, lanes])
            base = jnp.sum(jnp.where(oh, start, 0.0), axis=0, keepdims=True)
            dest_ref[k:k + 1, lanes] = base.astype(I32) + rank_ref[k:k + 1, lanes]


def _dispatch_plan(topi):
    n_tok = topi.shape[1]
    blk = pl.BlockSpec((TOP_K, RB), lambda i: (0, i))
    whole = pl.BlockSpec((N_EXPERTS, 128), lambda i: (0, 0))
    rank, counts = pl.pallas_call(
        _rank_body,
        out_shape=(jax.ShapeDtypeStruct((TOP_K, n_tok), I32), jax.ShapeDtypeStruct((N_EXPERTS, 128), F32)),
        grid=(n_tok // RB,),
        in_specs=[blk],
        out_specs=(blk, whole),
        scratch_shapes=[pltpu.VMEM((N_EXPERTS, 256), F32)],
        compiler_params=_cparams(),
        name="rank",
    )(topi)
    dest, start = pl.pallas_call(
        _dest_body,
        out_shape=(jax.ShapeDtypeStruct((TOP_K, n_tok), I32), jax.ShapeDtypeStruct((N_EXPERTS, 128), F32)),
        grid=(n_tok // RB,),
        in_specs=[blk, blk, whole],
        out_specs=(blk, whole),
        scratch_shapes=[pltpu.VMEM((N_EXPERTS, 128), F32)],
        compiler_params=_cparams(),
        name="dest",
    )(topi, rank, counts)
    return dest, start, counts


ROW_TILES = D_MODEL // 128


def _transpose8(vs):
    sub = lax.broadcasted_iota(I32, (8, 128), 0)
    for d in (4, 2, 1):
        keep = (sub & d) == 0
        out = list(vs)
        for i in range(8):
            if i & d == 0:
                a, b = vs[i], vs[i + d]
                out[i] = jnp.where(keep, a, pltpu.roll(b, d, 0))
                out[i + d] = jnp.where(keep, pltpu.roll(a, 8 - d, 0), b)
        vs = out
    return vs


def _rows_from_tiles(tiles):
    n_rows = tiles.shape[0] // ROW_TILES
    groups = [_transpose8([tiles[(g * 8 + r) * ROW_TILES:(g * 8 + r + 1) * ROW_TILES] for r in range(8)])
              for g in range(n_rows // 8)]
    return jnp.concatenate([jnp.concatenate([grp[c] for grp in groups], axis=0) for c in range(ROW_TILES)], axis=1)


def _rows_to_tiles(value):
    n_rows = value.shape[0]
    pieces = []
    for g in range(n_rows // 8):
        pieces += _transpose8([value[g * 8:(g + 1) * 8, c * 128:(c + 1) * 128] for c in range(ROW_TILES)])
    return jnp.concatenate(pieces, axis=0)


def _start_row_gather(src_hbm, idx_of, dst, sem, n_rows, priority_of=lambda r: r % 2):
    for r in range(n_rows):
        src_row = pl.multiple_of(idx_of(r) * ROW_TILES, ROW_TILES)
        pltpu.make_async_copy(src_hbm.at[pl.ds(src_row, ROW_TILES)], dst.at[pl.ds(r * ROW_TILES, ROW_TILES)],
                              sem).start(priority=priority_of(r))


DISPATCH_ROWS = 128
DISPATCH_DEPTH = 4


def _dispatch_body(dest_ref, tok_hbm, xs_hbm, zbuf, sem, zsem):
    k = pl.program_id(0)
    i = pl.program_id(1)
    n_blk = pl.num_programs(1)
    step = k * n_blk + i
    n_steps = pl.num_programs(0) * n_blk
    blk_rows = DISPATCH_ROWS * ROW_TILES

    def batch_wait():
        pltpu.make_async_copy(tok_hbm.at[pl.ds(0, blk_rows)], xs_hbm.at[pl.ds(0, blk_rows)], sem).wait()

    @pl.when(step == 0)
    def _():
        zbuf[...] = jnp.zeros(zbuf.shape, F32)
        tail = pltpu.make_async_copy(zbuf, xs_hbm.at[pl.ds(xs_hbm.shape[0] - zbuf.shape[0], zbuf.shape[0])], zsem)
        tail.start()
        tail.wait()

    @pl.when(step >= DISPATCH_DEPTH)
    def _():
        batch_wait()

    a0 = step * DISPATCH_ROWS
    t0 = i * DISPATCH_ROWS
    batch = 16
    for r0 in range(0, DISPATCH_ROWS, batch):
        slots = [dest_ref[a0 + r0 + u] for u in range(batch)]
        for u in range(batch):
            dst_row = pl.multiple_of(slots[u] * ROW_TILES, ROW_TILES)
            src_row = pl.multiple_of((t0 + r0 + u) * ROW_TILES, ROW_TILES)
            pltpu.make_async_copy(tok_hbm.at[pl.ds(src_row, ROW_TILES)], xs_hbm.at[pl.ds(dst_row, ROW_TILES)],
                                  sem).start(priority=u % 2)

    @pl.when(step == n_steps - 1)
    def _():
        for _ in range(DISPATCH_DEPTH):
            batch_wait()


def _dispatch(dest_flat, tok_tiles, n_tok, n_rows_out):
    return pl.pallas_call(
        _dispatch_body,
        out_shape=jax.ShapeDtypeStruct((n_rows_out * ROW_TILES, 128), F32),
        grid_spec=pltpu.PrefetchScalarGridSpec(
            num_scalar_prefetch=1,
            grid=(TOP_K, n_tok // DISPATCH_ROWS),
            in_specs=[pl.BlockSpec(memory_space=pl.ANY)],
            out_specs=pl.BlockSpec(memory_space=pl.ANY),
            scratch_shapes=[pltpu.VMEM((SUB * ROW_TILES, 128), F32),
                            pltpu.SemaphoreType.DMA(()), pltpu.SemaphoreType.DMA(())]),
        compiler_params=_cparams(2),
        name="dispatch",
    )(dest_flat, tok_tiles)


N_XBUF = 4
LOOKAHEAD = 3


def _experts_body(start_ref, nsub_ref,
                  xs_hbm, wg_hbm, wu_hbm, wd_hbm, y_hbm,
                  xbuf, ybuf, wg_f32, wu_f32, wd_f32, wg_bf, wu_bf, wd_bf, cur, nxt, gsem, osem, wsem):
    e = pl.program_id(0)
    n_e = pl.num_programs(0)
    nsub = nsub_ref[e]
    sub_rows = SUB * ROW_TILES

    def weight_copies(ex, slot):
        return [pltpu.make_async_copy(src.at[ex], dst.at[slot], wsem.at[slot, n])
                for n, (src, dst) in enumerate(((wg_hbm, wg_f32), (wu_hbm, wu_f32), (wd_hbm, wd_f32)))]

    def window(hbm, ex, j):
        return hbm.at[pl.ds(pl.multiple_of((start_ref[ex] + j * SUB) * ROW_TILES, ROW_TILES), sub_rows)]

    def produce():
        pe = cur[0]

        @pl.when(pe < n_e)
        def _():
            pj = cur[1]
            pg = cur[2]
            slot = lax.rem(pg, N_XBUF)
            pltpu.make_async_copy(window(xs_hbm, pe, pj), xbuf.at[slot], gsem.at[slot]).start()
            last = pj + 1 >= nsub_ref[pe]
            cur[0] = jnp.where(last, nxt[pe], pe)
            cur[1] = jnp.where(last, 0, pj + 1)
            cur[2] = pg + 1

    def out_wait():
        pltpu.make_async_copy(ybuf.at[0], y_hbm.at[pl.ds(0, sub_rows)], osem).wait()

    @pl.when(e == 0)
    def _():
        def fill(i, following):
            x = N_EXPERTS - 1 - i
            nxt[x] = following
            return jnp.where(nsub_ref[x] > 0, x, following)

        cur[0] = lax.fori_loop(0, N_EXPERTS, fill, N_EXPERTS)
        cur[1] = 0
        cur[2] = 0
        cur[3] = 0
        for cp in weight_copies(0, 0):
            cp.start()
        for _ in range(LOOKAHEAD):
            produce()

    wslot = e & 1

    @pl.when(e + 1 < n_e)
    def _():
        for cp in weight_copies(e + 1, 1 - wslot):
            cp.start()

    for cp in weight_copies(e, wslot):
        cp.wait()
    wg_bf[...] = wg_f32[wslot].astype(BF16)
    wu_bf[...] = wu_f32[wslot].astype(BF16)
    wd_bf[...] = wd_f32[wslot].astype(BF16)

    def step(j, carry):
        g = cur[3]
        produce()
        slot = lax.rem(g, N_XBUF)
        yslot = g & 1
        pltpu.make_async_copy(xs_hbm.at[pl.ds(0, sub_rows)], xbuf.at[slot], gsem.at[slot]).wait()
        xb = _rows_from_tiles(xbuf[slot]).astype(BF16)
        a = _dot(xb, wg_bf[...])
        b = _dot(xb, wu_bf[...])
        ybuf[yslot] = _rows_to_tiles(_dot((_silu(a) * b).astype(BF16), wd_bf[...]))

        @pl.when(g > 0)
        def _():
            out_wait()

        pltpu.make_async_copy(ybuf.at[yslot], window(y_hbm, e, j), osem).start()
        cur[3] = g + 1
        return carry

    lax.fori_loop(0, nsub, step, 0)

    @pl.when(e == n_e - 1)
    def _():
        @pl.when(cur[3] > 0)
        def _():
            out_wait()

        ybuf[0] = jnp.zeros(ybuf.shape[1:], F32)
        tail = pltpu.make_async_copy(ybuf.at[0], y_hbm.at[pl.ds(y_hbm.shape[0] - sub_rows, sub_rows)], osem)
        tail.start()
        tail.wait()


def _experts(start, nsub, xs_tiles, wg, wu, wd):
    sub_rows = SUB * ROW_TILES
    return pl.pallas_call(
        _experts_body,
        out_shape=jax.ShapeDtypeStruct(xs_tiles.shape, F32),
        grid_spec=pltpu.PrefetchScalarGridSpec(
            num_scalar_prefetch=2,
            grid=(N_EXPERTS,),
            in_specs=[pl.BlockSpec(memory_space=pl.ANY)] * 4,
            out_specs=pl.BlockSpec(memory_space=pl.ANY),
            scratch_shapes=[pltpu.VMEM((N_XBUF, sub_rows, 128), F32), pltpu.VMEM((2, sub_rows, 128), F32),
                            pltpu.VMEM((2, D_MODEL, FF), F32), pltpu.VMEM((2, D_MODEL, FF), F32),
                            pltpu.VMEM((2, FF, D_MODEL), F32),
                            pltpu.VMEM((D_MODEL, FF), BF16), pltpu.VMEM((D_MODEL, FF), BF16),
                            pltpu.VMEM((FF, D_MODEL), BF16), pltpu.SMEM((4,), I32), pltpu.SMEM((N_EXPERTS,), I32),
                            pltpu.SemaphoreType.DMA((N_XBUF,)), pltpu.SemaphoreType.DMA(()),
                            pltpu.SemaphoreType.DMA((2, 3))]),
        compiler_params=_cparams(),
        name="experts",
    )(start, nsub, xs_tiles, wg, wu, wd)


def _combine_body(n_tok, n_ctx_tiles, dest_ref,
                  y_hbm, topw_ref, tok_ref, xnew_ref, mod_ref, sg_ref, su_ref, sd_ref, gpost_ref,
                  outp_ref, outs_ref, gbuf, wcol, fbuf, gsem):
    i = pl.program_id(0)
    n = pl.num_programs(0)

    def gather(tile, slot):
        def per_choice(k, carry):
            base = k * n_tok + tile * TT
            _start_row_gather(y_hbm, lambda r: dest_ref[base + r], gbuf.at[slot, k], gsem.at[slot], TT)
            return carry
        lax.fori_loop(0, TOP_K, per_choice, 0)

    def gather_wait(slot):
        for k in range(TOP_K):
            pltpu.make_async_copy(y_hbm.at[pl.ds(0, TT * ROW_TILES)], gbuf.at[slot, k], gsem.at[slot]).wait()

    slot = i & 1

    @pl.when(i == 0)
    def _():
        gather(0, 0)

    nxt_tile = jnp.minimum(i + 1, n - 1)
    for k in range(TOP_K):
        base = k * n_tok + nxt_tile * TT
        _start_row_gather(y_hbm, lambda r: dest_ref[base + r], gbuf.at[1 - slot, k], gsem.at[1 - slot], TT)

    w_t = jnp.concatenate([topw_ref[...], jnp.zeros((128 - TOP_K, TT), F32)], axis=0).T
    for k in range(TOP_K):
        wcol[k] = jnp.broadcast_to(w_t[:, k:k + 1], (TT, 128))
    h = tok_ref[...].astype(BF16)
    fbuf[...] = _dot((_silu(_dot(h, sg_ref[...])) * _dot(h, su_ref[...])).astype(BF16), sd_ref[...])

    gather_wait(slot)
    for g in range(TT // 8):
        rows = slice(g * 8, (g + 1) * 8)
        tiles = []
        for r in range(8):
            t = g * 8 + r
            acc = None
            for k in range(TOP_K):
                w = jnp.broadcast_to(wcol[k, t:t + 1, :], (8, 128))
                term = gbuf[slot, k, t * ROW_TILES:(t + 1) * ROW_TILES, :] * w
                acc = term if acc is None else acc + term
            tiles.append(acc)
        f = fbuf[rows, :] + jnp.concatenate(_transpose8(tiles), axis=1)
        fbuf[rows, :] = xnew_ref[rows, :] + mod_ref[0, 5:6, :] * _rms(f, gpost_ref[...])

    @pl.when(i == n - 1)
    def _():
        gather_wait(1 - slot)

    @pl.when(i < n_ctx_tiles)
    def _():
        outp_ref[...] = fbuf[...]

    @pl.when(i >= n_ctx_tiles)
    def _():
        outs_ref[...] = fbuf[...]


def _combine(tiles, dest_flat, ysorted, topw, tok, xnew, mod3, sg_bf, su_bf, sd_bf, g_post):
    n_tok = tok.shape[0]
    n_ctx_tok = tiles.n_ctx * TM
    n_ctx_tiles = n_ctx_tok // TT
    lat_tiles_per_seq = tiles.lat_tiles * TM // TT

    def mod_row(i):
        return jnp.where(i < n_ctx_tiles, 0, 1 + jnp.maximum(i - n_ctx_tiles, 0) // lat_tiles_per_seq)

    def full(shape):
        return pl.BlockSpec(shape, lambda i, *_: (0,) * len(shape))

    rows = pl.BlockSpec((TT, D_MODEL), lambda i, *_: (i, 0))
    return pl.pallas_call(
        functools.partial(_combine_body, n_tok, n_ctx_tiles),
        out_shape=(jax.ShapeDtypeStruct((n_ctx_tok, D_MODEL), F32),
                   jax.ShapeDtypeStruct((n_tok - n_ctx_tok, D_MODEL), F32)),
        grid_spec=pltpu.PrefetchScalarGridSpec(
            num_scalar_prefetch=1,
            grid=(n_tok // TT,),
            in_specs=[pl.BlockSpec(memory_space=pl.ANY),
                      pl.BlockSpec((TOP_K, TT), lambda i, *_: (0, i)),
                      rows, rows,
                      pl.BlockSpec((1, 6, D_MODEL), lambda i, *_: (mod_row(i), 0, 0)),
                      full((D_MODEL, FF)), full((D_MODEL, FF)), full((FF, D_MODEL)), full((1, D_MODEL))],
            out_specs=(pl.BlockSpec((TT, D_MODEL), lambda i, *_: (jnp.minimum(i, n_ctx_tiles - 1), 0)),
                       pl.BlockSpec((TT, D_MODEL), lambda i, *_: (jnp.maximum(i - n_ctx_tiles, 0), 0))),
            scratch_shapes=[pltpu.VMEM((2, TOP_K, TT * ROW_TILES, 128), F32), pltpu.VMEM((TOP_K, TT, 128), F32),
                            pltpu.VMEM((TT, D_MODEL), F32), pltpu.SemaphoreType.DMA((2,))]),
        compiler_params=_cparams(),
        name="combine",
    )(dest_flat, ysorted, topw, tok, xnew, mod3, sg_bf, su_bf, sd_bf, g_post)


def _rope_tables(lat_len, lat_tiles):
    rows = lat_len // GRID_W
    row = jnp.repeat(jnp.arange(rows, dtype=F32), GRID_W)
    col = jnp.tile(jnp.arange(GRID_W, dtype=F32), rows)
    inv = ROPE_BASE ** (-jnp.arange(ROPE_FREQS, dtype=F32) / ROPE_FREQS)
    ang = jnp.concatenate([row[:, None] * inv[None, :]] * 2 + [col[:, None] * inv[None, :]] * 2, axis=1)
    sign = jnp.tile(jnp.concatenate([-jnp.ones((ROPE_FREQS,), F32), jnp.ones((ROPE_FREQS,), F32)]), 2)
    cos = jnp.cos(ang).reshape(lat_tiles, TM, DK)
    sin = (jnp.sin(ang) * sign[None, :]).reshape(lat_tiles, TM, DK)
    cos = jnp.concatenate([jnp.ones((1, TM, DK), F32), cos], axis=0)
    sin = jnp.concatenate([jnp.zeros((1, TM, DK), F32), sin], axis=0)
    return cos, sin


def kernel(x_prompt, x_sample, state_ret_fwd, state_ret_bwd, c, c_ctx, w_mod, b_mod, norm_mix_pre,
           norm_mix_post, norm_ffn_pre, norm_ffn_post, w_in, conv_w, conv_b, ret_decay_fwd,
           ret_decay_bwd, ret_gn_g, w_out, router_w, router_bias, expert_w_gate, expert_w_up,
           expert_w_down, shared_w_gate, shared_w_up, shared_w_down):
    bp, tp, d = x_prompt.shape
    bs, ts, _ = x_sample.shape
    depth = w_mod.shape[0]
    assert d == D_MODEL and tp == TM and ts % TM == 0 and bs + 1 <= 8
    tiles = _Tiles(bp, bs, ts)
    n_tok = tiles.n_tiles * TM
    n_slots = n_tok * TOP_K + SUB
    cos_t, sin_t = _rope_tables(ts, tiles.lat_tiles)

    perm = (jnp.arange(N_EXPERTS) % N_GROUPS) * GROUP_SIZE + jnp.arange(N_EXPERTS) // N_GROUPS

    xp2 = x_prompt.reshape(bp * tp, d)
    xs2 = x_sample.reshape(bs * ts, d)
    new_f, new_b = [], []
    for l in range(depth):
        c_rows = jnp.concatenate([c_ctx[None, :], c, jnp.zeros((8 - 1 - bs, d), F32)], axis=0)
        mod3 = _modulation(c_rows, w_mod[l], b_mod[l][None, :]).reshape(8, 6, d)
        dec = jnp.broadcast_to(jnp.stack([ret_decay_fwd[l], ret_decay_bwd[l]])[:, :, None, None],
                               (2, HEADS, DK, DK)).astype(F32)
        xnew, tok, sf_fin, sb_fin = _token_mixer(
            tiles, xp2, xs2, mod3, norm_mix_pre[l][None, :], w_in[l].astype(BF16), conv_w[l],
            conv_b[l][None, :], dec, cos_t, sin_t, state_ret_fwd[:, l], state_ret_bwd[:, l],
            w_out[l].astype(BF16), norm_mix_post[l][None, :], norm_ffn_pre[l][None, :], ret_gn_g[l][None, :])
        new_f.append(sf_fin)
        new_b.append(sb_fin)

        rwt = router_w[l].T[perm].astype(BF16)
        bias_b = jnp.broadcast_to(router_bias[l][perm][:, None], (N_EXPERTS, 128)).astype(F32)
        topi, topw = _route(tok, rwt, bias_b)
        dest, start, counts = _dispatch_plan(topi)
        dest_flat = dest.reshape(-1)
        start_i = start[:, 0].astype(I32)
        nsub = ((counts[:, 0] + float(SUB - 1)) / float(SUB)).astype(I32)
        tok_tiles = tok.reshape(n_tok * ROW_TILES, 128)
        xs_tiles = _dispatch(dest_flat, tok_tiles, n_tok, n_slots)
        ysorted = _experts(start_i, nsub, xs_tiles, expert_w_gate[l], expert_w_up[l], expert_w_down[l])
        xp2, xs2 = _combine(tiles, dest_flat, ysorted, topw, tok, xnew, mod3,
                            shared_w_gate[l].astype(BF16), shared_w_up[l].astype(BF16),
                            shared_w_down[l].astype(BF16), norm_ffn_post[l][None, :])

    return (xp2.reshape(bp, tp, d), xs2.reshape(bs, ts, d),
            jnp.stack(new_f, axis=1), jnp.stack(new_b, axis=1))
```

```python
import functools

import jax
import jax.numpy as jnp
from jax import lax
from jax.experimental import pallas as pl
from jax.experimental.pallas import tpu as pltpu

F32 = jnp.float32
BF16 = jnp.bfloat16
I32 = jnp.int32

D_MODEL = 1024
CONV_W = 512
RET_W = 512
HEADS = 4
DK = 128
CHUNK = 128
GRID_W = 64
ROPE_FREQS = 32
ROPE_BASE = 10000.0
IN_COLS = 3 * CONV_W + 4 * RET_W
N_EXPERTS = 256
N_GROUPS = 8
GROUP_SIZE = N_EXPERTS // N_GROUPS
TOPK_GROUPS = 4
TOP_K = 8
FF = 256
ROUTED_SCALE = 2.5
EPS = 1e-6

TM = 256
SUB = 128
TT = 128
RT = 512
RB = 2048
VMEM_LIMIT = 56 * 1024 * 1024


def _cparams(n_axes=1, vmem=VMEM_LIMIT):
    return pltpu.CompilerParams(dimension_semantics=("arbitrary",) * n_axes,
                                vmem_limit_bytes=vmem)


def _silu(x):
    return x * jax.nn.sigmoid(x)


def _log_sigmoid(x):
    return jnp.minimum(x, 0.0) - jnp.log1p(jnp.exp(-jnp.abs(x)))


def _rms(x, g):
    return x * lax.rsqrt(jnp.mean(x * x, axis=-1, keepdims=True) + EPS) * g


def _dot(a, b):
    return jnp.dot(a, b, preferred_element_type=F32)


def _mod_body(c_ref, w_ref, b_ref, o_ref):
    s = _silu(c_ref[...]).astype(BF16)
    o_ref[...] = _dot(s, w_ref[...].astype(BF16)) + b_ref[...]


def _modulation(c_rows, w_mod, b_mod):
    n_col = w_mod.shape[1]
    blk = 1536
    return pl.pallas_call(
        _mod_body,
        out_shape=jax.ShapeDtypeStruct((8, n_col), F32),
        grid=(n_col // blk,),
        in_specs=[pl.BlockSpec((8, D_MODEL), lambda i: (0, 0)),
                  pl.BlockSpec((D_MODEL, blk), lambda i: (0, i)),
                  pl.BlockSpec((1, blk), lambda i: (0, i))],
        out_specs=pl.BlockSpec((8, blk), lambda i: (0, i)),
        compiler_params=_cparams(),
        name="mod",
    )(c_rows, w_mod, b_mod)


class _Tiles:
    def __init__(self, n_ctx_seq, n_lat_seq, lat_len):
        self.n_ctx = n_ctx_seq
        self.lat_tiles = lat_len // TM
        self.n_lat_seq = n_lat_seq
        self.n_tiles = n_ctx_seq + n_lat_seq * self.lat_tiles

    def is_ctx(self, i):
        return i < self.n_ctx

    def lat_pos(self, i):
        j = jnp.maximum(i - self.n_ctx, 0)
        return j // self.lat_tiles, j % self.lat_tiles

    def phys_reversed(self, i):
        b, t = self.lat_pos(i)
        return jnp.where(i < self.n_ctx, i, self.n_ctx + b * self.lat_tiles + (self.lat_tiles - 1 - t))

    def mod_row(self, i):
        b, _ = self.lat_pos(i)
        return jnp.where(i < self.n_ctx, 0, 1 + b)


def _rope(x, cos, sin_signed):
    lane = lax.broadcasted_iota(I32, x.shape, 1)
    partner = jnp.where((lane & 63) < 32, pltpu.roll(x, 96, 1), pltpu.roll(x, 32, 1))
    return x * cos + partner * sin_signed


def _mix_a_body(tiles, xp_ref, xs_ref, mod_ref, gpre_ref, win_ref, cw_ref, cb_ref, dec_ref,
                cos_ref, sin_ref, s0b_ref,
                yconv_ref, q_ref, v_ref, g_ref, kt_ref, sbin_ref, sbfin_ref,
                sb_scr, tab_scr):
    i = pl.program_id(0)
    is_ctx = tiles.is_ctx(i)
    _, t_rev = tiles.lat_pos(i)
    first = jnp.logical_or(is_ctx, t_rev == 0)

    @pl.when(i == 0)
    def _():
        lg = _log_sigmoid(dec_ref[1])
        col = lax.broadcasted_iota(I32, lg.shape, 2).astype(F32)
        tab_scr[0] = jnp.exp(col * lg)
        tab_scr[1] = jnp.exp(float(CHUNK) * lg)

    @pl.when(first)
    def _():
        sb_scr[...] = jnp.where(is_ctx, 0.0, s0b_ref[0])

    x = jnp.where(is_ctx, xp_ref[...], xs_ref[...])
    h = (_rms(x, gpre_ref[...]) * (1.0 + mod_ref[0, 1:2, :]) + mod_ref[0, 0:1, :]).astype(BF16)

    def proj(k):
        return _dot(h, win_ref[:, k * 512:(k + 1) * 512])

    z = proj(1) * proj(2)
    row = lax.broadcasted_iota(I32, z.shape, 0)
    period = jnp.where(is_ctx, TM, GRID_W)
    pos = row & (period - 1)
    left = jnp.where(pos == 0, 0.0, pltpu.roll(z, 1, 0))
    right = jnp.where(pos == period - 1, 0.0, pltpu.roll(z, TM - 1, 0))
    zc = left * cw_ref[0:1, :] + z * cw_ref[1:2, :] + right * cw_ref[2:3, :] + cb_ref[...]
    yconv_ref[...] = (proj(0) * zc).astype(BF16)

    cos = cos_ref[0]
    sin = sin_ref[0]
    q = proj(3)
    k = proj(4)
    q = jnp.concatenate([_rope(q[:, hh * DK:(hh + 1) * DK], cos, sin) for hh in range(HEADS)], axis=1)
    k = jnp.concatenate([_rope(k[:, hh * DK:(hh + 1) * DK], cos, sin) for hh in range(HEADS)], axis=1)
    q_ref[...] = (q * (DK ** -0.5)).astype(BF16)
    kt = k.T
    kt_ref[...] = kt.astype(BF16)
    v = proj(5).astype(BF16)
    v_ref[...] = v
    g_ref[...] = proj(6)

    for c in (1, 0):
        for hh in range(HEADS):
            sbin_ref[c, hh] = sb_scr[hh].astype(BF16)
            kts = (kt[hh * DK:(hh + 1) * DK, c * CHUNK:(c + 1) * CHUNK] * tab_scr[0, hh]).astype(BF16)
            vc = v[c * CHUNK:(c + 1) * CHUNK, hh * DK:(hh + 1) * DK]
            sb_scr[hh] = sb_scr[hh] * tab_scr[1, hh] + _dot(kts, vc)

    @pl.when(is_ctx)
    def _():
        sbfin_ref[0] = sb_scr[...]


def _mix_b_body(tiles, xp_ref, xs_ref, mod_ref, q_ref, kt_ref, v_ref, g_ref, yconv_ref, sbin_ref,
                wout_ref, gpost_ref, gffn_ref, gn_ref, dec_ref, s0f_ref,
                xnew_ref, tok_ref, sffin_ref,
                sf_scr, tab_scr, ycat_scr):
    i = pl.program_id(0)
    is_ctx = tiles.is_ctx(i)
    _, t_pos = tiles.lat_pos(i)
    first = jnp.logical_or(is_ctx, t_pos == 0)

    @pl.when(i == 0)
    def _():
        lgf = _log_sigmoid(dec_ref[0])
        lgb = _log_sigmoid(dec_ref[1])
        row = lax.broadcasted_iota(I32, lgf.shape, 1)
        col = lax.broadcasted_iota(I32, lgf.shape, 2)
        d = (row - col).astype(F32)
        tab_scr[0] = (jnp.where(row >= col, jnp.exp(jnp.where(row >= col, d, 0.0) * lgf), 0.0)
                      + jnp.where(col >= row, jnp.exp(jnp.where(col >= row, -d, 0.0) * lgb), 0.0))
        tab_scr[1] = jnp.exp((row + 1).astype(F32) * lgf)
        tab_scr[2] = jnp.exp((CHUNK - row).astype(F32) * lgb)
        tab_scr[3] = jnp.exp((CHUNK - 1 - col).astype(F32) * lgf)
        tab_scr[4] = jnp.exp(float(CHUNK) * lgf)

    @pl.when(first)
    def _():
        sf_scr[...] = jnp.where(is_ctx, 0.0, s0f_ref[0])

    for c in range(TM // CHUNK):
        rows = slice(c * CHUNK, (c + 1) * CHUNK)
        for hh in range(HEADS):
            cols = slice(hh * DK, (hh + 1) * DK)
            qc = q_ref[rows, cols]
            ktc = kt_ref[cols, rows]
            vc = v_ref[rows, cols]
            att = (_dot(qc, ktc) * tab_scr[0, hh]).astype(BF16)
            o = (_dot(att, vc)
                 + tab_scr[1, hh] * _dot(qc, sf_scr[hh].astype(BF16))
                 + tab_scr[2, hh] * _dot(qc, sbin_ref[c, hh]))
            kts = (ktc.astype(F32) * tab_scr[3, hh]).astype(BF16)
            sf_scr[hh] = sf_scr[hh] * tab_scr[4, hh] + _dot(kts, vc)
            mu = jnp.mean(o, axis=-1, keepdims=True)
            dev = o - mu
            var = jnp.mean(dev * dev, axis=-1, keepdims=True)
            on = dev * lax.rsqrt(var + EPS) * gn_ref[:, cols]
            ycat_scr[rows, RET_W + hh * DK:RET_W + (hh + 1) * DK] = (_silu(g_ref[rows, cols]) * on).astype(BF16)
    ycat_scr[:, 0:CONV_W] = yconv_ref[...]

    @pl.when(is_ctx)
    def _():
        sffin_ref[0] = sf_scr[...]

    x = jnp.where(is_ctx, xp_ref[...], xs_ref[...])
    u = _dot(ycat_scr[...], wout_ref[...])
    xn = x + mod_ref[0, 2:3, :] * _rms(u, gpost_ref[...])
    xnew_ref[...] = xn
    tok_ref[...] = _rms(xn, gffn_ref[...]) * (1.0 + mod_ref[0, 4:5, :]) + mod_ref[0, 3:4, :]


def _token_mixer(tiles, xp2, xs2, mod3, g_pre, win_bf, conv_w, conv_b, dec, cos_t, sin_t,
                 s0f, s0b, wout_bf, g_post, g_ffn, gn_g):
    n_tok = tiles.n_tiles * TM
    n_ctx = tiles.n_ctx
    last_ctx = n_ctx - 1

    def full(shape):
        return pl.BlockSpec(shape, lambda i: (0,) * len(shape))

    def xp_spec(phys):
        return pl.BlockSpec((TM, D_MODEL), lambda i: (jnp.minimum(phys(i), last_ctx), 0))

    def xs_spec(phys):
        return pl.BlockSpec((TM, D_MODEL), lambda i: (jnp.maximum(phys(i) - n_ctx, 0), 0))

    mod_spec = pl.BlockSpec((1, 6, D_MODEL), lambda i: (tiles.mod_row(i), 0, 0))
    state_in = pl.BlockSpec((1, HEADS, DK, DK), lambda i: (tiles.lat_pos(i)[0], 0, 0, 0))
    state_out = pl.BlockSpec((1, HEADS, DK, DK), lambda i: (jnp.minimum(i, last_ctx), 0, 0, 0))

    rev = tiles.phys_reversed

    def rope_idx(i):
        _, t = tiles.lat_pos(i)
        return jnp.where(i < n_ctx, 0, 1 + (tiles.lat_tiles - 1 - t))

    rope_spec = pl.BlockSpec((1, TM, DK), lambda i: (rope_idx(i), 0, 0))

    def rows(width, phys):
        return pl.BlockSpec((TM, width), lambda i: (phys(i), 0))

    yconv, q, v, g, kt, sbin, sb_fin = pl.pallas_call(
        functools.partial(_mix_a_body, tiles),
        out_shape=(jax.ShapeDtypeStruct((n_tok, CONV_W), BF16),
                   jax.ShapeDtypeStruct((n_tok, RET_W), BF16),
                   jax.ShapeDtypeStruct((n_tok, RET_W), BF16),
                   jax.ShapeDtypeStruct((n_tok, RET_W), F32),
                   jax.ShapeDtypeStruct((RET_W, n_tok), BF16),
                   jax.ShapeDtypeStruct((n_tok // CHUNK, HEADS, DK, DK), BF16),
                   jax.ShapeDtypeStruct((n_ctx, HEADS, DK, DK), F32)),
        grid=(tiles.n_tiles,),
        in_specs=[xp_spec(rev), xs_spec(rev), mod_spec, full((1, D_MODEL)), full((D_MODEL, IN_COLS)),
                  full((3, CONV_W)), full((1, CONV_W)), full((2, HEADS, DK, DK)),
                  rope_spec, rope_spec, state_in],
        out_specs=(rows(CONV_W, rev), rows(RET_W, rev), rows(RET_W, rev), rows(RET_W, rev),
                   pl.BlockSpec((RET_W, TM), lambda i: (0, rev(i))),
                   pl.BlockSpec((TM // CHUNK, HEADS, DK, DK), lambda i: (rev(i), 0, 0, 0)),
                   state_out),
        scratch_shapes=[pltpu.VMEM((HEADS, DK, DK), F32), pltpu.VMEM((2, HEADS, DK, DK), F32)],
        compiler_params=_cparams(),
        name="mix_a",
    )(xp2, xs2, mod3, g_pre, win_bf, conv_w, conv_b, dec, cos_t, sin_t, s0b)

    ident = lambda i: i
    xnew, tok, sf_fin = pl.pallas_call(
        functools.partial(_mix_b_body, tiles),
        out_shape=(jax.ShapeDtypeStruct((n_tok, D_MODEL), F32),
                   jax.ShapeDtypeStruct((n_tok, D_MODEL), F32),
                   jax.ShapeDtypeStruct((n_ctx, HEADS, DK, DK), F32)),
        grid=(tiles.n_tiles,),
        in_specs=[xp_spec(ident), xs_spec(ident), mod_spec,
                  rows(RET_W, ident),
                  pl.BlockSpec((RET_W, TM), lambda i: (0, i)),
                  rows(RET_W, ident), rows(RET_W, ident), rows(CONV_W, ident),
                  pl.BlockSpec((TM // CHUNK, HEADS, DK, DK), lambda i: (i, 0, 0, 0)),
                  full((D_MODEL, D_MODEL)), full((1, D_MODEL)), full((1, D_MODEL)), full((1, RET_W)),
                  full((2, HEADS, DK, DK)), state_in],
        out_specs=(rows(D_MODEL, ident), rows(D_MODEL, ident), state_out),
        scratch_shapes=[pltpu.VMEM((HEADS, DK, DK), F32), pltpu.VMEM((5, HEADS, DK, DK), F32),
                        pltpu.VMEM((TM, D_MODEL), BF16)],
        compiler_params=_cparams(),
        name="mix_b",
    )(xp2, xs2, mod3, q, kt, v, g, yconv, sbin, wout_bf, g_post, g_ffn, gn_g, dec, s0f)
    return xnew, tok, sf_fin, sb_fin


def _route_body(tok_ref, rwt_ref, bias_ref, topi_ref, topw_ref):
    h = tok_ref[...].astype(BF16)
    logits = lax.dot_general(rwt_ref[...], h, (((1,), (1,)), ((), ())), preferred_element_type=F32)
    shape3 = (GROUP_SIZE, N_GROUPS, 128)
    member = lax.broadcasted_iota(I32, shape3, 0)
    group = lax.broadcasted_iota(I32, shape3, 1)
    expert = group * GROUP_SIZE + member
    group2 = lax.broadcasted_iota(I32, (N_GROUPS, 128), 0)
    neg = -jnp.inf
    for lb in range(RT // 128):
        scores = jax.nn.sigmoid(logits[:, lb * 128:(lb + 1) * 128]).reshape(shape3)
        biased = scores + bias_ref[...].reshape(shape3)
        m1 = jnp.max(biased, axis=0)
        first = jnp.min(jnp.where(biased == m1, member, GROUP_SIZE), axis=0)
        m2 = jnp.max(jnp.where(member == first, neg, biased), axis=0)
        gs = m1 + m2
        beaten = jnp.zeros(gs.shape, I32)
        for s in range(1, N_GROUPS):
            other = pltpu.roll(gs, s, 0)
            wins = (other > gs) | ((other == gs) & (group2 >= s))
            beaten = beaten + wins.astype(I32)
        keep = beaten < TOPK_GROUPS
        cand = jnp.where(keep, biased, neg)
        idx_rows, w_rows = [], []
        for _ in range(TOP_K):
            best = jnp.max(jnp.max(cand, axis=0), axis=0, keepdims=True)
            pick = jnp.min(jnp.min(jnp.where(cand == best, expert, N_EXPERTS), axis=0), axis=0, keepdims=True)
            hit = expert == pick
            w_rows.append(jnp.sum(jnp.sum(jnp.where(hit, scores, 0.0), axis=0), axis=0, keepdims=True))
            idx_rows.append(pick)
            cand = jnp.where(hit, neg, cand)
        w = jnp.concatenate(w_rows, axis=0)
        topi_ref[:, lb * 128:(lb + 1) * 128] = jnp.concatenate(idx_rows, axis=0)
        topw_ref[:, lb * 128:(lb + 1) * 128] = w / jnp.sum(w, axis=0, keepdims=True) * ROUTED_SCALE


def _route(tok, rwt_bf, bias_b):
    n_tok = tok.shape[0]
    return pl.pallas_call(
        _route_body,
        out_shape=(jax.ShapeDtypeStruct((TOP_K, n_tok), I32), jax.ShapeDtypeStruct((TOP_K, n_tok), F32)),
        grid=(n_tok // RT,),
        in_specs=[pl.BlockSpec((RT, D_MODEL), lambda i: (i, 0)),
                  pl.BlockSpec((N_EXPERTS, D_MODEL), lambda i: (0, 0)),
                  pl.BlockSpec((N_EXPERTS, 128), lambda i: (0, 0))],
        out_specs=(pl.BlockSpec((TOP_K, RT), lambda i: (0, i)), pl.BlockSpec((TOP_K, RT), lambda i: (0, i))),
        compiler_params=_cparams(),
        name="route",
    )(tok, rwt_bf, bias_b)


def _onehot(ids_row):
    e = lax.broadcasted_iota(I32, (N_EXPERTS, 256), 0)
    return e == ids_row


def _rank_body(topi_ref, rank_ref, counts_ref, run_scr):
    i = pl.program_id(0)

    @pl.when(i == 0)
    def _():
        run_scr[...] = jnp.zeros(run_scr.shape, F32)

    a0 = lax.broadcasted_iota(I32, (256, 256), 0)
    a1 = lax.broadcasted_iota(I32, (256, 256), 1)
    upper = (a0 <= a1).astype(BF16)
    ones = jnp.ones((256, 256), BF16)
    for k in range(TOP_K):
        for sb in range(RB // 256):
            lanes = slice(sb * 256, (sb + 1) * 256)
            oh = _onehot(topi_ref[k:k + 1, lanes])
            ohb = oh.astype(BF16)
            seen = _dot(ohb, upper) + run_scr[...]
            r = jnp.sum(jnp.where(oh, seen, 0.0), axis=0, keepdims=True) - 1.0
            rank_ref[k:k + 1, lanes] = r.astype(I32)
            run_scr[...] = run_scr[...] + _dot(ohb, ones)

    @pl.when(i == pl.num_programs(0) - 1)
    def _():
        counts_ref[...] = run_scr[:, 0:128]


def _dest_body(topi_ref, rank_ref, counts_ref, dest_ref, start_ref, start_scr):
    i = pl.program_id(0)

    @pl.when(i == 0)
    def _():
        c = counts_ref[...]
        d2 = jnp.floor(c / 16384.0)
        rem = c - d2 * 16384.0
        d1 = jnp.floor(rem / 128.0)
        d0 = rem - d1 * 128.0
        e0 = lax.broadcasted_iota(I32, (N_EXPERTS, N_EXPERTS), 0)
        e1 = lax.broadcasted_iota(I32, (N_EXPERTS, N_EXPERTS), 1)
        below = (e1 < e0).astype(BF16)
        start_scr[...] = (16384.0 * _dot(below, d2.astype(BF16)) + 128.0 * _dot(below, d1.astype(BF16))
                          + _dot(below, d0.astype(BF16)))
        start_ref[...] = start_scr[...]

    start = jnp.concatenate([start_scr[...], start_scr[...]], axis=1)
    for k in range(TOP_K):
        for sb in range(RB // 256):
            lanes = slice(sb * 256, (sb + 1) * 256)
            oh = _onehot(topi_ref[k:k + 1, lanes])
            base = jnp.sum(jnp.where(oh, start, 0.0), axis=0, keepdims=True)
            dest_ref[k:k + 1, lanes] = base.astype(I32) + rank_ref[k:k + 1, lanes]


def _dispatch_plan(topi):
    n_tok = topi.shape[1]
    blk = pl.BlockSpec((TOP_K, RB), lambda i: (0, i))
    whole = pl.BlockSpec((N_EXPERTS, 128), lambda i: (0, 0))
    rank, counts = pl.pallas_call(
        _rank_body,
        out_shape=(jax.ShapeDtypeStruct((TOP_K, n_tok), I32), jax.ShapeDtypeStruct((N_EXPERTS, 128), F32)),
        grid=(n_tok // RB,),
        in_specs=[blk],
        out_specs=(blk, whole),
        scratch_shapes=[pltpu.VMEM((N_EXPERTS, 256), F32)],
        compiler_params=_cparams(),
        name="rank",
    )(topi)
    dest, start = pl.pallas_call(
        _dest_body,
        out_shape=(jax.ShapeDtypeStruct((TOP_K, n_tok), I32), jax.ShapeDtypeStruct((N_EXPERTS, 128), F32)),
        grid=(n_tok // RB,),
        in_specs=[blk, blk, whole],
        out_specs=(blk, whole),
        scratch_shapes=[pltpu.VMEM((N_EXPERTS, 128), F32)],
        compiler_params=_cparams(),
        name="dest",
    )(topi, rank, counts)
    return dest, start, counts


ROW_TILES = D_MODEL // 128


def _transpose8(vs):
    sub = lax.broadcasted_iota(I32, (8, 128), 0)
    for d in (4, 2, 1):
        keep = (sub & d) == 0
        out = list(vs)
        for i in range(8):
            if i & d == 0:
                a, b = vs[i], vs[i + d]
                out[i] = jnp.where(keep, a, pltpu.roll(b, d, 0))
                out[i + d] = jnp.where(keep, pltpu.roll(a, 8 - d, 0), b)
        vs = out
    return vs


def _rows_from_tiles(tiles):
    n_rows = tiles.shape[0] // ROW_TILES
    groups = [_transpose8([tiles[(g * 8 + r) * ROW_TILES:(g * 8 + r + 1) * ROW_TILES] for r in range(8)])
              for g in range(n_rows // 8)]
    return jnp.concatenate([jnp.concatenate([grp[c] for grp in groups], axis=0) for c in range(ROW_TILES)], axis=1)


def _rows_to_tiles(value):
    n_rows = value.shape[0]
    pieces = []
    for g in range(n_rows // 8):
        pieces += _transpose8([value[g * 8:(g + 1) * 8, c * 128:(c + 1) * 128] for c in range(ROW_TILES)])
    return jnp.concatenate(pieces, axis=0)


def _start_row_gather(src_hbm, idx_of, dst, sem, n_rows, priority_of=lambda r: r % 2):
    for r in range(n_rows):
        src_row = pl.multiple_of(idx_of(r) * ROW_TILES, ROW_TILES)
        pltpu.make_async_copy(src_hbm.at[pl.ds(src_row, ROW_TILES)], dst.at[pl.ds(r * ROW_TILES, ROW_TILES)],
                              sem).start(priority=priority_of(r))


DISPATCH_ROWS = 128
DISPATCH_DEPTH = 4


def _dispatch_body(n_tok, dest_ref, tok_hbm, xs_hbm, tbuf, zbuf, sem, fsem, zsem):
    i = pl.program_id(0)
    k = pl.program_id(1)
    n_blk = pl.num_programs(0)
    step = i * TOP_K + k
    n_steps = n_blk * TOP_K
    blk_rows = DISPATCH_ROWS * ROW_TILES
    buf = i & 1

    def fetch(blk, b):
        return pltpu.make_async_copy(tok_hbm.at[pl.ds(pl.multiple_of(blk * blk_rows, blk_rows), blk_rows)],
                                     tbuf.at[b], fsem.at[b])

    def batch_wait(b):
        pltpu.make_async_copy(tbuf.at[b], xs_hbm.at[pl.ds(0, blk_rows)], sem.at[b]).wait()

    @pl.when(step == 0)
    def _():
        fetch(0, 0).start()
        zbuf[...] = jnp.zeros(zbuf.shape, F32)
        tail = pltpu.make_async_copy(zbuf, xs_hbm.at[pl.ds(xs_hbm.shape[0] - zbuf.shape[0], zbuf.shape[0])], zsem)
        tail.start()
        tail.wait()

    @pl.when(k == 0)
    def _():
        fetch(i, buf).wait()

    @pl.when(jnp.logical_and(i > 0, k < DISPATCH_DEPTH))
    def _():
        for _ in range(TOP_K // DISPATCH_DEPTH):
            batch_wait(1 - buf)

    @pl.when(jnp.logical_and(k == DISPATCH_DEPTH, i + 1 < n_blk))
    def _():
        fetch(i + 1, 1 - buf).start()

    a0 = k * n_tok + i * DISPATCH_ROWS
    src = tbuf.at[buf]
    for r in range(DISPATCH_ROWS):
        dst_row = pl.multiple_of(dest_ref[a0 + r] * ROW_TILES, ROW_TILES)
        pltpu.make_async_copy(src.at[pl.ds(r * ROW_TILES, ROW_TILES)], xs_hbm.at[pl.ds(dst_row, ROW_TILES)],
                              sem.at[buf]).start(priority=r % 2)

    @pl.when(step == n_steps - 1)
    def _():
        for _ in range(TOP_K):
            batch_wait(buf)


def _dispatch(dest_flat, tok_tiles, n_tok, n_rows_out):
    assert DISPATCH_DEPTH < TOP_K and TOP_K % DISPATCH_DEPTH == 0
    return pl.pallas_call(
        functools.partial(_dispatch_body, n_tok),
        out_shape=jax.ShapeDtypeStruct((n_rows_out * ROW_TILES, 128), F32),
        grid_spec=pltpu.PrefetchScalarGridSpec(
            num_scalar_prefetch=1,
            grid=(n_tok // DISPATCH_ROWS, TOP_K),
            in_specs=[pl.BlockSpec(memory_space=pl.ANY)],
            out_specs=pl.BlockSpec(memory_space=pl.ANY),
            scratch_shapes=[pltpu.VMEM((2, DISPATCH_ROWS * ROW_TILES, 128), F32),
                            pltpu.VMEM((SUB * ROW_TILES, 128), F32),
                            pltpu.SemaphoreType.DMA((2,)), pltpu.SemaphoreType.DMA((2,)),
                            pltpu.SemaphoreType.DMA(())]),
        compiler_params=_cparams(2),
        name="dispatch",
    )(dest_flat, tok_tiles)


N_XBUF = 4
LOOKAHEAD = 3


def _experts_body(start_ref, nsub_ref,
                  xs_hbm, wg_hbm, wu_hbm, wd_hbm, y_hbm,
                  xbuf, ybuf, wg_f32, wu_f32, wd_f32, wg_bf, wu_bf, wd_bf, cur, nxt, gsem, osem, wsem):
    e = pl.program_id(0)
    n_e = pl.num_programs(0)
    nsub = nsub_ref[e]
    sub_rows = SUB * ROW_TILES

    def weight_copies(ex, slot):
        return [pltpu.make_async_copy(src.at[ex], dst.at[slot], wsem.at[slot, n])
                for n, (src, dst) in enumerate(((wg_hbm, wg_f32), (wu_hbm, wu_f32), (wd_hbm, wd_f32)))]

    def window(hbm, ex, j):
        return hbm.at[pl.ds(pl.multiple_of((start_ref[ex] + j * SUB) * ROW_TILES, ROW_TILES), sub_rows)]

    def produce():
        pe = cur[0]

        @pl.when(pe < n_e)
        def _():
            pj = cur[1]
            pg = cur[2]
            slot = lax.rem(pg, N_XBUF)
            pltpu.make_async_copy(window(xs_hbm, pe, pj), xbuf.at[slot], gsem.at[slot]).start()
            last = pj + 1 >= nsub_ref[pe]
            cur[0] = jnp.where(last, nxt[pe], pe)
            cur[1] = jnp.where(last, 0, pj + 1)
            cur[2] = pg + 1

    def out_wait():
        pltpu.make_async_copy(ybuf.at[0], y_hbm.at[pl.ds(0, sub_rows)], osem).wait()

    @pl.when(e == 0)
    def _():
        def fill(i, following):
            x = N_EXPERTS - 1 - i
            nxt[x] = following
            return jnp.where(nsub_ref[x] > 0, x, following)

        cur[0] = lax.fori_loop(0, N_EXPERTS, fill, N_EXPERTS)
        cur[1] = 0
        cur[2] = 0
        cur[3] = 0
        for cp in weight_copies(0, 0):
            cp.start()
        for _ in range(LOOKAHEAD):
            produce()

    wslot = e & 1

    @pl.when(e + 1 < n_e)
    def _():
        for cp in weight_copies(e + 1, 1 - wslot):
            cp.start()

    for cp in weight_copies(e, wslot):
        cp.wait()
    wg_bf[...] = wg_f32[wslot].astype(BF16)
    wu_bf[...] = wu_f32[wslot].astype(BF16)
    wd_bf[...] = wd_f32[wslot].astype(BF16)

    def step(j, carry):
        g = cur[3]
        produce()
        slot = lax.rem(g, N_XBUF)
        yslot = g & 1
        pltpu.make_async_copy(xs_hbm.at[pl.ds(0, sub_rows)], xbuf.at[slot], gsem.at[slot]).wait()
        xb = _rows_from_tiles(xbuf[slot]).astype(BF16)
        a = _dot(xb, wg_bf[...])
        b = _dot(xb, wu_bf[...])
        ybuf[yslot] = _rows_to_tiles(_dot((_silu(a) * b).astype(BF16), wd_bf[...]))

        @pl.when(g > 0)
        def _():
            out_wait()

        pltpu.make_async_copy(ybuf.at[yslot], window(y_hbm, e, j), osem).start()
        cur[3] = g + 1
        return carry

    lax.fori_loop(0, nsub, step, 0)

    @pl.when(e == n_e - 1)
    def _():
        @pl.when(cur[3] > 0)
        def _():
            out_wait()

        ybuf[0] = jnp.zeros(ybuf.shape[1:], F32)
        tail = pltpu.make_async_copy(ybuf.at[0], y_hbm.at[pl.ds(y_hbm.shape[0] - sub_rows, sub_rows)], osem)
        tail.start()
        tail.wait()


def _experts(start, nsub, xs_tiles, wg, wu, wd):
    sub_rows = SUB * ROW_TILES
    return pl.pallas_call(
        _experts_body,
        out_shape=jax.ShapeDtypeStruct(xs_tiles.shape, F32),
        grid_spec=pltpu.PrefetchScalarGridSpec(
            num_scalar_prefetch=2,
            grid=(N_EXPERTS,),
            in_specs=[pl.BlockSpec(memory_space=pl.ANY)] * 4,
            out_specs=pl.BlockSpec(memory_space=pl.ANY),
            scratch_shapes=[pltpu.VMEM((N_XBUF, sub_rows, 128), F32), pltpu.VMEM((2, sub_rows, 128), F32),
                            pltpu.VMEM((2, D_MODEL, FF), F32), pltpu.VMEM((2, D_MODEL, FF), F32),
                            pltpu.VMEM((2, FF, D_MODEL), F32),
                            pltpu.VMEM((D_MODEL, FF), BF16), pltpu.VMEM((D_MODEL, FF), BF16),
                            pltpu.VMEM((FF, D_MODEL), BF16), pltpu.SMEM((4,), I32), pltpu.SMEM((N_EXPERTS,), I32),
                            pltpu.SemaphoreType.DMA((N_XBUF,)), pltpu.SemaphoreType.DMA(()),
                            pltpu.SemaphoreType.DMA((2, 3))]),
        compiler_params=_cparams(),
        name="experts",
    )(start, nsub, xs_tiles, wg, wu, wd)


def _combine_body(n_tok, n_ctx_tiles, dest_ref,
                  y_hbm, topw_ref, tok_ref, xnew_ref, mod_ref, sg_ref, su_ref, sd_ref, gpost_ref,
                  outp_ref, outs_ref, gbuf, wcol, fbuf, gsem):
    i = pl.program_id(0)
    n = pl.num_programs(0)

    def gather(tile, slot):
        def per_choice(k, carry):
            base = k * n_tok + tile * TT
            _start_row_gather(y_hbm, lambda r: dest_ref[base + r], gbuf.at[slot, k], gsem.at[slot], TT)
            return carry
        lax.fori_loop(0, TOP_K, per_choice, 0)

    def gather_wait(slot):
        for k in range(TOP_K):
            pltpu.make_async_copy(y_hbm.at[pl.ds(0, TT * ROW_TILES)], gbuf.at[slot, k], gsem.at[slot]).wait()

    slot = i & 1

    @pl.when(i == 0)
    def _():
        gather(0, 0)

    nxt_tile = jnp.minimum(i + 1, n - 1)
    for k in range(TOP_K):
        base = k * n_tok + nxt_tile * TT
        _start_row_gather(y_hbm, lambda r: dest_ref[base + r], gbuf.at[1 - slot, k], gsem.at[1 - slot], TT)

    w_t = jnp.concatenate([topw_ref[...], jnp.zeros((128 - TOP_K, TT), F32)], axis=0).T
    for k in range(TOP_K):
        wcol[k] = jnp.broadcast_to(w_t[:, k:k + 1], (TT, 128))
    h = tok_ref[...].astype(BF16)
    fbuf[...] = _dot((_silu(_dot(h, sg_ref[...])) * _dot(h, su_ref[...])).astype(BF16), sd_ref[...])

    gather_wait(slot)
    for g in range(TT // 8):
        rows = slice(g * 8, (g + 1) * 8)
        tiles = []
        for r in range(8):
            t = g * 8 + r
            acc = None
            for k in range(TOP_K):
                w = jnp.broadcast_to(wcol[k, t:t + 1, :], (8, 128))
                term = gbuf[slot, k, t * ROW_TILES:(t + 1) * ROW_TILES, :] * w
                acc = term if acc is None else acc + term
            tiles.append(acc)
        f = fbuf[rows, :] + jnp.concatenate(_transpose8(tiles), axis=1)
        fbuf[rows, :] = xnew_ref[rows, :] + mod_ref[0, 5:6, :] * _rms(f, gpost_ref[...])

    @pl.when(i == n - 1)
    def _():
        gather_wait(1 - slot)

    @pl.when(i < n_ctx_tiles)
    def _():
        outp_ref[...] = fbuf[...]

    @pl.when(i >= n_ctx_tiles)
    def _():
        outs_ref[...] = fbuf[...]


def _combine(tiles, dest_flat, ysorted, topw, tok, xnew, mod3, sg_bf, su_bf, sd_bf, g_post):
    n_tok = tok.shape[0]
    n_ctx_tok = tiles.n_ctx * TM
    n_ctx_tiles = n_ctx_tok // TT
    lat_tiles_per_seq = tiles.lat_tiles * TM // TT

    def mod_row(i):
        return jnp.where(i < n_ctx_tiles, 0, 1 + jnp.maximum(i - n_ctx_tiles, 0) // lat_tiles_per_seq)

    def full(shape):
        return pl.BlockSpec(shape, lambda i, *_: (0,) * len(shape))

    rows = pl.BlockSpec((TT, D_MODEL), lambda i, *_: (i, 0))
    return pl.pallas_call(
        functools.partial(_combine_body, n_tok, n_ctx_tiles),
        out_shape=(jax.ShapeDtypeStruct((n_ctx_tok, D_MODEL), F32),
                   jax.ShapeDtypeStruct((n_tok - n_ctx_tok, D_MODEL), F32)),
        grid_spec=pltpu.PrefetchScalarGridSpec(
            num_scalar_prefetch=1,
            grid=(n_tok // TT,),
            in_specs=[pl.BlockSpec(memory_space=pl.ANY),
                      pl.BlockSpec((TOP_K, TT), lambda i, *_: (0, i)),
                      rows, rows,
                      pl.BlockSpec((1, 6, D_MODEL), lambda i, *_: (mod_row(i), 0, 0)),
                      full((D_MODEL, FF)), full((D_MODEL, FF)), full((FF, D_MODEL)), full((1, D_MODEL))],
            out_specs=(pl.BlockSpec((TT, D_MODEL), lambda i, *_: (jnp.minimum(i, n_ctx_tiles - 1), 0)),
                       pl.BlockSpec((TT, D_MODEL), lambda i, *_: (jnp.maximum(i - n_ctx_tiles, 0), 0))),
            scratch_shapes=[pltpu.VMEM((2, TOP_K, TT * ROW_TILES, 128), F32), pltpu.VMEM((TOP_K, TT, 128), F32),
                            pltpu.VMEM((TT, D_MODEL), F32), pltpu.SemaphoreType.DMA((2,))]),
        compiler_params=_cparams(),
        name="combine",
    )(dest_flat, ysorted, topw, tok, xnew, mod3, sg_bf, su_bf, sd_bf, g_post)


def _rope_tables(lat_len, lat_tiles):
    rows = lat_len // GRID_W
    row = jnp.repeat(jnp.arange(rows, dtype=F32), GRID_W)
    col = jnp.tile(jnp.arange(GRID_W, dtype=F32), rows)
    inv = ROPE_BASE ** (-jnp.arange(ROPE_FREQS, dtype=F32) / ROPE_FREQS)
    ang = jnp.concatenate([row[:, None] * inv[None, :]] * 2 + [col[:, None] * inv[None, :]] * 2, axis=1)
    sign = jnp.tile(jnp.concatenate([-jnp.ones((ROPE_FREQS,), F32), jnp.ones((ROPE_FREQS,), F32)]), 2)
    cos = jnp.cos(ang).reshape(lat_tiles, TM, DK)
    sin = (jnp.sin(ang) * sign[None, :]).reshape(lat_tiles, TM, DK)
    cos = jnp.concatenate([jnp.ones((1, TM, DK), F32), cos], axis=0)
    sin = jnp.concatenate([jnp.zeros((1, TM, DK), F32), sin], axis=0)
    return cos, sin


def kernel(x_prompt, x_sample, state_ret_fwd, state_ret_bwd, c, c_ctx, w_mod, b_mod, norm_mix_pre,
           norm_mix_post, norm_ffn_pre, norm_ffn_post, w_in, conv_w, conv_b, ret_decay_fwd,
           ret_decay_bwd, ret_gn_g, w_out, router_w, router_bias, expert_w_gate, expert_w_up,
           expert_w_down, shared_w_gate, shared_w_up, shared_w_down):
    bp, tp, d = x_prompt.shape
    bs, ts, _ = x_sample.shape
    depth = w_mod.shape[0]
    assert d == D_MODEL and tp == TM and ts % TM == 0 and bs + 1 <= 8
    tiles = _Tiles(bp, bs, ts)
    n_tok = tiles.n_tiles * TM
    n_slots = n_tok * TOP_K + SUB
    cos_t, sin_t = _rope_tables(ts, tiles.lat_tiles)

    perm = (jnp.arange(N_EXPERTS) % N_GROUPS) * GROUP_SIZE + jnp.arange(N_EXPERTS) // N_GROUPS

    xp2 = x_prompt.reshape(bp * tp, d)
    xs2 = x_sample.reshape(bs * ts, d)
    new_f, new_b = [], []
    for l in range(depth):
        c_rows = jnp.concatenate([c_ctx[None, :], c, jnp.zeros((8 - 1 - bs, d), F32)], axis=0)
        mod3 = _modulation(c_rows, w_mod[l], b_mod[l][None, :]).reshape(8, 6, d)
        dec = jnp.broadcast_to(jnp.stack([ret_decay_fwd[l], ret_decay_bwd[l]])[:, :, None, None],
                               (2, HEADS, DK, DK)).astype(F32)
        xnew, tok, sf_fin, sb_fin = _token_mixer(
            tiles, xp2, xs2, mod3, norm_mix_pre[l][None, :], w_in[l].astype(BF16), conv_w[l],
            conv_b[l][None, :], dec, cos_t, sin_t, state_ret_fwd[:, l], state_ret_bwd[:, l],
            w_out[l].astype(BF16), norm_mix_post[l][None, :], norm_ffn_pre[l][None, :], ret_gn_g[l][None, :])
        new_f.append(sf_fin)
        new_b.append(sb_fin)

        rwt = router_w[l].T[perm].astype(BF16)
        bias_b = jnp.broadcast_to(router_bias[l][perm][:, None], (N_EXPERTS, 128)).astype(F32)
        topi, topw = _route(tok, rwt, bias_b)
        dest, start, counts = _dispatch_plan(topi)
        dest_flat = dest.reshape(-1)
        start_i = start[:, 0].astype(I32)
        nsub = ((counts[:, 0] + float(SUB - 1)) / float(SUB)).astype(I32)
        tok_tiles = tok.reshape(n_tok * ROW_TILES, 128)
        xs_tiles = _dispatch(dest_flat, tok_tiles, n_tok, n_slots)
        ysorted = _experts(start_i, nsub, xs_tiles, expert_w_gate[l], expert_w_up[l], expert_w_down[l])
        xp2, xs2 = _combine(tiles, dest_flat, ysorted, topw, tok, xnew, mod3,
                            shared_w_gate[l].astype(BF16), shared_w_up[l].astype(BF16),
                            shared_w_down[l].astype(BF16), norm_ffn_post[l][None, :])

    return (xp2.reshape(bp, tp, d), xs2.reshape(bs, ts, d),
            jnp.stack(new_f, axis=1), jnp.stack(new_b, axis=1))
```

```python
import functools

import jax
import jax.numpy as jnp
from jax import lax
from jax.experimental import pallas as pl
from jax.experimental.pallas import tpu as pltpu

F32 = jnp.float32
BF16 = jnp.bfloat16
I32 = jnp.int32

D_MODEL = 1024
CONV_W = 512
RET_W = 512
HEADS = 4
DK = 128
CHUNK = 128
GRID_W = 64
ROPE_FREQS = 32
ROPE_BASE = 10000.0
IN_COLS = 3 * CONV_W + 4 * RET_W
N_EXPERTS = 256
N_GROUPS = 8
GROUP_SIZE = N_EXPERTS // N_GROUPS
TOPK_GROUPS = 4
TOP_K = 8
FF = 256
ROUTED_SCALE = 2.5
EPS = 1e-6

TM = 256
SUB = 128
TT = 128
RT = 512
RB = 2048
VMEM_LIMIT = 56 * 1024 * 1024


def _cparams(n_axes=1, vmem=VMEM_LIMIT):
    return pltpu.CompilerParams(dimension_semantics=("arbitrary",) * n_axes,
                                vmem_limit_bytes=vmem)


def _silu(x):
    return x * jax.nn.sigmoid(x)


def _log_sigmoid(x):
    return jnp.minimum(x, 0.0) - jnp.log1p(jnp.exp(-jnp.abs(x)))


def _rms(x, g):
    return x * lax.rsqrt(jnp.mean(x * x, axis=-1, keepdims=True) + EPS) * g


def _dot(a, b):
    return jnp.dot(a, b, preferred_element_type=F32)


def _mod_body(c_ref, w_ref, b_ref, o_ref):
    s = _silu(c_ref[...]).astype(BF16)
    o_ref[...] = _dot(s, w_ref[...].astype(BF16)) + b_ref[...]


def _modulation(c_rows, w_mod, b_mod):
    n_col = w_mod.shape[1]
    blk = 1536
    return pl.pallas_call(
        _mod_body,
        out_shape=jax.ShapeDtypeStruct((8, n_col), F32),
        grid=(n_col // blk,),
        in_specs=[pl.BlockSpec((8, D_MODEL), lambda i: (0, 0)),
                  pl.BlockSpec((D_MODEL, blk), lambda i: (0, i)),
                  pl.BlockSpec((1, blk), lambda i: (0, i))],
        out_specs=pl.BlockSpec((8, blk), lambda i: (0, i)),
        compiler_params=_cparams(),
        name="mod",
    )(c_rows, w_mod, b_mod)


class _Tiles:
    def __init__(self, n_ctx_seq, n_lat_seq, lat_len):
        self.n_ctx = n_ctx_seq
        self.lat_tiles = lat_len // TM
        self.n_lat_seq = n_lat_seq
        self.n_tiles = n_ctx_seq + n_lat_seq * self.lat_tiles

    def is_ctx(self, i):
        return i < self.n_ctx

    def lat_pos(self, i):
        j = jnp.maximum(i - self.n_ctx, 0)
        return j // self.lat_tiles, j % self.lat_tiles

    def phys_reversed(self, i):
        b, t = self.lat_pos(i)
        return jnp.where(i < self.n_ctx, i, self.n_ctx + b * self.lat_tiles + (self.lat_tiles - 1 - t))

    def mod_row(self, i):
        b, _ = self.lat_pos(i)
        return jnp.where(i < self.n_ctx, 0, 1 + b)


def _rope(x, cos, sin_signed):
    lane = lax.broadcasted_iota(I32, x.shape, 1)
    partner = jnp.where((lane & 63) < 32, pltpu.roll(x, 96, 1), pltpu.roll(x, 32, 1))
    return x * cos + partner * sin_signed


def _mix_a_body(tiles, xp_ref, xs_ref, mod_ref, gpre_ref, win_ref, cw_ref, cb_ref, dec_ref,
                cos_ref, sin_ref, s0b_ref,
                yconv_ref, q_ref, v_ref, g_ref, kt_ref, sbin_ref, sbfin_ref,
                sb_scr, tab_scr):
    i = pl.program_id(0)
    is_ctx = tiles.is_ctx(i)
    _, t_rev = tiles.lat_pos(i)
    first = jnp.logical_or(is_ctx, t_rev == 0)

    @pl.when(i == 0)
    def _():
        lg = _log_sigmoid(dec_ref[1])
        col = lax.broadcasted_iota(I32, lg.shape, 2).astype(F32)
        tab_scr[0] = jnp.exp(col * lg)
        tab_scr[1] = jnp.exp(float(CHUNK) * lg)

    @pl.when(first)
    def _():
        sb_scr[...] = jnp.where(is_ctx, 0.0, s0b_ref[0])

    x = jnp.where(is_ctx, xp_ref[...], xs_ref[...])
    h = (_rms(x, gpre_ref[...]) * (1.0 + mod_ref[0, 1:2, :]) + mod_ref[0, 0:1, :]).astype(BF16)

    def proj(k):
        return _dot(h, win_ref[:, k * 512:(k + 1) * 512])

    z = proj(1) * proj(2)
    row = lax.broadcasted_iota(I32, z.shape, 0)
    period = jnp.where(is_ctx, TM, GRID_W)
    pos = row & (period - 1)
    left = jnp.where(pos == 0, 0.0, pltpu.roll(z, 1, 0))
    right = jnp.where(pos == period - 1, 0.0, pltpu.roll(z, TM - 1, 0))
    zc = left * cw_ref[0:1, :] + z * cw_ref[1:2, :] + right * cw_ref[2:3, :] + cb_ref[...]
    yconv_ref[...] = (proj(0) * zc).astype(BF16)

    cos = cos_ref[0]
    sin = sin_ref[0]
    q = proj(3)
    k = proj(4)
    q = jnp.concatenate([_rope(q[:, hh * DK:(hh + 1) * DK], cos, sin) for hh in range(HEADS)], axis=1)
    k = jnp.concatenate([_rope(k[:, hh * DK:(hh + 1) * DK], cos, sin) for hh in range(HEADS)], axis=1)
    q_ref[...] = (q * (DK ** -0.5)).astype(BF16)
    kt = k.T
    kt_ref[...] = kt.astype(BF16)
    v = proj(5).astype(BF16)
    v_ref[...] = v
    g_ref[...] = proj(6)

    for c in (1, 0):
        for hh in range(HEADS):
            sbin_ref[c, hh] = sb_scr[hh].astype(BF16)
            kts = (kt[hh * DK:(hh + 1) * DK, c * CHUNK:(c + 1) * CHUNK] * tab_scr[0, hh]).astype(BF16)
            vc = v[c * CHUNK:(c + 1) * CHUNK, hh * DK:(hh + 1) * DK]
            sb_scr[hh] = sb_scr[hh] * tab_scr[1, hh] + _dot(kts, vc)

    @pl.when(is_ctx)
    def _():
        sbfin_ref[0] = sb_scr[...]


def _mix_b_body(tiles, xp_ref, xs_ref, mod_ref, q_ref, kt_ref, v_ref, g_ref, yconv_ref, sbin_ref,
                wout_ref, gpost_ref, gffn_ref, gn_ref, dec_ref, s0f_ref,
                xnew_ref, tok_ref, toktiles_ref, sffin_ref,
                sf_scr, tab_scr, ycat_scr):
    i = pl.program_id(0)
    is_ctx = tiles.is_ctx(i)
    _, t_pos = tiles.lat_pos(i)
    first = jnp.logical_or(is_ctx, t_pos == 0)

    @pl.when(i == 0)
    def _():
        lgf = _log_sigmoid(dec_ref[0])
        lgb = _log_sigmoid(dec_ref[1])
        row = lax.broadcasted_iota(I32, lgf.shape, 1)
        col = lax.broadcasted_iota(I32, lgf.shape, 2)
        d = (row - col).astype(F32)
        tab_scr[0] = (jnp.where(row >= col, jnp.exp(jnp.where(row >= col, d, 0.0) * lgf), 0.0)
                      + jnp.where(col >= row, jnp.exp(jnp.where(col >= row, -d, 0.0) * lgb), 0.0))
        tab_scr[1] = jnp.exp((row + 1).astype(F32) * lgf)
        tab_scr[2] = jnp.exp((CHUNK - row).astype(F32) * lgb)
        tab_scr[3] = jnp.exp((CHUNK - 1 - col).astype(F32) * lgf)
        tab_scr[4] = jnp.exp(float(CHUNK) * lgf)

    @pl.when(first)
    def _():
        sf_scr[...] = jnp.where(is_ctx, 0.0, s0f_ref[0])

    for c in range(TM // CHUNK):
        rows = slice(c * CHUNK, (c + 1) * CHUNK)
        for hh in range(HEADS):
            cols = slice(hh * DK, (hh + 1) * DK)
            qc = q_ref[rows, cols]
            ktc = kt_ref[cols, rows]
            vc = v_ref[rows, cols]
            att = (_dot(qc, ktc) * tab_scr[0, hh]).astype(BF16)
            o = (_dot(att, vc)
                 + tab_scr[1, hh] * _dot(qc, sf_scr[hh].astype(BF16))
                 + tab_scr[2, hh] * _dot(qc, sbin_ref[c, hh]))
            kts = (ktc.astype(F32) * tab_scr[3, hh]).astype(BF16)
            sf_scr[hh] = sf_scr[hh] * tab_scr[4, hh] + _dot(kts, vc)
            mu = jnp.mean(o, axis=-1, keepdims=True)
            dev = o - mu
            var = jnp.mean(dev * dev, axis=-1, keepdims=True)
            on = dev * lax.rsqrt(var + EPS) * gn_ref[:, cols]
            ycat_scr[rows, RET_W + hh * DK:RET_W + (hh + 1) * DK] = (_silu(g_ref[rows, cols]) * on).astype(BF16)
    ycat_scr[:, 0:CONV_W] = yconv_ref[...]

    @pl.when(is_ctx)
    def _():
        sffin_ref[0] = sf_scr[...]

    x = jnp.where(is_ctx, xp_ref[...], xs_ref[...])
    u = _dot(ycat_scr[...], wout_ref[...])
    xn = x + mod_ref[0, 2:3, :] * _rms(u, gpost_ref[...])
    xnew_ref[...] = xn
    tok = _rms(xn, gffn_ref[...]) * (1.0 + mod_ref[0, 4:5, :]) + mod_ref[0, 3:4, :]
    tok_ref[...] = tok.astype(BF16)
    toktiles_ref[...] = _rows_to_tiles(tok)


def _token_mixer(tiles, xp2, xs2, mod3, g_pre, win_bf, conv_w, conv_b, dec, cos_t, sin_t,
                 s0f, s0b, wout_bf, g_post, g_ffn, gn_g):
    n_tok = tiles.n_tiles * TM
    n_ctx = tiles.n_ctx
    last_ctx = n_ctx - 1

    def full(shape):
        return pl.BlockSpec(shape, lambda i: (0,) * len(shape))

    def xp_spec(phys):
        return pl.BlockSpec((TM, D_MODEL), lambda i: (jnp.minimum(phys(i), last_ctx), 0))

    def xs_spec(phys):
        return pl.BlockSpec((TM, D_MODEL), lambda i: (jnp.maximum(phys(i) - n_ctx, 0), 0))

    mod_spec = pl.BlockSpec((1, 6, D_MODEL), lambda i: (tiles.mod_row(i), 0, 0))
    state_in = pl.BlockSpec((1, HEADS, DK, DK), lambda i: (tiles.lat_pos(i)[0], 0, 0, 0))
    state_out = pl.BlockSpec((1, HEADS, DK, DK), lambda i: (jnp.minimum(i, last_ctx), 0, 0, 0))

    rev = tiles.phys_reversed

    def rope_idx(i):
        _, t = tiles.lat_pos(i)
        return jnp.where(i < n_ctx, 0, 1 + (tiles.lat_tiles - 1 - t))

    rope_spec = pl.BlockSpec((1, TM, DK), lambda i: (rope_idx(i), 0, 0))

    def rows(width, phys):
        return pl.BlockSpec((TM, width), lambda i: (phys(i), 0))

    yconv, q, v, g, kt, sbin, sb_fin = pl.pallas_call(
        functools.partial(_mix_a_body, tiles),
        out_shape=(jax.ShapeDtypeStruct((n_tok, CONV_W), BF16),
                   jax.ShapeDtypeStruct((n_tok, RET_W), BF16),
                   jax.ShapeDtypeStruct((n_tok, RET_W), BF16),
                   jax.ShapeDtypeStruct((n_tok, RET_W), F32),
                   jax.ShapeDtypeStruct((RET_W, n_tok), BF16),
                   jax.ShapeDtypeStruct((n_tok // CHUNK, HEADS, DK, DK), BF16),
                   jax.ShapeDtypeStruct((n_ctx, HEADS, DK, DK), F32)),
        grid=(tiles.n_tiles,),
        in_specs=[xp_spec(rev), xs_spec(rev), mod_spec, full((1, D_MODEL)), full((D_MODEL, IN_COLS)),
                  full((3, CONV_W)), full((1, CONV_W)), full((2, HEADS, DK, DK)),
                  rope_spec, rope_spec, state_in],
        out_specs=(rows(CONV_W, rev), rows(RET_W, rev), rows(RET_W, rev), rows(RET_W, rev),
                   pl.BlockSpec((RET_W, TM), lambda i: (0, rev(i))),
                   pl.BlockSpec((TM // CHUNK, HEADS, DK, DK), lambda i: (rev(i), 0, 0, 0)),
                   state_out),
        scratch_shapes=[pltpu.VMEM((HEADS, DK, DK), F32), pltpu.VMEM((2, HEADS, DK, DK), F32)],
        compiler_params=_cparams(),
        name="mix_a",
    )(xp2, xs2, mod3, g_pre, win_bf, conv_w, conv_b, dec, cos_t, sin_t, s0b)

    ident = lambda i: i
    xnew, tok, tok_tiles, sf_fin = pl.pallas_call(
        functools.partial(_mix_b_body, tiles),
        out_shape=(jax.ShapeDtypeStruct((n_tok, D_MODEL), F32),
                   jax.ShapeDtypeStruct((n_tok, D_MODEL), BF16),
                   jax.ShapeDtypeStruct((n_tok * ROW_TILES, 128), F32),
                   jax.ShapeDtypeStruct((n_ctx, HEADS, DK, DK), F32)),
        grid=(tiles.n_tiles,),
        in_specs=[xp_spec(ident), xs_spec(ident), mod_spec,
                  rows(RET_W, ident),
                  pl.BlockSpec((RET_W, TM), lambda i: (0, i)),
                  rows(RET_W, ident), rows(RET_W, ident), rows(CONV_W, ident),
                  pl.BlockSpec((TM // CHUNK, HEADS, DK, DK), lambda i: (i, 0, 0, 0)),
                  full((D_MODEL, D_MODEL)), full((1, D_MODEL)), full((1, D_MODEL)), full((1, RET_W)),
                  full((2, HEADS, DK, DK)), state_in],
        out_specs=(rows(D_MODEL, ident), rows(D_MODEL, ident),
                   pl.BlockSpec((TM * ROW_TILES, 128), lambda i: (i, 0)), state_out),
        scratch_shapes=[pltpu.VMEM((HEADS, DK, DK), F32), pltpu.VMEM((5, HEADS, DK, DK), F32),
                        pltpu.VMEM((TM, D_MODEL), BF16)],
        compiler_params=_cparams(),
        name="mix_b",
    )(xp2, xs2, mod3, q, kt, v, g, yconv, sbin, wout_bf, g_post, g_ffn, gn_g, dec, s0f)
    return xnew, tok, tok_tiles, sf_fin, sb_fin


def _route_body(tok_ref, rwt_ref, bias_ref, topi_ref, topw_ref):
    h = tok_ref[...]
    logits = lax.dot_general(rwt_ref[...], h, (((1,), (1,)), ((), ())), preferred_element_type=F32)
    shape3 = (GROUP_SIZE, N_GROUPS, 128)
    member = lax.broadcasted_iota(I32, shape3, 0)
    group = lax.broadcasted_iota(I32, shape3, 1)
    expert = group * GROUP_SIZE + member
    group2 = lax.broadcasted_iota(I32, (N_GROUPS, 128), 0)
    neg = -jnp.inf
    for lb in range(RT // 128):
        scores = jax.nn.sigmoid(logits[:, lb * 128:(lb + 1) * 128]).reshape(shape3)
        biased = scores + bias_ref[...].reshape(shape3)
        m1 = jnp.max(biased, axis=0)
        first = jnp.min(jnp.where(biased == m1, member, GROUP_SIZE), axis=0)
        m2 = jnp.max(jnp.where(member == first, neg, biased), axis=0)
        gs = m1 + m2
        beaten = jnp.zeros(gs.shape, I32)
        for s in range(1, N_GROUPS):
            other = pltpu.roll(gs, s, 0)
            wins = (other > gs) | ((other == gs) & (group2 >= s))
            beaten = beaten + wins.astype(I32)
        keep = beaten < TOPK_GROUPS
        cand = jnp.where(keep, biased, neg)
        idx_rows, w_rows = [], []
        for _ in range(TOP_K):
            best = jnp.max(jnp.max(cand, axis=0), axis=0, keepdims=True)
            pick = jnp.min(jnp.min(jnp.where(cand == best, expert, N_EXPERTS), axis=0), axis=0, keepdims=True)
            hit = expert == pick
            w_rows.append(jnp.sum(jnp.sum(jnp.where(hit, scores, 0.0), axis=0), axis=0, keepdims=True))
            idx_rows.append(pick)
            cand = jnp.where(hit, neg, cand)
        w = jnp.concatenate(w_rows, axis=0)
        topi_ref[:, lb * 128:(lb + 1) * 128] = jnp.concatenate(idx_rows, axis=0)
        topw_ref[:, lb * 128:(lb + 1) * 128] = w / jnp.sum(w, axis=0, keepdims=True) * ROUTED_SCALE


def _route(tok, rwt_bf, bias_b):
    n_tok = tok.shape[0]
    return pl.pallas_call(
        _route_body,
        out_shape=(jax.ShapeDtypeStruct((TOP_K, n_tok), I32), jax.ShapeDtypeStruct((TOP_K, n_tok), F32)),
        grid=(n_tok // RT,),
        in_specs=[pl.BlockSpec((RT, D_MODEL), lambda i: (i, 0)),
                  pl.BlockSpec((N_EXPERTS, D_MODEL), lambda i: (0, 0)),
                  pl.BlockSpec((N_EXPERTS, 128), lambda i: (0, 0))],
        out_specs=(pl.BlockSpec((TOP_K, RT), lambda i: (0, i)), pl.BlockSpec((TOP_K, RT), lambda i: (0, i))),
        compiler_params=_cparams(),
        name="route",
    )(tok, rwt_bf, bias_b)


def _onehot(ids_row):
    e = lax.broadcasted_iota(I32, (N_EXPERTS, 256), 0)
    return e == ids_row


def _rank_body(topi_ref, rank_ref, counts_ref, run_scr):
    i = pl.program_id(0)

    @pl.when(i == 0)
    def _():
        run_scr[...] = jnp.zeros(run_scr.shape, F32)

    a0 = lax.broadcasted_iota(I32, (256, 256), 0)
    a1 = lax.broadcasted_iota(I32, (256, 256), 1)
    upper = (a0 <= a1).astype(BF16)
    ones = jnp.ones((256, 256), BF16)
    for k in range(TOP_K):
        for sb in range(RB // 256):
            lanes = slice(sb * 256, (sb + 1) * 256)
            oh = _onehot(topi_ref[k:k + 1, lanes])
            ohb = oh.astype(BF16)
            seen = _dot(ohb, upper) + run_scr[...]
            r = jnp.sum(jnp.where(oh, seen, 0.0), axis=0, keepdims=True) - 1.0
            rank_ref[k:k + 1, lanes] = r.astype(I32)
            run_scr[...] = run_scr[...] + _dot(ohb, ones)

    @pl.when(i == pl.num_programs(0) - 1)
    def _():
        counts_ref[...] = run_scr[:, 0:128]


def _dest_body(topi_ref, rank_ref, counts_ref, dest_ref, start_ref, start_scr):
    i = pl.program_id(0)

    @pl.when(i == 0)
    def _():
        c = counts_ref[...]
        d2 = jnp.floor(c / 16384.0)
        rem = c - d2 * 16384.0
        d1 = jnp.floor(rem / 128.0)
        d0 = rem - d1 * 128.0
        e0 = lax.broadcasted_iota(I32, (N_EXPERTS, N_EXPERTS), 0)
        e1 = lax.broadcasted_iota(I32, (N_EXPERTS, N_EXPERTS), 1)
        below = (e1 < e0).astype(BF16)
        start_scr[...] = (16384.0 * _dot(below, d2.astype(BF16)) + 128.0 * _dot(below, d1.astype(BF16))
                          + _dot(below, d0.astype(BF16)))
        start_ref[...] = start_scr[...]

    start = jnp.concatenate([start_scr[...], start_scr[...]], axis=1)
    for k in range(TOP_K):
        for sb in range(RB // 256):
            lanes = slice(sb * 256, (sb + 1) * 256)
            oh = _onehot(topi_ref[k:k + 1, lanes])
            base = jnp.sum(jnp.where(oh, start, 0.0), axis=0, keepdims=True)
            dest_ref[k:k + 1, lanes] = base.astype(I32) + rank_ref[k:k + 1, lanes]


def _dispatch_plan(topi):
    n_tok = topi.shape[1]
    blk = pl.BlockSpec((TOP_K, RB), lambda i: (0, i))
    whole = pl.BlockSpec((N_EXPERTS, 128), lambda i: (0, 0))
    rank, counts = pl.pallas_call(
        _rank_body,
        out_shape=(jax.ShapeDtypeStruct((TOP_K, n_tok), I32), jax.ShapeDtypeStruct((N_EXPERTS, 128), F32)),
        grid=(n_tok // RB,),
        in_specs=[blk],
        out_specs=(blk, whole),
        scratch_shapes=[pltpu.VMEM((N_EXPERTS, 256), F32)],
        compiler_params=_cparams(),
        name="rank",
    )(topi)
    dest, start = pl.pallas_call(
        _dest_body,
        out_shape=(jax.ShapeDtypeStruct((TOP_K, n_tok), I32), jax.ShapeDtypeStruct((N_EXPERTS, 128), F32)),
        grid=(n_tok // RB,),
        in_specs=[blk, blk, whole],
        out_specs=(blk, whole),
        scratch_shapes=[pltpu.VMEM((N_EXPERTS, 128), F32)],
        compiler_params=_cparams(),
        name="dest",
    )(topi, rank, counts)
    return dest, start, counts


ROW_TILES = D_MODEL // 128


def _transpose8(vs):
    sub = lax.broadcasted_iota(I32, (8, 128), 0)
    for d in (4, 2, 1):
        keep = (sub & d) == 0
        out = list(vs)
        for i in range(8):
            if i & d == 0:
                a, b = vs[i], vs[i + d]
                out[i] = jnp.where(keep, a, pltpu.roll(b, d, 0))
                out[i + d] = jnp.where(keep, pltpu.roll(a, 8 - d, 0), b)
        vs = out
    return vs


def _rows_from_tiles(tiles):
    n_rows = tiles.shape[0] // ROW_TILES
    groups = [_transpose8([tiles[(g * 8 + r) * ROW_TILES:(g * 8 + r + 1) * ROW_TILES] for r in range(8)])
              for g in range(n_rows // 8)]
    return jnp.concatenate([jnp.concatenate([grp[c] for grp in groups], axis=0) for c in range(ROW_TILES)], axis=1)


def _rows_to_tiles(value):
    n_rows = value.shape[0]
    pieces = []
    for g in range(n_rows // 8):
        pieces += _transpose8([value[g * 8:(g + 1) * 8, c * 128:(c + 1) * 128] for c in range(ROW_TILES)])
    return jnp.concatenate(pieces, axis=0)


def _start_row_gather(src_hbm, idx_of, dst, sem, n_rows, priority_of=lambda r: r % 2):
    for r in range(n_rows):
        src_row = pl.multiple_of(idx_of(r) * ROW_TILES, ROW_TILES)
        pltpu.make_async_copy(src_hbm.at[pl.ds(src_row, ROW_TILES)], dst.at[pl.ds(r * ROW_TILES, ROW_TILES)],
                              sem).start(priority=priority_of(r))


DISPATCH_ROWS = 128
DISPATCH_DEPTH = 4


def _dispatch_body(n_tok, dest_ref, tok_hbm, xs_hbm, tbuf, zbuf, sem, fsem, zsem):
    i = pl.program_id(0)
    k = pl.program_id(1)
    n_blk = pl.num_programs(0)
    step = i * TOP_K + k
    n_steps = n_blk * TOP_K
    blk_rows = DISPATCH_ROWS * ROW_TILES
    buf = i & 1

    def fetch(blk, b):
        return pltpu.make_async_copy(tok_hbm.at[pl.ds(pl.multiple_of(blk * blk_rows, blk_rows), blk_rows)],
                                     tbuf.at[b], fsem.at[b])

    def batch_wait(b):
        pltpu.make_async_copy(tbuf.at[b], xs_hbm.at[pl.ds(0, blk_rows)], sem.at[b]).wait()

    @pl.when(step == 0)
    def _():
        fetch(0, 0).start()
        zbuf[...] = jnp.zeros(zbuf.shape, F32)
        tail = pltpu.make_async_copy(zbuf, xs_hbm.at[pl.ds(xs_hbm.shape[0] - zbuf.shape[0], zbuf.shape[0])], zsem)
        tail.start()
        tail.wait()

    @pl.when(k == 0)
    def _():
        fetch(i, buf).wait()

    @pl.when(jnp.logical_and(i > 0, k < DISPATCH_DEPTH))
    def _():
        for _ in range(TOP_K // DISPATCH_DEPTH):
            batch_wait(1 - buf)

    @pl.when(jnp.logical_and(k == DISPATCH_DEPTH, i + 1 < n_blk))
    def _():
        fetch(i + 1, 1 - buf).start()

    a0 = k * n_tok + i * DISPATCH_ROWS

    def scatter(b):
        for r in range(DISPATCH_ROWS):
            dst_row = pl.multiple_of(dest_ref[a0 + r] * ROW_TILES, ROW_TILES)
            pltpu.make_async_copy(tbuf.at[b, pl.ds(r * ROW_TILES, ROW_TILES)], xs_hbm.at[pl.ds(dst_row, ROW_TILES)],
                                  sem.at[b]).start(priority=r % 2)

    for b in range(2):
        pl.when(buf == b)(functools.partial(scatter, b))

    @pl.when(step == n_steps - 1)
    def _():
        for _ in range(TOP_K):
            batch_wait(buf)


def _dispatch(dest_flat, tok_tiles, n_tok, n_rows_out):
    assert DISPATCH_DEPTH < TOP_K and TOP_K % DISPATCH_DEPTH == 0
    return pl.pallas_call(
        functools.partial(_dispatch_body, n_tok),
        out_shape=jax.ShapeDtypeStruct((n_rows_out * ROW_TILES, 128), F32),
        grid_spec=pltpu.PrefetchScalarGridSpec(
            num_scalar_prefetch=1,
            grid=(n_tok // DISPATCH_ROWS, TOP_K),
            in_specs=[pl.BlockSpec(memory_space=pl.ANY)],
            out_specs=pl.BlockSpec(memory_space=pl.ANY),
            scratch_shapes=[pltpu.VMEM((2, DISPATCH_ROWS * ROW_TILES, 128), F32),
                            pltpu.VMEM((SUB * ROW_TILES, 128), F32),
                            pltpu.SemaphoreType.DMA((2,)), pltpu.SemaphoreType.DMA((2,)),
                            pltpu.SemaphoreType.DMA(())]),
        compiler_params=_cparams(2),
        name="dispatch",
    )(dest_flat, tok_tiles)


N_XBUF = 4
LOOKAHEAD = 3


def _experts_body(start_ref, nsub_ref,
                  xs_hbm, wg_hbm, wu_hbm, wd_hbm, y_hbm,
                  xbuf, ybuf, wg_f32, wu_f32, wd_f32, wg_bf, wu_bf, wd_bf, cur, nxt, gsem, osem, wsem):
    e = pl.program_id(0)
    n_e = pl.num_programs(0)
    nsub = nsub_ref[e]
    sub_rows = SUB * ROW_TILES

    def weight_copies(ex, slot):
        return [pltpu.make_async_copy(src.at[ex], dst.at[slot], wsem.at[slot, n])
                for n, (src, dst) in enumerate(((wg_hbm, wg_f32), (wu_hbm, wu_f32), (wd_hbm, wd_f32)))]

    def window(hbm, ex, j):
        return hbm.at[pl.ds(pl.multiple_of((start_ref[ex] + j * SUB) * ROW_TILES, ROW_TILES), sub_rows)]

    def produce():
        pe = cur[0]

        @pl.when(pe < n_e)
        def _():
            pj = cur[1]
            pg = cur[2]
            slot = lax.rem(pg, N_XBUF)
            pltpu.make_async_copy(window(xs_hbm, pe, pj), xbuf.at[slot], gsem.at[slot]).start()
            last = pj + 1 >= nsub_ref[pe]
            cur[0] = jnp.where(last, nxt[pe], pe)
            cur[1] = jnp.where(last, 0, pj + 1)
            cur[2] = pg + 1

    def out_wait():
        pltpu.make_async_copy(ybuf.at[0], y_hbm.at[pl.ds(0, sub_rows)], osem).wait()

    @pl.when(e == 0)
    def _():
        def fill(i, following):
            x = N_EXPERTS - 1 - i
            nxt[x] = following
            return jnp.where(nsub_ref[x] > 0, x, following)

        cur[0] = lax.fori_loop(0, N_EXPERTS, fill, N_EXPERTS)
        cur[1] = 0
        cur[2] = 0
        cur[3] = 0
        for cp in weight_copies(0, 0):
            cp.start()
        for _ in range(LOOKAHEAD):
            produce()

    wslot = e & 1

    @pl.when(e + 1 < n_e)
    def _():
        for cp in weight_copies(e + 1, 1 - wslot):
            cp.start()

    for cp in weight_copies(e, wslot):
        cp.wait()
    wg_bf[...] = wg_f32[wslot].astype(BF16)
    wu_bf[...] = wu_f32[wslot].astype(BF16)
    wd_bf[...] = wd_f32[wslot].astype(BF16)

    def step(j, carry):
        g = cur[3]
        produce()
        slot = lax.rem(g, N_XBUF)
        yslot = g & 1
        pltpu.make_async_copy(xs_hbm.at[pl.ds(0, sub_rows)], xbuf.at[slot], gsem.at[slot]).wait()
        xb = _rows_from_tiles(xbuf[slot]).astype(BF16)
        a = _dot(xb, wg_bf[...])
        b = _dot(xb, wu_bf[...])
        ybuf[yslot] = _rows_to_tiles(_dot((_silu(a) * b).astype(BF16), wd_bf[...]))

        @pl.when(g > 0)
        def _():
            out_wait()

        pltpu.make_async_copy(ybuf.at[yslot], window(y_hbm, e, j), osem).start()
        cur[3] = g + 1
        return carry

    lax.fori_loop(0, nsub, step, 0)

    @pl.when(e == n_e - 1)
    def _():
        @pl.when(cur[3] > 0)
        def _():
            out_wait()

        ybuf[0] = jnp.zeros(ybuf.shape[1:], F32)
        tail = pltpu.make_async_copy(ybuf.at[0], y_hbm.at[pl.ds(y_hbm.shape[0] - sub_rows, sub_rows)], osem)
        tail.start()
        tail.wait()


def _experts(start, nsub, xs_tiles, wg, wu, wd):
    sub_rows = SUB * ROW_TILES
    return pl.pallas_call(
        _experts_body,
        out_shape=jax.ShapeDtypeStruct(xs_tiles.shape, F32),
        grid_spec=pltpu.PrefetchScalarGridSpec(
            num_scalar_prefetch=2,
            grid=(N_EXPERTS,),
            in_specs=[pl.BlockSpec(memory_space=pl.ANY)] * 4,
            out_specs=pl.BlockSpec(memory_space=pl.ANY),
            scratch_shapes=[pltpu.VMEM((N_XBUF, sub_rows, 128), F32), pltpu.VMEM((2, sub_rows, 128), F32),
                            pltpu.VMEM((2, D_MODEL, FF), F32), pltpu.VMEM((2, D_MODEL, FF), F32),
                            pltpu.VMEM((2, FF, D_MODEL), F32),
                            pltpu.VMEM((D_MODEL, FF), BF16), pltpu.VMEM((D_MODEL, FF), BF16),
                            pltpu.VMEM((FF, D_MODEL), BF16), pltpu.SMEM((4,), I32), pltpu.SMEM((N_EXPERTS,), I32),
                            pltpu.SemaphoreType.DMA((N_XBUF,)), pltpu.SemaphoreType.DMA(()),
                            pltpu.SemaphoreType.DMA((2, 3))]),
        compiler_params=_cparams(),
        name="experts",
    )(start, nsub, xs_tiles, wg, wu, wd)


def _combine_body(n_tok, n_ctx_tiles, dest_ref,
                  y_hbm, topw_ref, tok_ref, xnew_ref, mod_ref, sg_ref, su_ref, sd_ref, gpost_ref,
                  outp_ref, outs_ref, gbuf, wcol, fbuf, gsem):
    i = pl.program_id(0)
    n = pl.num_programs(0)

    def gather(tile, slot):
        def per_choice(k, carry):
            base = k * n_tok + tile * TT
            _start_row_gather(y_hbm, lambda r: dest_ref[base + r], gbuf.at[slot, k], gsem.at[slot], TT)
            return carry
        lax.fori_loop(0, TOP_K, per_choice, 0)

    def gather_wait(slot):
        for k in range(TOP_K):
            pltpu.make_async_copy(y_hbm.at[pl.ds(0, TT * ROW_TILES)], gbuf.at[slot, k], gsem.at[slot]).wait()

    slot = i & 1

    @pl.when(i == 0)
    def _():
        gather(0, 0)

    def tile_step(cur_slot):
        nxt_slot = 1 - cur_slot
        nxt_tile = jnp.minimum(i + 1, n - 1)
        for k in range(TOP_K):
            base = k * n_tok + nxt_tile * TT
            _start_row_gather(y_hbm, lambda r: dest_ref[base + r], gbuf.at[nxt_slot, k], gsem.at[nxt_slot], TT)

        w_t = jnp.concatenate([topw_ref[...], jnp.zeros((128 - TOP_K, TT), F32)], axis=0).T
        for k in range(TOP_K):
            wcol[k] = jnp.broadcast_to(w_t[:, k:k + 1], (TT, 128))
        h = tok_ref[...]
        fbuf[...] = _dot((_silu(_dot(h, sg_ref[...])) * _dot(h, su_ref[...])).astype(BF16), sd_ref[...])

        gather_wait(cur_slot)
        for g in range(TT // 8):
            rows = slice(g * 8, (g + 1) * 8)
            tiles = []
            for r in range(8):
                t = g * 8 + r
                acc = None
                for k in range(TOP_K):
                    w = jnp.broadcast_to(wcol[k, t:t + 1, :], (8, 128))
                    term = gbuf[cur_slot, k, t * ROW_TILES:(t + 1) * ROW_TILES, :] * w
                    acc = term if acc is None else acc + term
                tiles.append(acc)
            f = fbuf[rows, :] + jnp.concatenate(_transpose8(tiles), axis=1)
            fbuf[rows, :] = xnew_ref[rows, :] + mod_ref[0, 5:6, :] * _rms(f, gpost_ref[...])

    for s in range(2):
        pl.when(slot == s)(functools.partial(tile_step, s))

    @pl.when(i == n - 1)
    def _():
        gather_wait(1 - slot)

    @pl.when(i < n_ctx_tiles)
    def _():
        outp_ref[...] = fbuf[...]

    @pl.when(i >= n_ctx_tiles)
    def _():
        outs_ref[...] = fbuf[...]


def _combine(tiles, dest_flat, ysorted, topw, tok, xnew, mod3, sg_bf, su_bf, sd_bf, g_post):
    n_tok = tok.shape[0]
    n_ctx_tok = tiles.n_ctx * TM
    n_ctx_tiles = n_ctx_tok // TT
    lat_tiles_per_seq = tiles.lat_tiles * TM // TT

    def mod_row(i):
        return jnp.where(i < n_ctx_tiles, 0, 1 + jnp.maximum(i - n_ctx_tiles, 0) // lat_tiles_per_seq)

    def full(shape):
        return pl.BlockSpec(shape, lambda i, *_: (0,) * len(shape))

    rows = pl.BlockSpec((TT, D_MODEL), lambda i, *_: (i, 0))
    return pl.pallas_call(
        functools.partial(_combine_body, n_tok, n_ctx_tiles),
        out_shape=(jax.ShapeDtypeStruct((n_ctx_tok, D_MODEL), F32),
                   jax.ShapeDtypeStruct((n_tok - n_ctx_tok, D_MODEL), F32)),
        grid_spec=pltpu.PrefetchScalarGridSpec(
            num_scalar_prefetch=1,
            grid=(n_tok // TT,),
            in_specs=[pl.BlockSpec(memory_space=pl.ANY),
                      pl.BlockSpec((TOP_K, TT), lambda i, *_: (0, i)),
                      rows, rows,
                      pl.BlockSpec((1, 6, D_MODEL), lambda i, *_: (mod_row(i), 0, 0)),
                      full((D_MODEL, FF)), full((D_MODEL, FF)), full((FF, D_MODEL)), full((1, D_MODEL))],
            out_specs=(pl.BlockSpec((TT, D_MODEL), lambda i, *_: (jnp.minimum(i, n_ctx_tiles - 1), 0)),
                       pl.BlockSpec((TT, D_MODEL), lambda i, *_: (jnp.maximum(i - n_ctx_tiles, 0), 0))),
            scratch_shapes=[pltpu.VMEM((2, TOP_K, TT * ROW_TILES, 128), F32), pltpu.VMEM((TOP_K, TT, 128), F32),
                            pltpu.VMEM((TT, D_MODEL), F32), pltpu.SemaphoreType.DMA((2,))]),
        compiler_params=_cparams(),
        name="combine",
    )(dest_flat, ysorted, topw, tok, xnew, mod3, sg_bf, su_bf, sd_bf, g_post)


def _rope_tables(lat_len, lat_tiles):
    rows = lat_len // GRID_W
    row = jnp.repeat(jnp.arange(rows, dtype=F32), GRID_W)
    col = jnp.tile(jnp.arange(GRID_W, dtype=F32), rows)
    inv = ROPE_BASE ** (-jnp.arange(ROPE_FREQS, dtype=F32) / ROPE_FREQS)
    ang = jnp.concatenate([row[:, None] * inv[None, :]] * 2 + [col[:, None] * inv[None, :]] * 2, axis=1)
    sign = jnp.tile(jnp.concatenate([-jnp.ones((ROPE_FREQS,), F32), jnp.ones((ROPE_FREQS,), F32)]), 2)
    cos = jnp.cos(ang).reshape(lat_tiles, TM, DK)
    sin = (jnp.sin(ang) * sign[None, :]).reshape(lat_tiles, TM, DK)
    cos = jnp.concatenate([jnp.ones((1, TM, DK), F32), cos], axis=0)
    sin = jnp.concatenate([jnp.zeros((1, TM, DK), F32), sin], axis=0)
    return cos, sin


def kernel(x_prompt, x_sample, state_ret_fwd, state_ret_bwd, c, c_ctx, w_mod, b_mod, norm_mix_pre,
           norm_mix_post, norm_ffn_pre, norm_ffn_post, w_in, conv_w, conv_b, ret_decay_fwd,
           ret_decay_bwd, ret_gn_g, w_out, router_w, router_bias, expert_w_gate, expert_w_up,
           expert_w_down, shared_w_gate, shared_w_up, shared_w_down):
    bp, tp, d = x_prompt.shape
    bs, ts, _ = x_sample.shape
    depth = w_mod.shape[0]
    assert d == D_MODEL and tp == TM and ts % TM == 0 and bs + 1 <= 8
    tiles = _Tiles(bp, bs, ts)
    n_tok = tiles.n_tiles * TM
    n_slots = n_tok * TOP_K + SUB
    cos_t, sin_t = _rope_tables(ts, tiles.lat_tiles)

    perm = (jnp.arange(N_EXPERTS) % N_GROUPS) * GROUP_SIZE + jnp.arange(N_EXPERTS) // N_GROUPS

    xp2 = x_prompt.reshape(bp * tp, d)
    xs2 = x_sample.reshape(bs * ts, d)
    new_f, new_b = [], []
    for l in range(depth):
        c_rows = jnp.concatenate([c_ctx[None, :], c, jnp.zeros((8 - 1 - bs, d), F32)], axis=0)
        mod3 = _modulation(c_rows, w_mod[l], b_mod[l][None, :]).reshape(8, 6, d)
        dec = jnp.broadcast_to(jnp.stack([ret_decay_fwd[l], ret_decay_bwd[l]])[:, :, None, None],
                               (2, HEADS, DK, DK)).astype(F32)
        xnew, tok, tok_tiles, sf_fin, sb_fin = _token_mixer(
            tiles, xp2, xs2, mod3, norm_mix_pre[l][None, :], w_in[l].astype(BF16), conv_w[l],
            conv_b[l][None, :], dec, cos_t, sin_t, state_ret_fwd[:, l], state_ret_bwd[:, l],
            w_out[l].astype(BF16), norm_mix_post[l][None, :], norm_ffn_pre[l][None, :], ret_gn_g[l][None, :])
        new_f.append(sf_fin)
        new_b.append(sb_fin)

        rwt = router_w[l].T[perm].astype(BF16)
        bias_b = jnp.broadcast_to(router_bias[l][perm][:, None], (N_EXPERTS, 128)).astype(F32)
        topi, topw = _route(tok, rwt, bias_b)
        dest, start, counts = _dispatch_plan(topi)
        dest_flat = dest.reshape(-1)
        start_i = start[:, 0].astype(I32)
        nsub = ((counts[:, 0] + float(SUB - 1)) / float(SUB)).astype(I32)
        xs_tiles = _dispatch(dest_flat, tok_tiles, n_tok, n_slots)
        ysorted = _experts(start_i, nsub, xs_tiles, expert_w_gate[l], expert_w_up[l], expert_w_down[l])
        xp2, xs2 = _combine(tiles, dest_flat, ysorted, topw, tok, xnew, mod3,
                            shared_w_gate[l].astype(BF16), shared_w_up[l].astype(BF16),
                            shared_w_down[l].astype(BF16), norm_ffn_post[l][None, :])

    return (xp2.reshape(bp, tp, d), xs2.reshape(bs, ts, d),
            jnp.stack(new_f, axis=1), jnp.stack(new_b, axis=1))
```

```python
import functools

import numpy as np
import jax
import jax.numpy as jnp
from jax import lax
from jax.experimental import pallas as pl
from jax.experimental.pallas import tpu as pltpu

F32 = jnp.float32
BF16 = jnp.bfloat16
I32 = jnp.int32

D_MODEL = 1024
CONV_W = 512
RET_W = 512
HEADS = 4
DK = 128
CHUNK = 128
GRID_W = 64
ROPE_FREQS = 32
ROPE_BASE = 10000.0
IN_COLS = 3 * CONV_W + 4 * RET_W
N_EXPERTS = 256
N_GROUPS = 8
GROUP_SIZE = N_EXPERTS // N_GROUPS
TOPK_GROUPS = 4
TOP_K = 8
FF = 256
ROUTED_SCALE = 2.5
EPS = 1e-6

TM = 256
SUB = 128
TT = 128
RT = 512
RB = 2048
VMEM_LIMIT = 56 * 1024 * 1024


def _cparams(n_axes=1, vmem=VMEM_LIMIT):
    return pltpu.CompilerParams(dimension_semantics=("arbitrary",) * n_axes,
                                vmem_limit_bytes=vmem)


def _silu(x):
    return x * jax.nn.sigmoid(x)


def _log_sigmoid(x):
    return jnp.minimum(x, 0.0) - jnp.log1p(jnp.exp(-jnp.abs(x)))


def _rms(x, g):
    return x * lax.rsqrt(jnp.mean(x * x, axis=-1, keepdims=True) + EPS) * g


def _dot(a, b):
    return jnp.dot(a, b, preferred_element_type=F32)


def _mod_body(c_ref, w_ref, b_ref, o_ref):
    s = _silu(c_ref[...]).astype(BF16)
    o_ref[...] = _dot(s, w_ref[...].astype(BF16)) + b_ref[...]


def _modulation(c_rows, w_mod, b_mod):
    n_col = w_mod.shape[1]
    blk = 1536
    return pl.pallas_call(
        _mod_body,
        out_shape=jax.ShapeDtypeStruct((8, n_col), F32),
        grid=(n_col // blk,),
        in_specs=[pl.BlockSpec((8, D_MODEL), lambda i: (0, 0)),
                  pl.BlockSpec((D_MODEL, blk), lambda i: (0, i)),
                  pl.BlockSpec((1, blk), lambda i: (0, i))],
        out_specs=pl.BlockSpec((8, blk), lambda i: (0, i)),
        compiler_params=_cparams(),
        name="mod",
    )(c_rows, w_mod, b_mod)


class _Tiles:
    def __init__(self, n_ctx_seq, n_lat_seq, lat_len):
        self.n_ctx = n_ctx_seq
        self.lat_tiles = lat_len // TM
        self.n_lat_seq = n_lat_seq
        self.n_tiles = n_ctx_seq + n_lat_seq * self.lat_tiles

    def is_ctx(self, i):
        return i < self.n_ctx

    def lat_pos(self, i):
        j = jnp.maximum(i - self.n_ctx, 0)
        return j // self.lat_tiles, j % self.lat_tiles

    def phys_reversed(self, i):
        b, t = self.lat_pos(i)
        return jnp.where(i < self.n_ctx, i, self.n_ctx + b * self.lat_tiles + (self.lat_tiles - 1 - t))

    def mod_row(self, i):
        b, _ = self.lat_pos(i)
        return jnp.where(i < self.n_ctx, 0, 1 + b)


def _rope(x, cos, sin_signed):
    lane = lax.broadcasted_iota(I32, x.shape, 1)
    partner = jnp.where((lane & 63) < 32, pltpu.roll(x, 96, 1), pltpu.roll(x, 32, 1))
    return x * cos + partner * sin_signed


def _mix_a_body(tiles, xp_ref, xs_ref, mod_ref, gpre_ref, win_ref, cw_ref, cb_ref, dec_ref,
                cos_ref, sin_ref, s0b_ref,
                yconv_ref, q_ref, v_ref, g_ref, kt_ref, sbin_ref, sbfin_ref,
                sb_scr, tab_scr):
    i = pl.program_id(0)
    is_ctx = tiles.is_ctx(i)
    _, t_rev = tiles.lat_pos(i)
    first = jnp.logical_or(is_ctx, t_rev == 0)

    @pl.when(i == 0)
    def _():
        lg = _log_sigmoid(dec_ref[1])
        col = lax.broadcasted_iota(I32, lg.shape, 2).astype(F32)
        tab_scr[0] = jnp.exp(col * lg)
        tab_scr[1] = jnp.exp(float(CHUNK) * lg)

    @pl.when(first)
    def _():
        sb_scr[...] = jnp.where(is_ctx, 0.0, s0b_ref[0])

    x = jnp.where(is_ctx, xp_ref[...], xs_ref[...])
    h = (_rms(x, gpre_ref[...]) * (1.0 + mod_ref[0, 1:2, :]) + mod_ref[0, 0:1, :]).astype(BF16)

    def proj(k):
        return _dot(h, win_ref[:, k * 512:(k + 1) * 512])

    z = proj(1) * proj(2)
    row = lax.broadcasted_iota(I32, z.shape, 0)
    period = jnp.where(is_ctx, TM, GRID_W)
    pos = row & (period - 1)
    left = jnp.where(pos == 0, 0.0, pltpu.roll(z, 1, 0))
    right = jnp.where(pos == period - 1, 0.0, pltpu.roll(z, TM - 1, 0))
    zc = left * cw_ref[0:1, :] + z * cw_ref[1:2, :] + right * cw_ref[2:3, :] + cb_ref[...]
    yconv_ref[...] = (proj(0) * zc).astype(BF16)

    cos = cos_ref[0]
    sin = sin_ref[0]
    q = proj(3)
    k = proj(4)
    q = jnp.concatenate([_rope(q[:, hh * DK:(hh + 1) * DK], cos, sin) for hh in range(HEADS)], axis=1)
    k = jnp.concatenate([_rope(k[:, hh * DK:(hh + 1) * DK], cos, sin) for hh in range(HEADS)], axis=1)
    q_ref[...] = (q * (DK ** -0.5)).astype(BF16)
    kt = k.T
    kt_ref[...] = kt.astype(BF16)
    v = proj(5).astype(BF16)
    v_ref[...] = v
    g_ref[...] = proj(6)

    for c in (1, 0):
        for hh in range(HEADS):
            sbin_ref[c, hh] = sb_scr[hh].astype(BF16)
            kts = (kt[hh * DK:(hh + 1) * DK, c * CHUNK:(c + 1) * CHUNK] * tab_scr[0, hh]).astype(BF16)
            vc = v[c * CHUNK:(c + 1) * CHUNK, hh * DK:(hh + 1) * DK]
            sb_scr[hh] = sb_scr[hh] * tab_scr[1, hh] + _dot(kts, vc)

    @pl.when(is_ctx)
    def _():
        sbfin_ref[0] = sb_scr[...]


def _mix_b_body(tiles, xp_ref, xs_ref, mod_ref, q_ref, kt_ref, v_ref, g_ref, yconv_ref, sbin_ref,
                wout_ref, gpost_ref, gffn_ref, gn_ref, dec_ref, s0f_ref,
                xnew_ref, tok_ref, toktiles_ref, sffin_ref,
                sf_scr, tab_scr, ycat_scr):
    i = pl.program_id(0)
    is_ctx = tiles.is_ctx(i)
    _, t_pos = tiles.lat_pos(i)
    first = jnp.logical_or(is_ctx, t_pos == 0)

    @pl.when(i == 0)
    def _():
        lgf = _log_sigmoid(dec_ref[0])
        lgb = _log_sigmoid(dec_ref[1])
        row = lax.broadcasted_iota(I32, lgf.shape, 1)
        col = lax.broadcasted_iota(I32, lgf.shape, 2)
        d = (row - col).astype(F32)
        tab_scr[0] = (jnp.where(row >= col, jnp.exp(jnp.where(row >= col, d, 0.0) * lgf), 0.0)
                      + jnp.where(col >= row, jnp.exp(jnp.where(col >= row, -d, 0.0) * lgb), 0.0))
        tab_scr[1] = jnp.exp((row + 1).astype(F32) * lgf)
        tab_scr[2] = jnp.exp((CHUNK - row).astype(F32) * lgb)
        tab_scr[3] = jnp.exp((CHUNK - 1 - col).astype(F32) * lgf)
        tab_scr[4] = jnp.exp(float(CHUNK) * lgf)

    @pl.when(first)
    def _():
        sf_scr[...] = jnp.where(is_ctx, 0.0, s0f_ref[0])

    for c in range(TM // CHUNK):
        rows = slice(c * CHUNK, (c + 1) * CHUNK)
        for hh in range(HEADS):
            cols = slice(hh * DK, (hh + 1) * DK)
            qc = q_ref[rows, cols]
            ktc = kt_ref[cols, rows]
            vc = v_ref[rows, cols]
            att = (_dot(qc, ktc) * tab_scr[0, hh]).astype(BF16)
            o = (_dot(att, vc)
                 + tab_scr[1, hh] * _dot(qc, sf_scr[hh].astype(BF16))
                 + tab_scr[2, hh] * _dot(qc, sbin_ref[c, hh]))
            kts = (ktc.astype(F32) * tab_scr[3, hh]).astype(BF16)
            sf_scr[hh] = sf_scr[hh] * tab_scr[4, hh] + _dot(kts, vc)
            mu = jnp.mean(o, axis=-1, keepdims=True)
            dev = o - mu
            var = jnp.mean(dev * dev, axis=-1, keepdims=True)
            on = dev * lax.rsqrt(var + EPS) * gn_ref[:, cols]
            ycat_scr[rows, RET_W + hh * DK:RET_W + (hh + 1) * DK] = (_silu(g_ref[rows, cols]) * on).astype(BF16)
    ycat_scr[:, 0:CONV_W] = yconv_ref[...]

    @pl.when(is_ctx)
    def _():
        sffin_ref[0] = sf_scr[...]

    x = jnp.where(is_ctx, xp_ref[...], xs_ref[...])
    u = _dot(ycat_scr[...], wout_ref[...])
    xn = x + mod_ref[0, 2:3, :] * _rms(u, gpost_ref[...])
    xnew_ref[...] = xn
    tok = _rms(xn, gffn_ref[...]) * (1.0 + mod_ref[0, 4:5, :]) + mod_ref[0, 3:4, :]
    tok_ref[...] = tok.astype(BF16)
    toktiles_ref[...] = _rows_to_tiles(tok)


def _token_mixer(tiles, xp2, xs2, mod3, g_pre, win_bf, conv_w, conv_b, dec, cos_t, sin_t,
                 s0f, s0b, wout_bf, g_post, g_ffn, gn_g):
    n_tok = tiles.n_tiles * TM
    n_ctx = tiles.n_ctx
    last_ctx = n_ctx - 1

    def full(shape):
        return pl.BlockSpec(shape, lambda i: (0,) * len(shape))

    def xp_spec(phys):
        return pl.BlockSpec((TM, D_MODEL), lambda i: (jnp.minimum(phys(i), last_ctx), 0))

    def xs_spec(phys):
        return pl.BlockSpec((TM, D_MODEL), lambda i: (jnp.maximum(phys(i) - n_ctx, 0), 0))

    mod_spec = pl.BlockSpec((1, 6, D_MODEL), lambda i: (tiles.mod_row(i), 0, 0))
    state_in = pl.BlockSpec((1, HEADS, DK, DK), lambda i: (tiles.lat_pos(i)[0], 0, 0, 0))
    state_out = pl.BlockSpec((1, HEADS, DK, DK), lambda i: (jnp.minimum(i, last_ctx), 0, 0, 0))

    rev = tiles.phys_reversed

    def rope_idx(i):
        _, t = tiles.lat_pos(i)
        return jnp.where(i < n_ctx, 0, 1 + (tiles.lat_tiles - 1 - t))

    rope_spec = pl.BlockSpec((1, TM, DK), lambda i: (rope_idx(i), 0, 0))

    def rows(width, phys):
        return pl.BlockSpec((TM, width), lambda i: (phys(i), 0))

    yconv, q, v, g, kt, sbin, sb_fin = pl.pallas_call(
        functools.partial(_mix_a_body, tiles),
        out_shape=(jax.ShapeDtypeStruct((n_tok, CONV_W), BF16),
                   jax.ShapeDtypeStruct((n_tok, RET_W), BF16),
                   jax.ShapeDtypeStruct((n_tok, RET_W), BF16),
                   jax.ShapeDtypeStruct((n_tok, RET_W), F32),
                   jax.ShapeDtypeStruct((RET_W, n_tok), BF16),
                   jax.ShapeDtypeStruct((n_tok // CHUNK, HEADS, DK, DK), BF16),
                   jax.ShapeDtypeStruct((n_ctx, HEADS, DK, DK), F32)),
        grid=(tiles.n_tiles,),
        in_specs=[xp_spec(rev), xs_spec(rev), mod_spec, full((1, D_MODEL)), full((D_MODEL, IN_COLS)),
                  full((3, CONV_W)), full((1, CONV_W)), full((2, HEADS, DK, DK)),
                  rope_spec, rope_spec, state_in],
        out_specs=(rows(CONV_W, rev), rows(RET_W, rev), rows(RET_W, rev), rows(RET_W, rev),
                   pl.BlockSpec((RET_W, TM), lambda i: (0, rev(i))),
                   pl.BlockSpec((TM // CHUNK, HEADS, DK, DK), lambda i: (rev(i), 0, 0, 0)),
                   state_out),
        scratch_shapes=[pltpu.VMEM((HEADS, DK, DK), F32), pltpu.VMEM((2, HEADS, DK, DK), F32)],
        compiler_params=_cparams(),
        name="mix_a",
    )(xp2, xs2, mod3, g_pre, win_bf, conv_w, conv_b, dec, cos_t, sin_t, s0b)

    ident = lambda i: i
    xnew, tok, tok_tiles, sf_fin = pl.pallas_call(
        functools.partial(_mix_b_body, tiles),
        out_shape=(jax.ShapeDtypeStruct((n_tok, D_MODEL), F32),
                   jax.ShapeDtypeStruct((n_tok, D_MODEL), BF16),
                   jax.ShapeDtypeStruct((n_tok * ROW_TILES, 128), F32),
                   jax.ShapeDtypeStruct((n_ctx, HEADS, DK, DK), F32)),
        grid=(tiles.n_tiles,),
        in_specs=[xp_spec(ident), xs_spec(ident), mod_spec,
                  rows(RET_W, ident),
                  pl.BlockSpec((RET_W, TM), lambda i: (0, i)),
                  rows(RET_W, ident), rows(RET_W, ident), rows(CONV_W, ident),
                  pl.BlockSpec((TM // CHUNK, HEADS, DK, DK), lambda i: (i, 0, 0, 0)),
                  full((D_MODEL, D_MODEL)), full((1, D_MODEL)), full((1, D_MODEL)), full((1, RET_W)),
                  full((2, HEADS, DK, DK)), state_in],
        out_specs=(rows(D_MODEL, ident), rows(D_MODEL, ident),
                   pl.BlockSpec((TM * ROW_TILES, 128), lambda i: (i, 0)), state_out),
        scratch_shapes=[pltpu.VMEM((HEADS, DK, DK), F32), pltpu.VMEM((5, HEADS, DK, DK), F32),
                        pltpu.VMEM((TM, D_MODEL), BF16)],
        compiler_params=_cparams(),
        name="mix_b",
    )(xp2, xs2, mod3, q, kt, v, g, yconv, sbin, wout_bf, g_post, g_ffn, gn_g, dec, s0f)
    return xnew, tok, tok_tiles, sf_fin, sb_fin


def _route_body(tok_ref, rwt_ref, bias_ref, topi_ref, topw_ref):
    h = tok_ref[...]
    logits = lax.dot_general(rwt_ref[...], h, (((1,), (1,)), ((), ())), preferred_element_type=F32)
    shape3 = (GROUP_SIZE, N_GROUPS, 128)
    member = lax.broadcasted_iota(I32, shape3, 0)
    group = lax.broadcasted_iota(I32, shape3, 1)
    expert = group * GROUP_SIZE + member
    group2 = lax.broadcasted_iota(I32, (N_GROUPS, 128), 0)
    neg = -jnp.inf
    for lb in range(RT // 128):
        scores = jax.nn.sigmoid(logits[:, lb * 128:(lb + 1) * 128]).reshape(shape3)
        biased = scores + bias_ref[...].reshape(shape3)
        m1 = jnp.max(biased, axis=0)
        first = jnp.min(jnp.where(biased == m1, member, GROUP_SIZE), axis=0)
        m2 = jnp.max(jnp.where(member == first, neg, biased), axis=0)
        gs = m1 + m2
        beaten = jnp.zeros(gs.shape, I32)
        for s in range(1, N_GROUPS):
            other = pltpu.roll(gs, s, 0)
            wins = (other > gs) | ((other == gs) & (group2 >= s))
            beaten = beaten + wins.astype(I32)
        keep = beaten < TOPK_GROUPS
        cand = jnp.where(keep, biased, neg)
        idx_rows, w_rows = [], []
        for _ in range(TOP_K):
            best = jnp.max(jnp.max(cand, axis=0), axis=0, keepdims=True)
            pick = jnp.min(jnp.min(jnp.where(cand == best, expert, N_EXPERTS), axis=0), axis=0, keepdims=True)
            hit = expert == pick
            w_rows.append(jnp.sum(jnp.sum(jnp.where(hit, scores, 0.0), axis=0), axis=0, keepdims=True))
            idx_rows.append(pick)
            cand = jnp.where(hit, neg, cand)
        w = jnp.concatenate(w_rows, axis=0)
        topi_ref[:, lb * 128:(lb + 1) * 128] = jnp.concatenate(idx_rows, axis=0)
        topw_ref[:, lb * 128:(lb + 1) * 128] = w / jnp.sum(w, axis=0, keepdims=True) * ROUTED_SCALE


def _route(tok, rwt_bf, bias_b):
    n_tok = tok.shape[0]
    return pl.pallas_call(
        _route_body,
        out_shape=(jax.ShapeDtypeStruct((TOP_K, n_tok), I32), jax.ShapeDtypeStruct((TOP_K, n_tok), F32)),
        grid=(n_tok // RT,),
        in_specs=[pl.BlockSpec((RT, D_MODEL), lambda i: (i, 0)),
                  pl.BlockSpec((N_EXPERTS, D_MODEL), lambda i: (0, 0)),
                  pl.BlockSpec((N_EXPERTS, 128), lambda i: (0, 0))],
        out_specs=(pl.BlockSpec((TOP_K, RT), lambda i: (0, i)), pl.BlockSpec((TOP_K, RT), lambda i: (0, i))),
        compiler_params=_cparams(),
        name="route",
    )(tok, rwt_bf, bias_b)


def _onehot(ids_row):
    e = lax.broadcasted_iota(I32, (N_EXPERTS, 256), 0)
    return e == ids_row


def _rank_body(topi_ref, rank_ref, counts_ref, run_scr):
    i = pl.program_id(0)

    @pl.when(i == 0)
    def _():
        run_scr[...] = jnp.zeros(run_scr.shape, F32)

    a0 = lax.broadcasted_iota(I32, (256, 256), 0)
    a1 = lax.broadcasted_iota(I32, (256, 256), 1)
    upper = (a0 <= a1).astype(BF16)
    ones = jnp.ones((256, 256), BF16)
    for k in range(TOP_K):
        for sb in range(RB // 256):
            lanes = slice(sb * 256, (sb + 1) * 256)
            oh = _onehot(topi_ref[k:k + 1, lanes])
            ohb = oh.astype(BF16)
            seen = _dot(ohb, upper) + run_scr[...]
            r = jnp.sum(jnp.where(oh, seen, 0.0), axis=0, keepdims=True) - 1.0
            rank_ref[k:k + 1, lanes] = r.astype(I32)
            run_scr[...] = run_scr[...] + _dot(ohb, ones)

    @pl.when(i == pl.num_programs(0) - 1)
    def _():
        counts_ref[...] = run_scr[:, 0:128]


def _dest_body(topi_ref, rank_ref, counts_ref, dest_ref, start_ref, start_scr):
    i = pl.program_id(0)

    @pl.when(i == 0)
    def _():
        c = counts_ref[...]
        d2 = jnp.floor(c / 16384.0)
        rem = c - d2 * 16384.0
        d1 = jnp.floor(rem / 128.0)
        d0 = rem - d1 * 128.0
        e0 = lax.broadcasted_iota(I32, (N_EXPERTS, N_EXPERTS), 0)
        e1 = lax.broadcasted_iota(I32, (N_EXPERTS, N_EXPERTS), 1)
        below = (e1 < e0).astype(BF16)
        start_scr[...] = (16384.0 * _dot(below, d2.astype(BF16)) + 128.0 * _dot(below, d1.astype(BF16))
                          + _dot(below, d0.astype(BF16)))
        start_ref[...] = start_scr[...]

    start = jnp.concatenate([start_scr[...], start_scr[...]], axis=1)
    for k in range(TOP_K):
        for sb in range(RB // 256):
            lanes = slice(sb * 256, (sb + 1) * 256)
            oh = _onehot(topi_ref[k:k + 1, lanes])
            base = jnp.sum(jnp.where(oh, start, 0.0), axis=0, keepdims=True)
            dest_ref[k:k + 1, lanes] = base.astype(I32) + rank_ref[k:k + 1, lanes]


def _dispatch_plan(topi):
    n_tok = topi.shape[1]
    blk = pl.BlockSpec((TOP_K, RB), lambda i: (0, i))
    whole = pl.BlockSpec((N_EXPERTS, 128), lambda i: (0, 0))
    rank, counts = pl.pallas_call(
        _rank_body,
        out_shape=(jax.ShapeDtypeStruct((TOP_K, n_tok), I32), jax.ShapeDtypeStruct((N_EXPERTS, 128), F32)),
        grid=(n_tok // RB,),
        in_specs=[blk],
        out_specs=(blk, whole),
        scratch_shapes=[pltpu.VMEM((N_EXPERTS, 256), F32)],
        compiler_params=_cparams(),
        name="rank",
    )(topi)
    dest, start = pl.pallas_call(
        _dest_body,
        out_shape=(jax.ShapeDtypeStruct((TOP_K, n_tok), I32), jax.ShapeDtypeStruct((N_EXPERTS, 128), F32)),
        grid=(n_tok // RB,),
        in_specs=[blk, blk, whole],
        out_specs=(blk, whole),
        scratch_shapes=[pltpu.VMEM((N_EXPERTS, 128), F32)],
        compiler_params=_cparams(),
        name="dest",
    )(topi, rank, counts)
    return dest, start, counts


ROW_TILES = D_MODEL // 128


def _transpose8(vs):
    sub = lax.broadcasted_iota(I32, (8, 128), 0)
    for d in (4, 2, 1):
        keep = (sub & d) == 0
        out = list(vs)
        for i in range(8):
            if i & d == 0:
                a, b = vs[i], vs[i + d]
                out[i] = jnp.where(keep, a, pltpu.roll(b, d, 0))
                out[i + d] = jnp.where(keep, pltpu.roll(a, 8 - d, 0), b)
        vs = out
    return vs


def _rows_from_tiles(tiles):
    n_rows = tiles.shape[0] // ROW_TILES
    groups = [_transpose8([tiles[(g * 8 + r) * ROW_TILES:(g * 8 + r + 1) * ROW_TILES] for r in range(8)])
              for g in range(n_rows // 8)]
    return jnp.concatenate([jnp.concatenate([grp[c] for grp in groups], axis=0) for c in range(ROW_TILES)], axis=1)


def _rows_to_tiles(value):
    n_rows = value.shape[0]
    pieces = []
    for g in range(n_rows // 8):
        pieces += _transpose8([value[g * 8:(g + 1) * 8, c * 128:(c + 1) * 128] for c in range(ROW_TILES)])
    return jnp.concatenate(pieces, axis=0)


def _start_row_gather(src_hbm, idx_of, dst, sem, n_rows, priority_of=lambda r: r % 2):
    for r in range(n_rows):
        src_row = pl.multiple_of(idx_of(r) * ROW_TILES, ROW_TILES)
        pltpu.make_async_copy(src_hbm.at[pl.ds(src_row, ROW_TILES)], dst.at[pl.ds(r * ROW_TILES, ROW_TILES)],
                              sem).start(priority=priority_of(r))


DISPATCH_ROWS = 128
DISPATCH_DEPTH = 4


def _dispatch_body(n_tok, dest_ref, tok_hbm, xs_hbm, tbuf, zbuf, sem, fsem, zsem):
    i = pl.program_id(0)
    k = pl.program_id(1)
    n_blk = pl.num_programs(0)
    step = i * TOP_K + k
    n_steps = n_blk * TOP_K
    unit = tbuf.shape[1] // DISPATCH_ROWS
    blk_rows = DISPATCH_ROWS * unit
    buf = i & 1

    def fetch(blk, b):
        return pltpu.make_async_copy(tok_hbm.at[pl.ds(pl.multiple_of(blk * blk_rows, blk_rows), blk_rows)],
                                     tbuf.at[b], fsem.at[b])

    def batch_wait(b):
        pltpu.make_async_copy(tbuf.at[b], xs_hbm.at[pl.ds(0, blk_rows)], sem.at[b]).wait()

    @pl.when(step == 0)
    def _():
        fetch(0, 0).start()
        zbuf[...] = jnp.zeros(zbuf.shape, F32)
        tail = pltpu.make_async_copy(zbuf, xs_hbm.at[pl.ds(xs_hbm.shape[0] - zbuf.shape[0], zbuf.shape[0])], zsem)
        tail.start()
        tail.wait()

    @pl.when(k == 0)
    def _():
        fetch(i, buf).wait()

    @pl.when(jnp.logical_and(i > 0, k < DISPATCH_DEPTH))
    def _():
        for _ in range(TOP_K // DISPATCH_DEPTH):
            batch_wait(1 - buf)

    @pl.when(jnp.logical_and(k == DISPATCH_DEPTH, i + 1 < n_blk))
    def _():
        fetch(i + 1, 1 - buf).start()

    a0 = k * n_tok + i * DISPATCH_ROWS

    def scatter(b):
        for r in range(DISPATCH_ROWS):
            dst_row = pl.multiple_of(dest_ref[a0 + r] * unit, unit)
            pltpu.make_async_copy(tbuf.at[b, pl.ds(r * unit, unit)], xs_hbm.at[pl.ds(dst_row, unit)],
                                  sem.at[b]).start(priority=r % 2)

    for b in range(2):
        pl.when(buf == b)(functools.partial(scatter, b))

    @pl.when(step == n_steps - 1)
    def _():
        for _ in range(TOP_K):
            batch_wait(buf)


def _dispatch(dest_flat, tok_rows, n_tok, n_rows_out):
    assert DISPATCH_DEPTH < TOP_K and TOP_K % DISPATCH_DEPTH == 0
    unit = tok_rows.shape[0] // n_tok
    width = tok_rows.shape[1]
    return pl.pallas_call(
        functools.partial(_dispatch_body, n_tok),
        out_shape=jax.ShapeDtypeStruct((n_rows_out * unit, width), F32),
        grid_spec=pltpu.PrefetchScalarGridSpec(
            num_scalar_prefetch=1,
            grid=(n_tok // DISPATCH_ROWS, TOP_K),
            in_specs=[pl.BlockSpec(memory_space=pl.ANY)],
            out_specs=pl.BlockSpec(memory_space=pl.ANY),
            scratch_shapes=[pltpu.VMEM((2, DISPATCH_ROWS * unit, width), F32),
                            pltpu.VMEM((SUB * unit, width), F32),
                            pltpu.SemaphoreType.DMA((2,)), pltpu.SemaphoreType.DMA((2,)),
                            pltpu.SemaphoreType.DMA(())]),
        compiler_params=_cparams(2),
        name="dispatch",
    )(dest_flat, tok_rows)


N_XBUF = 4
LOOKAHEAD = 3
W_AHEAD = 2


def _experts_body(start_ref, nsub_ref,
                  xs_hbm, wg_hbm, wu_hbm, wd_hbm, y_hbm,
                  xbuf, ybuf0, ybuf1,
                  wg_f32, wu_f32, wd_f32, wg_bf, wu_bf, wd_bf, cur, nxt, gsem, osem, wsem):
    e = pl.program_id(0)
    n_e = pl.num_programs(0)
    nsub = nsub_ref[e]
    sub_rows = SUB * ROW_TILES
    overflow_row = y_hbm.shape[0] - sub_rows

    def weight_copies(ex, slot):
        return [pltpu.make_async_copy(src.at[ex], dst.at[slot], wsem.at[slot, n])
                for n, (src, dst) in enumerate(((wg_hbm, wg_f32), (wu_hbm, wu_f32), (wd_hbm, wd_f32)))]

    def window_row(ex, j):
        return pl.multiple_of((start_ref[ex] + j * SUB) * ROW_TILES, ROW_TILES)

    def produce():
        pe = cur[0]
        pj = cur[1]
        pg = cur[2]
        live = pe < n_e
        pe_c = jnp.minimum(pe, n_e - 1)
        row = pl.multiple_of(jnp.where(live, window_row(pe_c, pj), 0), ROW_TILES)
        slot = lax.rem(pg, N_XBUF)
        pltpu.make_async_copy(xs_hbm.at[pl.ds(row, sub_rows)], xbuf.at[slot], gsem.at[slot]).start()
        last = pj + 1 >= nsub_ref[pe_c]
        cur[0] = jnp.where(jnp.logical_and(live, last), nxt[pe_c], pe)
        cur[1] = jnp.where(last, 0, pj + 1)
        cur[2] = pg + 1

    def fetch_wait(slot):
        pltpu.make_async_copy(xs_hbm.at[pl.ds(0, sub_rows)], xbuf.at[slot], gsem.at[slot]).wait()

    ybuf = (ybuf0, ybuf1)

    def out_wait():
        pltpu.make_async_copy(ybuf0, y_hbm.at[pl.ds(0, sub_rows)], osem).wait()

    @pl.when(e == 0)
    def _():
        def fill(i, following):
            x = N_EXPERTS - 1 - i
            nxt[x] = following
            return jnp.where(nsub_ref[x] > 0, x, following)

        cur[0] = lax.fori_loop(0, N_EXPERTS, fill, N_EXPERTS)
        cur[1] = 0
        cur[2] = 0
        cur[3] = 0
        for ahead in range(W_AHEAD):
            for cp in weight_copies(ahead, ahead):
                cp.start()
        for _ in range(LOOKAHEAD):
            produce()
        ybuf1[...] = jnp.zeros(ybuf1.shape, F32)
        pltpu.make_async_copy(ybuf1, y_hbm.at[pl.ds(overflow_row, sub_rows)], osem).start()

    wslot = lax.rem(e, W_AHEAD + 1)

    @pl.when(e + W_AHEAD < n_e)
    def _():
        for cp in weight_copies(e + W_AHEAD, lax.rem(e + W_AHEAD, W_AHEAD + 1)):
            cp.start()

    for cp in weight_copies(e, wslot):
        cp.wait()
    wg_bf[...] = wg_f32[wslot].astype(BF16)
    wu_bf[...] = wu_f32[wslot].astype(BF16)
    wd_bf[...] = wd_f32[wslot].astype(BF16)

    def step(j, carry):
        g = cur[3]
        produce()
        slot = lax.rem(g, N_XBUF)
        fetch_wait(slot)
        xb = _rows_from_tiles(xbuf[slot]).astype(BF16)
        a = _dot(xb, wg_bf[...])
        b = _dot(xb, wu_bf[...])
        y_tiles = _rows_to_tiles(_dot((_silu(a) * b).astype(BF16), wd_bf[...]))
        row = window_row(e, j)
        for parity in range(2):
            @pl.when((g & 1) == parity)
            def _():
                ybuf[parity][...] = y_tiles
                out_wait()
                pltpu.make_async_copy(ybuf[parity], y_hbm.at[pl.ds(row, sub_rows)], osem).start()
        cur[3] = g + 1
        return carry

    lax.fori_loop(0, nsub, step, 0)

    @pl.when(e == n_e - 1)
    def _():
        total = cur[3]
        out_wait()
        for ahead in range(LOOKAHEAD):
            fetch_wait(lax.rem(total + ahead, N_XBUF))
        ybuf0[...] = jnp.zeros(ybuf0.shape, F32)
        tail = pltpu.make_async_copy(ybuf0, y_hbm.at[pl.ds(overflow_row, sub_rows)], osem)
        tail.start()
        tail.wait()


def _experts(start, nsub, xs_tiles, wg, wu, wd):
    sub_rows = SUB * ROW_TILES
    return pl.pallas_call(
        _experts_body,
        out_shape=jax.ShapeDtypeStruct(xs_tiles.shape, F32),
        grid_spec=pltpu.PrefetchScalarGridSpec(
            num_scalar_prefetch=2,
            grid=(N_EXPERTS,),
            in_specs=[pl.BlockSpec(memory_space=pl.ANY)] * 4,
            out_specs=pl.BlockSpec(memory_space=pl.ANY),
            scratch_shapes=[pltpu.VMEM((N_XBUF, sub_rows, 128), F32),
                            pltpu.VMEM((sub_rows, 128), F32), pltpu.VMEM((sub_rows, 128), F32),
                            pltpu.VMEM((W_AHEAD + 1, D_MODEL, FF), F32), pltpu.VMEM((W_AHEAD + 1, D_MODEL, FF), F32),
                            pltpu.VMEM((W_AHEAD + 1, FF, D_MODEL), F32),
                            pltpu.VMEM((D_MODEL, FF), BF16), pltpu.VMEM((D_MODEL, FF), BF16),
                            pltpu.VMEM((FF, D_MODEL), BF16), pltpu.SMEM((4,), I32), pltpu.SMEM((N_EXPERTS,), I32),
                            pltpu.SemaphoreType.DMA((N_XBUF,)), pltpu.SemaphoreType.DMA(()),
                            pltpu.SemaphoreType.DMA((W_AHEAD + 1, 3))]),
        compiler_params=_cparams(),
        name="experts",
    )(start, nsub, xs_tiles, wg, wu, wd)


def _combine_body(n_tok, n_ctx_tiles, dest_ref,
                  y_hbm, topw_ref, tok_ref, xnew_ref, mod_ref, sg_ref, su_ref, sd_ref, gpost_ref,
                  outp_ref, outs_ref, gbuf, wcol, fbuf, gsem):
    i = pl.program_id(0)
    n = pl.num_programs(0)

    def gather(tile, slot):
        def per_choice(k, carry):
            base = k * n_tok + tile * TT
            _start_row_gather(y_hbm, lambda r: dest_ref[base + r], gbuf.at[slot, k], gsem.at[slot], TT)
            return carry
        lax.fori_loop(0, TOP_K, per_choice, 0)

    def gather_wait(slot):
        for k in range(TOP_K):
            pltpu.make_async_copy(y_hbm.at[pl.ds(0, TT * ROW_TILES)], gbuf.at[slot, k], gsem.at[slot]).wait()

    slot = i & 1

    @pl.when(i == 0)
    def _():
        gather(0, 0)

    def tile_step(cur_slot):
        nxt_slot = 1 - cur_slot
        nxt_tile = jnp.minimum(i + 1, n - 1)
        for k in range(TOP_K):
            base = k * n_tok + nxt_tile * TT
            _start_row_gather(y_hbm, lambda r: dest_ref[base + r], gbuf.at[nxt_slot, k], gsem.at[nxt_slot], TT)

        w_t = jnp.concatenate([topw_ref[...], jnp.zeros((128 - TOP_K, TT), F32)], axis=0).T
        for k in range(TOP_K):
            wcol[k] = jnp.broadcast_to(w_t[:, k:k + 1], (TT, 128))
        h = tok_ref[...]
        fbuf[...] = _dot((_silu(_dot(h, sg_ref[...])) * _dot(h, su_ref[...])).astype(BF16), sd_ref[...])

        gather_wait(cur_slot)
        for g in range(TT // 8):
            rows = slice(g * 8, (g + 1) * 8)
            tiles = []
            for r in range(8):
                t = g * 8 + r
                acc = None
                for k in range(TOP_K):
                    w = jnp.broadcast_to(wcol[k, t:t + 1, :], (8, 128))
                    term = gbuf[cur_slot, k, t * ROW_TILES:(t + 1) * ROW_TILES, :] * w
                    acc = term if acc is None else acc + term
                tiles.append(acc)
            f = fbuf[rows, :] + jnp.concatenate(_transpose8(tiles), axis=1)
            fbuf[rows, :] = xnew_ref[rows, :] + mod_ref[0, 5:6, :] * _rms(f, gpost_ref[...])

    for s in range(2):
        pl.when(slot == s)(functools.partial(tile_step, s))

    @pl.when(i == n - 1)
    def _():
        gather_wait(1 - slot)

    @pl.when(i < n_ctx_tiles)
    def _():
        outp_ref[...] = fbuf[...]

    @pl.when(i >= n_ctx_tiles)
    def _():
        outs_ref[...] = fbuf[...]


def _combine(tiles, dest_flat, ysorted, topw, tok, xnew, mod3, sg_bf, su_bf, sd_bf, g_post):
    n_tok = tok.shape[0]
    n_ctx_tok = tiles.n_ctx * TM
    n_ctx_tiles = n_ctx_tok // TT
    lat_tiles_per_seq = tiles.lat_tiles * TM // TT

    def mod_row(i):
        return jnp.where(i < n_ctx_tiles, 0, 1 + jnp.maximum(i - n_ctx_tiles, 0) // lat_tiles_per_seq)

    def full(shape):
        return pl.BlockSpec(shape, lambda i, *_: (0,) * len(shape))

    rows = pl.BlockSpec((TT, D_MODEL), lambda i, *_: (i, 0))
    return pl.pallas_call(
        functools.partial(_combine_body, n_tok, n_ctx_tiles),
        out_shape=(jax.ShapeDtypeStruct((n_ctx_tok, D_MODEL), F32),
                   jax.ShapeDtypeStruct((n_tok - n_ctx_tok, D_MODEL), F32)),
        grid_spec=pltpu.PrefetchScalarGridSpec(
            num_scalar_prefetch=1,
            grid=(n_tok // TT,),
            in_specs=[pl.BlockSpec(memory_space=pl.ANY),
                      pl.BlockSpec((TOP_K, TT), lambda i, *_: (0, i)),
                      rows, rows,
                      pl.BlockSpec((1, 6, D_MODEL), lambda i, *_: (mod_row(i), 0, 0)),
                      full((D_MODEL, FF)), full((D_MODEL, FF)), full((FF, D_MODEL)), full((1, D_MODEL))],
            out_specs=(pl.BlockSpec((TT, D_MODEL), lambda i, *_: (jnp.minimum(i, n_ctx_tiles - 1), 0)),
                       pl.BlockSpec((TT, D_MODEL), lambda i, *_: (jnp.maximum(i - n_ctx_tiles, 0), 0))),
            scratch_shapes=[pltpu.VMEM((2, TOP_K, TT * ROW_TILES, 128), F32), pltpu.VMEM((TOP_K, TT, 128), F32),
                            pltpu.VMEM((TT, D_MODEL), F32), pltpu.SemaphoreType.DMA((2,))]),
        compiler_params=_cparams(),
        name="combine",
    )(dest_flat, ysorted, topw, tok, xnew, mod3, sg_bf, su_bf, sd_bf, g_post)


def _rope_tables(lat_len, lat_tiles):
    f32 = np.float32
    rows = lat_len // GRID_W
    row = np.repeat(np.arange(rows, dtype=f32), GRID_W)
    col = np.tile(np.arange(GRID_W, dtype=f32), rows)
    inv = np.power(f32(ROPE_BASE), -np.arange(ROPE_FREQS, dtype=f32) / f32(ROPE_FREQS)).astype(f32)
    ang = np.concatenate([row[:, None] * inv[None, :]] * 2 + [col[:, None] * inv[None, :]] * 2, axis=1).astype(f32)
    sign = np.tile(np.concatenate([-np.ones((ROPE_FREQS,), f32), np.ones((ROPE_FREQS,), f32)]), 2)
    cos = np.cos(ang).astype(f32).reshape(lat_tiles, TM, DK)
    sin = (np.sin(ang).astype(f32) * sign[None, :]).reshape(lat_tiles, TM, DK)
    cos = np.concatenate([np.ones((1, TM, DK), f32), cos], axis=0)
    sin = np.concatenate([np.zeros((1, TM, DK), f32), sin], axis=0)
    return jnp.asarray(cos), jnp.asarray(sin)


def kernel(x_prompt, x_sample, state_ret_fwd, state_ret_bwd, c, c_ctx, w_mod, b_mod, norm_mix_pre,
           norm_mix_post, norm_ffn_pre, norm_ffn_post, w_in, conv_w, conv_b, ret_decay_fwd,
           ret_decay_bwd, ret_gn_g, w_out, router_w, router_bias, expert_w_gate, expert_w_up,
           expert_w_down, shared_w_gate, shared_w_up, shared_w_down):
    bp, tp, d = x_prompt.shape
    bs, ts, _ = x_sample.shape
    depth = w_mod.shape[0]
    assert d == D_MODEL and tp == TM and ts % TM == 0 and bs + 1 <= 8
    tiles = _Tiles(bp, bs, ts)
    n_tok = tiles.n_tiles * TM
    n_slots = n_tok * TOP_K + SUB
    cos_t, sin_t = _rope_tables(ts, tiles.lat_tiles)

    perm = (jnp.arange(N_EXPERTS) % N_GROUPS) * GROUP_SIZE + jnp.arange(N_EXPERTS) // N_GROUPS

    xp2 = x_prompt.reshape(bp * tp, d)
    xs2 = x_sample.reshape(bs * ts, d)
    new_f, new_b = [], []
    for l in range(depth):
        c_rows = jnp.concatenate([c_ctx[None, :], c, jnp.zeros((8 - 1 - bs, d), F32)], axis=0)
        mod3 = _modulation(c_rows, w_mod[l], b_mod[l][None, :]).reshape(8, 6, d)
        dec = jnp.broadcast_to(jnp.stack([ret_decay_fwd[l], ret_decay_bwd[l]])[:, :, None, None],
                               (2, HEADS, DK, DK)).astype(F32)
        xnew, tok, tok_tiles, sf_fin, sb_fin = _token_mixer(
            tiles, xp2, xs2, mod3, norm_mix_pre[l][None, :], w_in[l].astype(BF16), conv_w[l],
            conv_b[l][None, :], dec, cos_t, sin_t, state_ret_fwd[:, l], state_ret_bwd[:, l],
            w_out[l].astype(BF16), norm_mix_post[l][None, :], norm_ffn_pre[l][None, :], ret_gn_g[l][None, :])
        new_f.append(sf_fin)
        new_b.append(sb_fin)

        rwt = router_w[l].T[perm].astype(BF16)
        bias_b = jnp.broadcast_to(router_bias[l][perm][:, None], (N_EXPERTS, 128)).astype(F32)
        topi, topw = _route(tok, rwt, bias_b)
        dest, start, counts = _dispatch_plan(topi)
        dest_flat = dest.reshape(-1)
        start_i = start[:, 0].astype(I32)
        nsub = ((counts[:, 0] + float(SUB - 1)) / float(SUB)).astype(I32)
        xs_tiles = _dispatch(dest_flat, tok_tiles, n_tok, n_slots)
        ysorted = _experts(start_i, nsub, xs_tiles, expert_w_gate[l], expert_w_up[l], expert_w_down[l])
        xp2, xs2 = _combine(tiles, dest_flat, ysorted, topw, tok, xnew, mod3,
                            shared_w_gate[l].astype(BF16), shared_w_up[l].astype(BF16),
                            shared_w_down[l].astype(BF16), norm_ffn_post[l][None, :])

    return (xp2.reshape(bp, tp, d), xs2.reshape(bs, ts, d),
            jnp.stack(new_f, axis=1), jnp.stack(new_b, axis=1))
```

```python
import functools

import numpy as np
import jax
import jax.numpy as jnp
from jax import lax
from jax.experimental import pallas as pl
from jax.experimental.pallas import tpu as pltpu

F32 = jnp.float32
BF16 = jnp.bfloat16
I32 = jnp.int32

D_MODEL = 1024
CONV_W = 512
RET_W = 512
HEADS = 4
DK = 128
CHUNK = 128
GRID_W = 64
ROPE_FREQS = 32
ROPE_BASE = 10000.0
IN_COLS = 3 * CONV_W + 4 * RET_W
N_EXPERTS = 256
N_GROUPS = 8
GROUP_SIZE = N_EXPERTS // N_GROUPS
TOPK_GROUPS = 4
TOP_K = 8
FF = 256
ROUTED_SCALE = 2.5
EPS = 1e-6

TM = 256
SUB = 128
TT = 128
RT = 512
RB = 2048
VMEM_LIMIT = 56 * 1024 * 1024


def _cparams(n_axes=1, vmem=VMEM_LIMIT):
    return pltpu.CompilerParams(dimension_semantics=("arbitrary",) * n_axes,
                                vmem_limit_bytes=vmem)


def _silu(x):
    return x * jax.nn.sigmoid(x)


def _log_sigmoid(x):
    return jnp.minimum(x, 0.0) - jnp.log1p(jnp.exp(-jnp.abs(x)))


def _rms(x, g):
    return x * lax.rsqrt(jnp.mean(x * x, axis=-1, keepdims=True) + EPS) * g


def _dot(a, b):
    return jnp.dot(a, b, preferred_element_type=F32)


def _mod_body(c_ref, w_ref, b_ref, o_ref):
    s = _silu(c_ref[...]).astype(BF16)
    o_ref[...] = _dot(s, w_ref[...].astype(BF16)) + b_ref[...]


def _modulation(c_rows, w_mod, b_mod):
    n_col = w_mod.shape[1]
    blk = 1536
    return pl.pallas_call(
        _mod_body,
        out_shape=jax.ShapeDtypeStruct((8, n_col), F32),
        grid=(n_col // blk,),
        in_specs=[pl.BlockSpec((8, D_MODEL), lambda i: (0, 0)),
                  pl.BlockSpec((D_MODEL, blk), lambda i: (0, i)),
                  pl.BlockSpec((1, blk), lambda i: (0, i))],
        out_specs=pl.BlockSpec((8, blk), lambda i: (0, i)),
        compiler_params=_cparams(),
        name="mod",
    )(c_rows, w_mod, b_mod)


class _Tiles:
    def __init__(self, n_ctx_seq, n_lat_seq, lat_len):
        self.n_ctx = n_ctx_seq
        self.lat_tiles = lat_len // TM
        self.n_lat_seq = n_lat_seq
        self.n_tiles = n_ctx_seq + n_lat_seq * self.lat_tiles

    def is_ctx(self, i):
        return i < self.n_ctx

    def lat_pos(self, i):
        j = jnp.maximum(i - self.n_ctx, 0)
        return j // self.lat_tiles, j % self.lat_tiles

    def phys_reversed(self, i):
        b, t = self.lat_pos(i)
        return jnp.where(i < self.n_ctx, i, self.n_ctx + b * self.lat_tiles + (self.lat_tiles - 1 - t))

    def mod_row(self, i):
        b, _ = self.lat_pos(i)
        return jnp.where(i < self.n_ctx, 0, 1 + b)


def _rope(x, cos, sin_signed):
    lane = lax.broadcasted_iota(I32, x.shape, 1)
    partner = jnp.where((lane & 63) < 32, pltpu.roll(x, 96, 1), pltpu.roll(x, 32, 1))
    return x * cos + partner * sin_signed


def _mix_a_body(tiles, xp_ref, xs_ref, mod_ref, gpre_ref, win_ref, cw_ref, cb_ref, dec_ref,
                cos_ref, sin_ref, s0b_ref,
                yconv_ref, q_ref, v_ref, g_ref, kt_ref, sbin_ref, sbfin_ref,
                sb_scr, tab_scr):
    i = pl.program_id(0)
    is_ctx = tiles.is_ctx(i)
    _, t_rev = tiles.lat_pos(i)
    first = jnp.logical_or(is_ctx, t_rev == 0)

    @pl.when(i == 0)
    def _():
        lg = _log_sigmoid(dec_ref[1])
        col = lax.broadcasted_iota(I32, lg.shape, 2).astype(F32)
        tab_scr[0] = jnp.exp(col * lg)
        tab_scr[1] = jnp.exp(float(CHUNK) * lg)

    @pl.when(first)
    def _():
        sb_scr[...] = jnp.where(is_ctx, 0.0, s0b_ref[0])

    x = jnp.where(is_ctx, xp_ref[...], xs_ref[...])
    h = (_rms(x, gpre_ref[...]) * (1.0 + mod_ref[0, 1:2, :]) + mod_ref[0, 0:1, :]).astype(BF16)

    def proj(k):
        return _dot(h, win_ref[:, k * 512:(k + 1) * 512])

    z = proj(1) * proj(2)
    row = lax.broadcasted_iota(I32, z.shape, 0)
    period = jnp.where(is_ctx, TM, GRID_W)
    pos = row & (period - 1)
    left = jnp.where(pos == 0, 0.0, pltpu.roll(z, 1, 0))
    right = jnp.where(pos == period - 1, 0.0, pltpu.roll(z, TM - 1, 0))
    zc = left * cw_ref[0:1, :] + z * cw_ref[1:2, :] + right * cw_ref[2:3, :] + cb_ref[...]
    yconv_ref[...] = (proj(0) * zc).astype(BF16)

    cos = cos_ref[0]
    sin = sin_ref[0]
    q = proj(3)
    k = proj(4)
    q = jnp.concatenate([_rope(q[:, hh * DK:(hh + 1) * DK], cos, sin) for hh in range(HEADS)], axis=1)
    k = jnp.concatenate([_rope(k[:, hh * DK:(hh + 1) * DK], cos, sin) for hh in range(HEADS)], axis=1)
    q_ref[...] = (q * (DK ** -0.5)).astype(BF16)
    kt = k.T
    kt_ref[...] = kt.astype(BF16)
    v = proj(5).astype(BF16)
    v_ref[...] = v
    g_ref[...] = proj(6)

    for c in (1, 0):
        for hh in range(HEADS):
            sbin_ref[c, hh] = sb_scr[hh].astype(BF16)
            kts = (kt[hh * DK:(hh + 1) * DK, c * CHUNK:(c + 1) * CHUNK] * tab_scr[0, hh]).astype(BF16)
            vc = v[c * CHUNK:(c + 1) * CHUNK, hh * DK:(hh + 1) * DK]
            sb_scr[hh] = sb_scr[hh] * tab_scr[1, hh] + _dot(kts, vc)

    @pl.when(is_ctx)
    def _():
        sbfin_ref[0] = sb_scr[...]


def _mix_b_body(tiles, xp_ref, xs_ref, mod_ref, q_ref, kt_ref, v_ref, g_ref, yconv_ref, sbin_ref,
                wout_ref, gpost_ref, gffn_ref, gn_ref, dec_ref, s0f_ref,
                xnew_ref, tok_ref, toktiles_ref, sffin_ref,
                sf_scr, tab_scr, ycat_scr):
    i = pl.program_id(0)
    is_ctx = tiles.is_ctx(i)
    _, t_pos = tiles.lat_pos(i)
    first = jnp.logical_or(is_ctx, t_pos == 0)

    @pl.when(i == 0)
    def _():
        lgf = _log_sigmoid(dec_ref[0])
        lgb = _log_sigmoid(dec_ref[1])
        row = lax.broadcasted_iota(I32, lgf.shape, 1)
        col = lax.broadcasted_iota(I32, lgf.shape, 2)
        d = (row - col).astype(F32)
        tab_scr[0] = (jnp.where(row >= col, jnp.exp(jnp.where(row >= col, d, 0.0) * lgf), 0.0)
                      + jnp.where(col >= row, jnp.exp(jnp.where(col >= row, -d, 0.0) * lgb), 0.0))
        tab_scr[1] = jnp.exp((row + 1).astype(F32) * lgf)
        tab_scr[2] = jnp.exp((CHUNK - row).astype(F32) * lgb)
        tab_scr[3] = jnp.exp((CHUNK - 1 - col).astype(F32) * lgf)
        tab_scr[4] = jnp.exp(float(CHUNK) * lgf)

    @pl.when(first)
    def _():
        sf_scr[...] = jnp.where(is_ctx, 0.0, s0f_ref[0])

    for c in range(TM // CHUNK):
        rows = slice(c * CHUNK, (c + 1) * CHUNK)
        for hh in range(HEADS):
            cols = slice(hh * DK, (hh + 1) * DK)
            qc = q_ref[rows, cols]
            ktc = kt_ref[cols, rows]
            vc = v_ref[rows, cols]
            att = (_dot(qc, ktc) * tab_scr[0, hh]).astype(BF16)
            o = (_dot(att, vc)
                 + tab_scr[1, hh] * _dot(qc, sf_scr[hh].astype(BF16))
                 + tab_scr[2, hh] * _dot(qc, sbin_ref[c, hh]))
            kts = (ktc.astype(F32) * tab_scr[3, hh]).astype(BF16)
            sf_scr[hh] = sf_scr[hh] * tab_scr[4, hh] + _dot(kts, vc)
            mu = jnp.mean(o, axis=-1, keepdims=True)
            dev = o - mu
            var = jnp.mean(dev * dev, axis=-1, keepdims=True)
            on = dev * lax.rsqrt(var + EPS) * gn_ref[:, cols]
            ycat_scr[rows, RET_W + hh * DK:RET_W + (hh + 1) * DK] = (_silu(g_ref[rows, cols]) * on).astype(BF16)
    ycat_scr[:, 0:CONV_W] = yconv_ref[...]

    @pl.when(is_ctx)
    def _():
        sffin_ref[0] = sf_scr[...]

    x = jnp.where(is_ctx, xp_ref[...], xs_ref[...])
    u = _dot(ycat_scr[...], wout_ref[...])
    xn = x + mod_ref[0, 2:3, :] * _rms(u, gpost_ref[...])
    xnew_ref[...] = xn
    tok = _rms(xn, gffn_ref[...]) * (1.0 + mod_ref[0, 4:5, :]) + mod_ref[0, 3:4, :]
    tok_ref[...] = tok.astype(BF16)
    toktiles_ref[...] = _halftiles_from_rows(_pack_rows(tok))


def _token_mixer(tiles, xp2, xs2, mod3, g_pre, win_bf, conv_w, conv_b, dec, cos_t, sin_t,
                 s0f, s0b, wout_bf, g_post, g_ffn, gn_g):
    n_tok = tiles.n_tiles * TM
    n_ctx = tiles.n_ctx
    last_ctx = n_ctx - 1

    def full(shape):
        return pl.BlockSpec(shape, lambda i: (0,) * len(shape))

    def xp_spec(phys):
        return pl.BlockSpec((TM, D_MODEL), lambda i: (jnp.minimum(phys(i), last_ctx), 0))

    def xs_spec(phys):
        return pl.BlockSpec((TM, D_MODEL), lambda i: (jnp.maximum(phys(i) - n_ctx, 0), 0))

    mod_spec = pl.BlockSpec((1, 6, D_MODEL), lambda i: (tiles.mod_row(i), 0, 0))
    state_in = pl.BlockSpec((1, HEADS, DK, DK), lambda i: (tiles.lat_pos(i)[0], 0, 0, 0))
    state_out = pl.BlockSpec((1, HEADS, DK, DK), lambda i: (jnp.minimum(i, last_ctx), 0, 0, 0))

    rev = tiles.phys_reversed

    def rope_idx(i):
        _, t = tiles.lat_pos(i)
        return jnp.where(i < n_ctx, 0, 1 + (tiles.lat_tiles - 1 - t))

    rope_spec = pl.BlockSpec((1, TM, DK), lambda i: (rope_idx(i), 0, 0))

    def rows(width, phys):
        return pl.BlockSpec((TM, width), lambda i: (phys(i), 0))

    yconv, q, v, g, kt, sbin, sb_fin = pl.pallas_call(
        functools.partial(_mix_a_body, tiles),
        out_shape=(jax.ShapeDtypeStruct((n_tok, CONV_W), BF16),
                   jax.ShapeDtypeStruct((n_tok, RET_W), BF16),
                   jax.ShapeDtypeStruct((n_tok, RET_W), BF16),
                   jax.ShapeDtypeStruct((n_tok, RET_W), F32),
                   jax.ShapeDtypeStruct((RET_W, n_tok), BF16),
                   jax.ShapeDtypeStruct((n_tok // CHUNK, HEADS, DK, DK), BF16),
                   jax.ShapeDtypeStruct((n_ctx, HEADS, DK, DK), F32)),
        grid=(tiles.n_tiles,),
        in_specs=[xp_spec(rev), xs_spec(rev), mod_spec, full((1, D_MODEL)), full((D_MODEL, IN_COLS)),
                  full((3, CONV_W)), full((1, CONV_W)), full((2, HEADS, DK, DK)),
                  rope_spec, rope_spec, state_in],
        out_specs=(rows(CONV_W, rev), rows(RET_W, rev), rows(RET_W, rev), rows(RET_W, rev),
                   pl.BlockSpec((RET_W, TM), lambda i: (0, rev(i))),
                   pl.BlockSpec((TM // CHUNK, HEADS, DK, DK), lambda i: (rev(i), 0, 0, 0)),
                   state_out),
        scratch_shapes=[pltpu.VMEM((HEADS, DK, DK), F32), pltpu.VMEM((2, HEADS, DK, DK), F32)],
        compiler_params=_cparams(),
        name="mix_a",
    )(xp2, xs2, mod3, g_pre, win_bf, conv_w, conv_b, dec, cos_t, sin_t, s0b)

    ident = lambda i: i
    xnew, tok, tok_tiles, sf_fin = pl.pallas_call(
        functools.partial(_mix_b_body, tiles),
        out_shape=(jax.ShapeDtypeStruct((n_tok, D_MODEL), F32),
                   jax.ShapeDtypeStruct((n_tok, D_MODEL), BF16),
                   jax.ShapeDtypeStruct((n_tok * HALF_TILE, 128), U32),
                   jax.ShapeDtypeStruct((n_ctx, HEADS, DK, DK), F32)),
        grid=(tiles.n_tiles,),
        in_specs=[xp_spec(ident), xs_spec(ident), mod_spec,
                  rows(RET_W, ident),
                  pl.BlockSpec((RET_W, TM), lambda i: (0, i)),
                  rows(RET_W, ident), rows(RET_W, ident), rows(CONV_W, ident),
                  pl.BlockSpec((TM // CHUNK, HEADS, DK, DK), lambda i: (i, 0, 0, 0)),
                  full((D_MODEL, D_MODEL)), full((1, D_MODEL)), full((1, D_MODEL)), full((1, RET_W)),
                  full((2, HEADS, DK, DK)), state_in],
        out_specs=(rows(D_MODEL, ident), rows(D_MODEL, ident),
                   pl.BlockSpec((TM * HALF_TILE, 128), lambda i: (i, 0)), state_out),
        scratch_shapes=[pltpu.VMEM((HEADS, DK, DK), F32), pltpu.VMEM((5, HEADS, DK, DK), F32),
                        pltpu.VMEM((TM, D_MODEL), BF16)],
        compiler_params=_cparams(),
        name="mix_b",
    )(xp2, xs2, mod3, q, kt, v, g, yconv, sbin, wout_bf, g_post, g_ffn, gn_g, dec, s0f)
    return xnew, tok, tok_tiles, sf_fin, sb_fin


def _route_body(tok_ref, rwt_ref, bias_ref, topi_ref, topw_ref):
    h = tok_ref[...]
    logits = lax.dot_general(rwt_ref[...], h, (((1,), (1,)), ((), ())), preferred_element_type=F32)
    shape3 = (GROUP_SIZE, N_GROUPS, 128)
    member = lax.broadcasted_iota(I32, shape3, 0)
    group = lax.broadcasted_iota(I32, shape3, 1)
    expert = group * GROUP_SIZE + member
    group2 = lax.broadcasted_iota(I32, (N_GROUPS, 128), 0)
    neg = -jnp.inf
    for lb in range(RT // 128):
        scores = jax.nn.sigmoid(logits[:, lb * 128:(lb + 1) * 128]).reshape(shape3)
        biased = scores + bias_ref[...].reshape(shape3)
        m1 = jnp.max(biased, axis=0)
        first = jnp.min(jnp.where(biased == m1, member, GROUP_SIZE), axis=0)
        m2 = jnp.max(jnp.where(member == first, neg, biased), axis=0)
        gs = m1 + m2
        beaten = jnp.zeros(gs.shape, I32)
        for s in range(1, N_GROUPS):
            other = pltpu.roll(gs, s, 0)
            wins = (other > gs) | ((other == gs) & (group2 >= s))
            beaten = beaten + wins.astype(I32)
        keep = beaten < TOPK_GROUPS
        cand = jnp.where(keep, biased, neg)
        idx_rows, w_rows = [], []
        for _ in range(TOP_K):
            best = jnp.max(jnp.max(cand, axis=0), axis=0, keepdims=True)
            pick = jnp.min(jnp.min(jnp.where(cand == best, expert, N_EXPERTS), axis=0), axis=0, keepdims=True)
            hit = expert == pick
            w_rows.append(jnp.sum(jnp.sum(jnp.where(hit, scores, 0.0), axis=0), axis=0, keepdims=True))
            idx_rows.append(pick)
            cand = jnp.where(hit, neg, cand)
        w = jnp.concatenate(w_rows, axis=0)
        topi_ref[:, lb * 128:(lb + 1) * 128] = jnp.concatenate(idx_rows, axis=0)
        topw_ref[:, lb * 128:(lb + 1) * 128] = w / jnp.sum(w, axis=0, keepdims=True) * ROUTED_SCALE


def _route(tok, rwt_bf, bias_b):
    n_tok = tok.shape[0]
    return pl.pallas_call(
        _route_body,
        out_shape=(jax.ShapeDtypeStruct((TOP_K, n_tok), I32), jax.ShapeDtypeStruct((TOP_K, n_tok), F32)),
        grid=(n_tok // RT,),
        in_specs=[pl.BlockSpec((RT, D_MODEL), lambda i: (i, 0)),
                  pl.BlockSpec((N_EXPERTS, D_MODEL), lambda i: (0, 0)),
                  pl.BlockSpec((N_EXPERTS, 128), lambda i: (0, 0))],
        out_specs=(pl.BlockSpec((TOP_K, RT), lambda i: (0, i)), pl.BlockSpec((TOP_K, RT), lambda i: (0, i))),
        compiler_params=_cparams(),
        name="route",
    )(tok, rwt_bf, bias_b)


def _onehot(ids_row):
    e = lax.broadcasted_iota(I32, (N_EXPERTS, 256), 0)
    return e == ids_row


def _rank_body(topi_ref, rank_ref, counts_ref, run_scr):
    i = pl.program_id(0)

    @pl.when(i == 0)
    def _():
        run_scr[...] = jnp.zeros(run_scr.shape, F32)

    a0 = lax.broadcasted_iota(I32, (256, 256), 0)
    a1 = lax.broadcasted_iota(I32, (256, 256), 1)
    upper = (a0 <= a1).astype(BF16)
    ones = jnp.ones((256, 256), BF16)
    for k in range(TOP_K):
        for sb in range(RB // 256):
            lanes = slice(sb * 256, (sb + 1) * 256)
            oh = _onehot(topi_ref[k:k + 1, lanes])
            ohb = oh.astype(BF16)
            seen = _dot(ohb, upper) + run_scr[...]
            r = jnp.sum(jnp.where(oh, seen, 0.0), axis=0, keepdims=True) - 1.0
            rank_ref[k:k + 1, lanes] = r.astype(I32)
            run_scr[...] = run_scr[...] + _dot(ohb, ones)

    @pl.when(i == pl.num_programs(0) - 1)
    def _():
        counts_ref[...] = run_scr[:, 0:128]


def _dest_body(topi_ref, rank_ref, counts_ref, dest_ref, start_ref, start_scr):
    i = pl.program_id(0)

    @pl.when(i == 0)
    def _():
        c = counts_ref[...]
        d2 = jnp.floor(c / 16384.0)
        rem = c - d2 * 16384.0
        d1 = jnp.floor(rem / 128.0)
        d0 = rem - d1 * 128.0
        e0 = lax.broadcasted_iota(I32, (N_EXPERTS, N_EXPERTS), 0)
        e1 = lax.broadcasted_iota(I32, (N_EXPERTS, N_EXPERTS), 1)
        below = (e1 < e0).astype(BF16)
        start_scr[...] = (16384.0 * _dot(below, d2.astype(BF16)) + 128.0 * _dot(below, d1.astype(BF16))
                          + _dot(below, d0.astype(BF16)))
        start_ref[...] = start_scr[...]

    start = jnp.concatenate([start_scr[...], start_scr[...]], axis=1)
    for k in range(TOP_K):
        for sb in range(RB // 256):
            lanes = slice(sb * 256, (sb + 1) * 256)
            oh = _onehot(topi_ref[k:k + 1, lanes])
            base = jnp.sum(jnp.where(oh, start, 0.0), axis=0, keepdims=True)
            dest_ref[k:k + 1, lanes] = base.astype(I32) + rank_ref[k:k + 1, lanes]


def _dispatch_plan(topi):
    n_tok = topi.shape[1]
    blk = pl.BlockSpec((TOP_K, RB), lambda i: (0, i))
    whole = pl.BlockSpec((N_EXPERTS, 128), lambda i: (0, 0))
    rank, counts = pl.pallas_call(
        _rank_body,
        out_shape=(jax.ShapeDtypeStruct((TOP_K, n_tok), I32), jax.ShapeDtypeStruct((N_EXPERTS, 128), F32)),
        grid=(n_tok // RB,),
        in_specs=[blk],
        out_specs=(blk, whole),
        scratch_shapes=[pltpu.VMEM((N_EXPERTS, 256), F32)],
        compiler_params=_cparams(),
        name="rank",
    )(topi)
    dest, start = pl.pallas_call(
        _dest_body,
        out_shape=(jax.ShapeDtypeStruct((TOP_K, n_tok), I32), jax.ShapeDtypeStruct((N_EXPERTS, 128), F32)),
        grid=(n_tok // RB,),
        in_specs=[blk, blk, whole],
        out_specs=(blk, whole),
        scratch_shapes=[pltpu.VMEM((N_EXPERTS, 128), F32)],
        compiler_params=_cparams(),
        name="dest",
    )(topi, rank, counts)
    return dest, start, counts


ROW_TILES = D_MODEL // 128


def _transpose8(vs):
    sub = lax.broadcasted_iota(I32, (8, 128), 0)
    for d in (4, 2, 1):
        keep = (sub & d) == 0
        out = list(vs)
        for i in range(8):
            if i & d == 0:
                a, b = vs[i], vs[i + d]
                out[i] = jnp.where(keep, a, pltpu.roll(b, d, 0))
                out[i + d] = jnp.where(keep, pltpu.roll(a, 8 - d, 0), b)
        vs = out
    return vs


def _rows_from_tiles(tiles):
    n_rows = tiles.shape[0] // ROW_TILES
    groups = [_transpose8([tiles[(g * 8 + r) * ROW_TILES:(g * 8 + r + 1) * ROW_TILES] for r in range(8)])
              for g in range(n_rows // 8)]
    return jnp.concatenate([jnp.concatenate([grp[c] for grp in groups], axis=0) for c in range(ROW_TILES)], axis=1)


def _rows_to_tiles(value, row_of_tile=lambda t: t):
    n_rows = value.shape[0]
    pieces = []
    for g in range(n_rows // 8):
        pieces += _transpose8([value[g * 8:(g + 1) * 8, c * 128:(c + 1) * 128] for c in range(ROW_TILES)])
    return jnp.concatenate([pieces[row_of_tile(t)] for t in range(n_rows)], axis=0)


U32 = jnp.uint32
HALF_TILE = 4
HIGH_HALF = 0xFFFF0000


def _pack_rows(x):
    half = D_MODEL // 2
    lo = pltpu.bitcast(x[:, :half].astype(BF16).astype(F32), U32)
    hi = pltpu.bitcast(x[:, half:].astype(BF16).astype(F32), U32)
    return (lo >> 16) | (hi & jnp.uint32(HIGH_HALF))


def _unpack_rows(u):
    lo = pltpu.bitcast(u << 16, F32)
    hi = pltpu.bitcast(u & jnp.uint32(HIGH_HALF), F32)
    return jnp.concatenate([lo, hi], axis=1).astype(BF16)


def _halftiles_from_rows(u):
    pieces = []
    for q in range(u.shape[0] // 16):
        pieces += _transpose8([u[q * 16 + h * 8:q * 16 + h * 8 + 8, c * 128:(c + 1) * 128]
                               for h in range(2) for c in range(HALF_TILE)])
    return jnp.concatenate(pieces, axis=0)


def _halftile_row(r):
    q, j = divmod(r, 16)
    return (q * 8 + j % 8) * 8 + (j // 8) * HALF_TILE


def _rows_from_halftiles(t):
    blocks = []
    for q in range(t.shape[0] // 64):
        outs = _transpose8([t[(q * 8 + p) * 8:(q * 8 + p + 1) * 8] for p in range(8)])
        for h in range(2):
            blocks.append(jnp.concatenate(outs[h * HALF_TILE:(h + 1) * HALF_TILE], axis=1))
    return jnp.concatenate(blocks, axis=0)


def _row_of_linear_slot(s):
    q, m = divmod(s, 16)
    return q * 16 + (m // 2 if m % 2 == 0 else 8 + m // 2)


def _start_row_gather(src_hbm, idx_of, dst, sem, n_rows, priority_of=lambda r: r % 2):
    for r in range(n_rows):
        src_row = pl.multiple_of(idx_of(r) * ROW_TILES, ROW_TILES)
        pltpu.make_async_copy(src_hbm.at[pl.ds(src_row, ROW_TILES)], dst.at[pl.ds(r * ROW_TILES, ROW_TILES)],
                              sem).start(priority=priority_of(r))


DISPATCH_ROWS = 128
DISPATCH_DEPTH = 4


def _dispatch_body(n_tok, dest_ref, tok_hbm, xs_hbm, tbuf, zbuf, sem, fsem, zsem):
    i = pl.program_id(0)
    k = pl.program_id(1)
    n_blk = pl.num_programs(0)
    step = i * TOP_K + k
    n_steps = n_blk * TOP_K
    unit = tbuf.shape[1] // DISPATCH_ROWS
    blk_rows = DISPATCH_ROWS * unit
    buf = i & 1

    def fetch(blk, b):
        return pltpu.make_async_copy(tok_hbm.at[pl.ds(pl.multiple_of(blk * blk_rows, blk_rows), blk_rows)],
                                     tbuf.at[b], fsem.at[b])

    def batch_wait(b):
        pltpu.make_async_copy(tbuf.at[b], xs_hbm.at[pl.ds(0, blk_rows)], sem.at[b]).wait()

    @pl.when(step == 0)
    def _():
        fetch(0, 0).start()
        zbuf[...] = jnp.zeros(zbuf.shape, zbuf.dtype)
        tail = pltpu.make_async_copy(zbuf, xs_hbm.at[pl.ds(xs_hbm.shape[0] - zbuf.shape[0], zbuf.shape[0])], zsem)
        tail.start()
        tail.wait()

    @pl.when(k == 0)
    def _():
        fetch(i, buf).wait()

    @pl.when(jnp.logical_and(i > 0, k < DISPATCH_DEPTH))
    def _():
        for _ in range(TOP_K // DISPATCH_DEPTH):
            batch_wait(1 - buf)

    @pl.when(jnp.logical_and(k == DISPATCH_DEPTH, i + 1 < n_blk))
    def _():
        fetch(i + 1, 1 - buf).start()

    a0 = k * n_tok + i * DISPATCH_ROWS

    def scatter(b):
        for r in range(DISPATCH_ROWS):
            dst_row = pl.multiple_of(dest_ref[a0 + r] * unit, unit)
            pltpu.make_async_copy(tbuf.at[b, pl.ds(_halftile_row(r), unit)], xs_hbm.at[pl.ds(dst_row, unit)],
                                  sem.at[b]).start(priority=r % 2)

    for b in range(2):
        pl.when(buf == b)(functools.partial(scatter, b))

    @pl.when(step == n_steps - 1)
    def _():
        for _ in range(TOP_K):
            batch_wait(buf)


def _dispatch(dest_flat, tok_rows, n_tok, n_rows_out):
    assert DISPATCH_DEPTH < TOP_K and TOP_K % DISPATCH_DEPTH == 0
    unit = tok_rows.shape[0] // n_tok
    width = tok_rows.shape[1]
    assert unit == HALF_TILE and DISPATCH_ROWS % 16 == 0
    dtype = tok_rows.dtype
    return pl.pallas_call(
        functools.partial(_dispatch_body, n_tok),
        out_shape=jax.ShapeDtypeStruct((n_rows_out * unit, width), dtype),
        grid_spec=pltpu.PrefetchScalarGridSpec(
            num_scalar_prefetch=1,
            grid=(n_tok // DISPATCH_ROWS, TOP_K),
            in_specs=[pl.BlockSpec(memory_space=pl.ANY)],
            out_specs=pl.BlockSpec(memory_space=pl.ANY),
            scratch_shapes=[pltpu.VMEM((2, DISPATCH_ROWS * unit, width), dtype),
                            pltpu.VMEM((SUB * unit, width), dtype),
                            pltpu.SemaphoreType.DMA((2,)), pltpu.SemaphoreType.DMA((2,)),
                            pltpu.SemaphoreType.DMA(())]),
        compiler_params=_cparams(2),
        name="dispatch",
    )(dest_flat, tok_rows)


N_XBUF = 4
LOOKAHEAD = 3
W_AHEAD = 2


def _experts_body(start_ref, nsub_ref,
                  xs_hbm, wg_hbm, wu_hbm, wd_hbm, y_hbm,
                  xbuf, ybuf0, ybuf1,
                  wg_f32, wu_f32, wd_f32, wg_bf, wu_bf, wd_bf, cur, nxt, gsem, osem, wsem):
    e = pl.program_id(0)
    n_e = pl.num_programs(0)
    nsub = nsub_ref[e]
    sub_rows = SUB * ROW_TILES
    in_rows = SUB * HALF_TILE
    overflow_row = y_hbm.shape[0] - sub_rows

    def weight_copies(ex, slot):
        return [pltpu.make_async_copy(src.at[ex], dst.at[slot], wsem.at[slot, n])
                for n, (src, dst) in enumerate(((wg_hbm, wg_f32), (wu_hbm, wu_f32), (wd_hbm, wd_f32)))]

    def window_row(ex, j, rows_per_slot):
        return pl.multiple_of((start_ref[ex] + j * SUB) * rows_per_slot, rows_per_slot)


    def produce():
        pe = cur[0]
        pj = cur[1]
        pg = cur[2]
        live = pe < n_e
        pe_c = jnp.minimum(pe, n_e - 1)
        row = pl.multiple_of(jnp.where(live, window_row(pe_c, pj, HALF_TILE), 0), HALF_TILE)
        slot = lax.rem(pg, N_XBUF)
        pltpu.make_async_copy(xs_hbm.at[pl.ds(row, in_rows)], xbuf.at[slot], gsem.at[slot]).start()
        last = pj + 1 >= nsub_ref[pe_c]
        cur[0] = jnp.where(jnp.logical_and(live, last), nxt[pe_c], pe)
        cur[1] = jnp.where(last, 0, pj + 1)
        cur[2] = pg + 1

    def fetch_wait(slot):
        pltpu.make_async_copy(xs_hbm.at[pl.ds(0, in_rows)], xbuf.at[slot], gsem.at[slot]).wait()

    ybuf = (ybuf0, ybuf1)

    def out_wait():
        pltpu.make_async_copy(ybuf0, y_hbm.at[pl.ds(0, sub_rows)], osem).wait()

    @pl.when(e == 0)
    def _():
        def fill(i, following):
            x = N_EXPERTS - 1 - i
            nxt[x] = following
            return jnp.where(nsub_ref[x] > 0, x, following)

        cur[0] = lax.fori_loop(0, N_EXPERTS, fill, N_EXPERTS)
        cur[1] = 0
        cur[2] = 0
        cur[3] = 0
        for ahead in range(W_AHEAD):
            for cp in weight_copies(ahead, ahead):
                cp.start()
        for _ in range(LOOKAHEAD):
            produce()
        ybuf1[...] = jnp.zeros(ybuf1.shape, F32)
        pltpu.make_async_copy(ybuf1, y_hbm.at[pl.ds(overflow_row, sub_rows)], osem).start()

    wslot = lax.rem(e, W_AHEAD + 1)

    @pl.when(e + W_AHEAD < n_e)
    def _():
        for cp in weight_copies(e + W_AHEAD, lax.rem(e + W_AHEAD, W_AHEAD + 1)):
            cp.start()

    for cp in weight_copies(e, wslot):
        cp.wait()
    wg_bf[...] = wg_f32[wslot].astype(BF16)
    wu_bf[...] = wu_f32[wslot].astype(BF16)
    wd_bf[...] = wd_f32[wslot].astype(BF16)

    def step(j, carry):
        g = cur[3]
        produce()
        slot = lax.rem(g, N_XBUF)
        fetch_wait(slot)
        xb = _unpack_rows(_rows_from_halftiles(xbuf[slot]))
        a = _dot(xb, wg_bf[...])
        b = _dot(xb, wu_bf[...])
        y_tiles = _rows_to_tiles(_dot((_silu(a) * b).astype(BF16), wd_bf[...]), _row_of_linear_slot)
        row = window_row(e, j, ROW_TILES)
        for parity in range(2):
            @pl.when((g & 1) == parity)
            def _():
                ybuf[parity][...] = y_tiles
                out_wait()
                pltpu.make_async_copy(ybuf[parity], y_hbm.at[pl.ds(row, sub_rows)], osem).start()
        cur[3] = g + 1
        return carry

    lax.fori_loop(0, nsub, step, 0)

    @pl.when(e == n_e - 1)
    def _():
        total = cur[3]
        out_wait()
        for ahead in range(LOOKAHEAD):
            fetch_wait(lax.rem(total + ahead, N_XBUF))
        ybuf0[...] = jnp.zeros(ybuf0.shape, F32)
        tail = pltpu.make_async_copy(ybuf0, y_hbm.at[pl.ds(overflow_row, sub_rows)], osem)
        tail.start()
        tail.wait()


def _experts(start, nsub, xs_tiles, wg, wu, wd):
    sub_rows = SUB * ROW_TILES
    n_slots = xs_tiles.shape[0] // HALF_TILE
    return pl.pallas_call(
        _experts_body,
        out_shape=jax.ShapeDtypeStruct((n_slots * ROW_TILES, 128), F32),
        grid_spec=pltpu.PrefetchScalarGridSpec(
            num_scalar_prefetch=2,
            grid=(N_EXPERTS,),
            in_specs=[pl.BlockSpec(memory_space=pl.ANY)] * 4,
            out_specs=pl.BlockSpec(memory_space=pl.ANY),
            scratch_shapes=[pltpu.VMEM((N_XBUF, SUB * HALF_TILE, 128), U32),
                            pltpu.VMEM((sub_rows, 128), F32), pltpu.VMEM((sub_rows, 128), F32),
                            pltpu.VMEM((W_AHEAD + 1, D_MODEL, FF), F32), pltpu.VMEM((W_AHEAD + 1, D_MODEL, FF), F32),
                            pltpu.VMEM((W_AHEAD + 1, FF, D_MODEL), F32),
                            pltpu.VMEM((D_MODEL, FF), BF16), pltpu.VMEM((D_MODEL, FF), BF16),
                            pltpu.VMEM((FF, D_MODEL), BF16), pltpu.SMEM((4,), I32), pltpu.SMEM((N_EXPERTS,), I32),
                            pltpu.SemaphoreType.DMA((N_XBUF,)), pltpu.SemaphoreType.DMA(()),
                            pltpu.SemaphoreType.DMA((W_AHEAD + 1, 3))]),
        compiler_params=_cparams(),
        name="experts",
    )(start, nsub, xs_tiles, wg, wu, wd)


def _combine_body(n_tok, n_ctx_tiles, dest_ref,
                  y_hbm, topw_ref, tok_ref, xnew_ref, mod_ref, sg_ref, su_ref, sd_ref, gpost_ref,
                  outp_ref, outs_ref, gbuf, wcol, fbuf, gsem):
    i = pl.program_id(0)
    n = pl.num_programs(0)

    def gather(tile, slot):
        def per_choice(k, carry):
            base = k * n_tok + tile * TT
            _start_row_gather(y_hbm, lambda r: dest_ref[base + r], gbuf.at[slot, k], gsem.at[slot], TT)
            return carry
        lax.fori_loop(0, TOP_K, per_choice, 0)

    def gather_wait(slot):
        for k in range(TOP_K):
            pltpu.make_async_copy(y_hbm.at[pl.ds(0, TT * ROW_TILES)], gbuf.at[slot, k], gsem.at[slot]).wait()

    slot = i & 1

    @pl.when(i == 0)
    def _():
        gather(0, 0)

    def tile_step(cur_slot):
        nxt_slot = 1 - cur_slot
        nxt_tile = jnp.minimum(i + 1, n - 1)
        for k in range(TOP_K):
            base = k * n_tok + nxt_tile * TT
            _start_row_gather(y_hbm, lambda r: dest_ref[base + r], gbuf.at[nxt_slot, k], gsem.at[nxt_slot], TT)

        w_t = jnp.concatenate([topw_ref[...], jnp.zeros((128 - TOP_K, TT), F32)], axis=0).T
        for k in range(TOP_K):
            wcol[k] = jnp.broadcast_to(w_t[:, k:k + 1], (TT, 128))
        h = tok_ref[...]
        fbuf[...] = _dot((_silu(_dot(h, sg_ref[...])) * _dot(h, su_ref[...])).astype(BF16), sd_ref[...])

        gather_wait(cur_slot)
        for g in range(TT // 8):
            rows = slice(g * 8, (g + 1) * 8)
            tiles = []
            for r in range(8):
                t = g * 8 + r
                acc = None
                for k in range(TOP_K):
                    w = jnp.broadcast_to(wcol[k, t:t + 1, :], (8, 128))
                    term = gbuf[cur_slot, k, t * ROW_TILES:(t + 1) * ROW_TILES, :] * w
                    acc = term if acc is None else acc + term
                tiles.append(acc)
            f = fbuf[rows, :] + jnp.concatenate(_transpose8(tiles), axis=1)
            fbuf[rows, :] = xnew_ref[rows, :] + mod_ref[0, 5:6, :] * _rms(f, gpost_ref[...])

    for s in range(2):
        pl.when(slot == s)(functools.partial(tile_step, s))

    @pl.when(i == n - 1)
    def _():
        gather_wait(1 - slot)

    @pl.when(i < n_ctx_tiles)
    def _():
        outp_ref[...] = fbuf[...]

    @pl.when(i >= n_ctx_tiles)
    def _():
        outs_ref[...] = fbuf[...]


def _combine(tiles, dest_flat, ysorted, topw, tok, xnew, mod3, sg_bf, su_bf, sd_bf, g_post):
    n_tok = tok.shape[0]
    n_ctx_tok = tiles.n_ctx * TM
    n_ctx_tiles = n_ctx_tok // TT
    lat_tiles_per_seq = tiles.lat_tiles * TM // TT

    def mod_row(i):
        return jnp.where(i < n_ctx_tiles, 0, 1 + jnp.maximum(i - n_ctx_tiles, 0) // lat_tiles_per_seq)

    def full(shape):
        return pl.BlockSpec(shape, lambda i, *_: (0,) * len(shape))

    rows = pl.BlockSpec((TT, D_MODEL), lambda i, *_: (i, 0))
    return pl.pallas_call(
        functools.partial(_combine_body, n_tok, n_ctx_tiles),
        out_shape=(jax.ShapeDtypeStruct((n_ctx_tok, D_MODEL), F32),
                   jax.ShapeDtypeStruct((n_tok - n_ctx_tok, D_MODEL), F32)),
        grid_spec=pltpu.PrefetchScalarGridSpec(
            num_scalar_prefetch=1,
            grid=(n_tok // TT,),
            in_specs=[pl.BlockSpec(memory_space=pl.ANY),
                      pl.BlockSpec((TOP_K, TT), lambda i, *_: (0, i)),
                      rows, rows,
                      pl.BlockSpec((1, 6, D_MODEL), lambda i, *_: (mod_row(i), 0, 0)),
                      full((D_MODEL, FF)), full((D_MODEL, FF)), full((FF, D_MODEL)), full((1, D_MODEL))],
            out_specs=(pl.BlockSpec((TT, D_MODEL), lambda i, *_: (jnp.minimum(i, n_ctx_tiles - 1), 0)),
                       pl.BlockSpec((TT, D_MODEL), lambda i, *_: (jnp.maximum(i - n_ctx_tiles, 0), 0))),
            scratch_shapes=[pltpu.VMEM((2, TOP_K, TT * ROW_TILES, 128), F32), pltpu.VMEM((TOP_K, TT, 128), F32),
                            pltpu.VMEM((TT, D_MODEL), F32), pltpu.SemaphoreType.DMA((2,))]),
        compiler_params=_cparams(),
        name="combine",
    )(dest_flat, ysorted, topw, tok, xnew, mod3, sg_bf, su_bf, sd_bf, g_post)


def _rope_tables(lat_len, lat_tiles):
    f32 = np.float32
    rows = lat_len // GRID_W
    row = np.repeat(np.arange(rows, dtype=f32), GRID_W)
    col = np.tile(np.arange(GRID_W, dtype=f32), rows)
    inv = np.power(f32(ROPE_BASE), -np.arange(ROPE_FREQS, dtype=f32) / f32(ROPE_FREQS)).astype(f32)
    ang = np.concatenate([row[:, None] * inv[None, :]] * 2 + [col[:, None] * inv[None, :]] * 2, axis=1).astype(f32)
    sign = np.tile(np.concatenate([-np.ones((ROPE_FREQS,), f32), np.ones((ROPE_FREQS,), f32)]), 2)
    cos = np.cos(ang).astype(f32).reshape(lat_tiles, TM, DK)
    sin = (np.sin(ang).astype(f32) * sign[None, :]).reshape(lat_tiles, TM, DK)
    cos = np.concatenate([np.ones((1, TM, DK), f32), cos], axis=0)
    sin = np.concatenate([np.zeros((1, TM, DK), f32), sin], axis=0)
    return jnp.asarray(cos), jnp.asarray(sin)


def kernel(x_prompt, x_sample, state_ret_fwd, state_ret_bwd, c, c_ctx, w_mod, b_mod, norm_mix_pre,
           norm_mix_post, norm_ffn_pre, norm_ffn_post, w_in, conv_w, conv_b, ret_decay_fwd,
           ret_decay_bwd, ret_gn_g, w_out, router_w, router_bias, expert_w_gate, expert_w_up,
           expert_w_down, shared_w_gate, shared_w_up, shared_w_down):
    bp, tp, d = x_prompt.shape
    bs, ts, _ = x_sample.shape
    depth = w_mod.shape[0]
    assert d == D_MODEL and tp == TM and ts % TM == 0 and bs + 1 <= 8
    tiles = _Tiles(bp, bs, ts)
    n_tok = tiles.n_tiles * TM
    n_slots = n_tok * TOP_K + SUB
    cos_t, sin_t = _rope_tables(ts, tiles.lat_tiles)

    perm = (jnp.arange(N_EXPERTS) % N_GROUPS) * GROUP_SIZE + jnp.arange(N_EXPERTS) // N_GROUPS

    xp2 = x_prompt.reshape(bp * tp, d)
    xs2 = x_sample.reshape(bs * ts, d)
    new_f, new_b = [], []
    for l in range(depth):
        c_rows = jnp.concatenate([c_ctx[None, :], c, jnp.zeros((8 - 1 - bs, d), F32)], axis=0)
        mod3 = _modulation(c_rows, w_mod[l], b_mod[l][None, :]).reshape(8, 6, d)
        dec = jnp.broadcast_to(jnp.stack([ret_decay_fwd[l], ret_decay_bwd[l]])[:, :, None, None],
                               (2, HEADS, DK, DK)).astype(F32)
        xnew, tok, tok_tiles, sf_fin, sb_fin = _token_mixer(
            tiles, xp2, xs2, mod3, norm_mix_pre[l][None, :], w_in[l].astype(BF16), conv_w[l],
            conv_b[l][None, :], dec, cos_t, sin_t, state_ret_fwd[:, l], state_ret_bwd[:, l],
            w_out[l].astype(BF16), norm_mix_post[l][None, :], norm_ffn_pre[l][None, :], ret_gn_g[l][None, :])
        new_f.append(sf_fin)
        new_b.append(sb_fin)

        rwt = router_w[l].T[perm].astype(BF16)
        bias_b = jnp.broadcast_to(router_bias[l][perm][:, None], (N_EXPERTS, 128)).astype(F32)
        topi, topw = _route(tok, rwt, bias_b)
        dest, start, counts = _dispatch_plan(topi)
        dest_flat = dest.reshape(-1)
        start_i = start[:, 0].astype(I32)
        nsub = ((counts[:, 0] + float(SUB - 1)) / float(SUB)).astype(I32)
        xs_tiles = _dispatch(dest_flat, tok_tiles, n_tok, n_slots)
        ysorted = _experts(start_i, nsub, xs_tiles, expert_w_gate[l], expert_w_up[l], expert_w_down[l])
        xp2, xs2 = _combine(tiles, dest_flat, ysorted, topw, tok, xnew, mod3,
                            shared_w_gate[l].astype(BF16), shared_w_up[l].astype(BF16),
                            shared_w_down[l].astype(BF16), norm_ffn_post[l][None, :])

    return (xp2.reshape(bp, tp, d), xs2.reshape(bs, ts, d),
            jnp.stack(new_f, axis=1), jnp.stack(new_b, axis=1))
```

```python
import functools

import numpy as np
import jax
import jax.numpy as jnp
from jax import lax
from jax.experimental import pallas as pl
from jax.experimental.pallas import tpu as pltpu

F32 = jnp.float32
BF16 = jnp.bfloat16
I32 = jnp.int32

D_MODEL = 1024
CONV_W = 512
RET_W = 512
HEADS = 4
DK = 128
CHUNK = 128
GRID_W = 64
ROPE_FREQS = 32
ROPE_BASE = 10000.0
IN_COLS = 3 * CONV_W + 4 * RET_W
N_EXPERTS = 256
N_GROUPS = 8
GROUP_SIZE = N_EXPERTS // N_GROUPS
TOPK_GROUPS = 4
TOP_K = 8
FF = 256
ROUTED_SCALE = 2.5
EPS = 1e-6

TM = 256
SUB = 512
TT = 128
RT = 512
RB = 2048
VMEM_LIMIT = 56 * 1024 * 1024


def _cparams(n_axes=1, vmem=VMEM_LIMIT):
    return pltpu.CompilerParams(dimension_semantics=("arbitrary",) * n_axes,
                                vmem_limit_bytes=vmem)


def _silu(x):
    return x * jax.nn.sigmoid(x)


def _log_sigmoid(x):
    return jnp.minimum(x, 0.0) - jnp.log1p(jnp.exp(-jnp.abs(x)))


def _rms(x, g):
    return x * lax.rsqrt(jnp.mean(x * x, axis=-1, keepdims=True) + EPS) * g


def _dot(a, b):
    return jnp.dot(a, b, preferred_element_type=F32)


def _mod_body(c_ref, w_ref, b_ref, o_ref):
    s = _silu(c_ref[...]).astype(BF16)
    o_ref[...] = _dot(s, w_ref[...].astype(BF16)) + b_ref[...]


def _modulation(c_rows, w_mod, b_mod):
    n_col = w_mod.shape[1]
    blk = 1536
    return pl.pallas_call(
        _mod_body,
        out_shape=jax.ShapeDtypeStruct((8, n_col), F32),
        grid=(n_col // blk,),
        in_specs=[pl.BlockSpec((8, D_MODEL), lambda i: (0, 0)),
                  pl.BlockSpec((D_MODEL, blk), lambda i: (0, i)),
                  pl.BlockSpec((1, blk), lambda i: (0, i))],
        out_specs=pl.BlockSpec((8, blk), lambda i: (0, i)),
        compiler_params=_cparams(),
        name="mod",
    )(c_rows, w_mod, b_mod)


class _Tiles:
    def __init__(self, n_ctx_seq, n_lat_seq, lat_len):
        self.n_ctx = n_ctx_seq
        self.lat_tiles = lat_len // TM
        self.n_lat_seq = n_lat_seq
        self.n_tiles = n_ctx_seq + n_lat_seq * self.lat_tiles

    def is_ctx(self, i):
        return i < self.n_ctx

    def lat_pos(self, i):
        j = jnp.maximum(i - self.n_ctx, 0)
        return j // self.lat_tiles, j % self.lat_tiles

    def phys_reversed(self, i):
        b, t = self.lat_pos(i)
        return jnp.where(i < self.n_ctx, i, self.n_ctx + b * self.lat_tiles + (self.lat_tiles - 1 - t))

    def mod_row(self, i):
        b, _ = self.lat_pos(i)
        return jnp.where(i < self.n_ctx, 0, 1 + b)


def _rope(x, cos, sin_signed):
    lane = lax.broadcasted_iota(I32, x.shape, 1)
    partner = jnp.where((lane & 63) < 32, pltpu.roll(x, 96, 1), pltpu.roll(x, 32, 1))
    return x * cos + partner * sin_signed


def _mix_a_body(tiles, xp_ref, xs_ref, mod_ref, gpre_ref, win_ref, cw_ref, cb_ref, dec_ref,
                cos_ref, sin_ref, s0b_ref,
                yconv_ref, q_ref, v_ref, g_ref, kt_ref, sbin_ref, sbfin_ref,
                sb_scr, tab_scr):
    i = pl.program_id(0)
    is_ctx = tiles.is_ctx(i)
    _, t_rev = tiles.lat_pos(i)
    first = jnp.logical_or(is_ctx, t_rev == 0)

    @pl.when(i == 0)
    def _():
        lg = _log_sigmoid(dec_ref[1])
        col = lax.broadcasted_iota(I32, lg.shape, 2).astype(F32)
        tab_scr[0] = jnp.exp(col * lg)
        tab_scr[1] = jnp.exp(float(CHUNK) * lg)

    @pl.when(first)
    def _():
        sb_scr[...] = jnp.where(is_ctx, 0.0, s0b_ref[0])

    x = jnp.where(is_ctx, xp_ref[...], xs_ref[...])
    h = (_rms(x, gpre_ref[...]) * (1.0 + mod_ref[0, 1:2, :]) + mod_ref[0, 0:1, :]).astype(BF16)

    def proj(k):
        return _dot(h, win_ref[:, k * 512:(k + 1) * 512])

    z = proj(1) * proj(2)
    row = lax.broadcasted_iota(I32, z.shape, 0)
    period = jnp.where(is_ctx, TM, GRID_W)
    pos = row & (period - 1)
    left = jnp.where(pos == 0, 0.0, pltpu.roll(z, 1, 0))
    right = jnp.where(pos == period - 1, 0.0, pltpu.roll(z, TM - 1, 0))
    zc = left * cw_ref[0:1, :] + z * cw_ref[1:2, :] + right * cw_ref[2:3, :] + cb_ref[...]
    yconv_ref[...] = (proj(0) * zc).astype(BF16)

    cos = cos_ref[0]
    sin = sin_ref[0]
    q = proj(3)
    k = proj(4)
    q = jnp.concatenate([_rope(q[:, hh * DK:(hh + 1) * DK], cos, sin) for hh in range(HEADS)], axis=1)
    k = jnp.concatenate([_rope(k[:, hh * DK:(hh + 1) * DK], cos, sin) for hh in range(HEADS)], axis=1)
    q_ref[...] = (q * (DK ** -0.5)).astype(BF16)
    kt = k.T
    kt_ref[...] = kt.astype(BF16)
    v = proj(5).astype(BF16)
    v_ref[...] = v
    g_ref[...] = proj(6)

    for c in (1, 0):
        for hh in range(HEADS):
            sbin_ref[c, hh] = sb_scr[hh].astype(BF16)
            kts = (kt[hh * DK:(hh + 1) * DK, c * CHUNK:(c + 1) * CHUNK] * tab_scr[0, hh]).astype(BF16)
            vc = v[c * CHUNK:(c + 1) * CHUNK, hh * DK:(hh + 1) * DK]
            sb_scr[hh] = sb_scr[hh] * tab_scr[1, hh] + _dot(kts, vc)

    @pl.when(is_ctx)
    def _():
        sbfin_ref[0] = sb_scr[...]


def _mix_b_body(tiles, xp_ref, xs_ref, mod_ref, q_ref, kt_ref, v_ref, g_ref, yconv_ref, sbin_ref,
                wout_ref, gpost_ref, gffn_ref, gn_ref, dec_ref, s0f_ref,
                xnew_ref, tok_ref, toktiles_ref, sffin_ref,
                sf_scr, tab_scr, ycat_scr):
    i = pl.program_id(0)
    is_ctx = tiles.is_ctx(i)
    _, t_pos = tiles.lat_pos(i)
    first = jnp.logical_or(is_ctx, t_pos == 0)

    @pl.when(i == 0)
    def _():
        lgf = _log_sigmoid(dec_ref[0])
        lgb = _log_sigmoid(dec_ref[1])
        row = lax.broadcasted_iota(I32, lgf.shape, 1)
        col = lax.broadcasted_iota(I32, lgf.shape, 2)
        d = (row - col).astype(F32)
        tab_scr[0] = (jnp.where(row >= col, jnp.exp(jnp.where(row >= col, d, 0.0) * lgf), 0.0)
                      + jnp.where(col >= row, jnp.exp(jnp.where(col >= row, -d, 0.0) * lgb), 0.0))
        tab_scr[1] = jnp.exp((row + 1).astype(F32) * lgf)
        tab_scr[2] = jnp.exp((CHUNK - row).astype(F32) * lgb)
        tab_scr[3] = jnp.exp((CHUNK - 1 - col).astype(F32) * lgf)
        tab_scr[4] = jnp.exp(float(CHUNK) * lgf)

    @pl.when(first)
    def _():
        sf_scr[...] = jnp.where(is_ctx, 0.0, s0f_ref[0])

    for c in range(TM // CHUNK):
        rows = slice(c * CHUNK, (c + 1) * CHUNK)
        for hh in range(HEADS):
            cols = slice(hh * DK, (hh + 1) * DK)
            qc = q_ref[rows, cols]
            ktc = kt_ref[cols, rows]
            vc = v_ref[rows, cols]
            att = (_dot(qc, ktc) * tab_scr[0, hh]).astype(BF16)
            o = (_dot(att, vc)
                 + tab_scr[1, hh] * _dot(qc, sf_scr[hh].astype(BF16))
                 + tab_scr[2, hh] * _dot(qc, sbin_ref[c, hh]))
            kts = (ktc.astype(F32) * tab_scr[3, hh]).astype(BF16)
            sf_scr[hh] = sf_scr[hh] * tab_scr[4, hh] + _dot(kts, vc)
            mu = jnp.mean(o, axis=-1, keepdims=True)
            dev = o - mu
            var = jnp.mean(dev * dev, axis=-1, keepdims=True)
            on = dev * lax.rsqrt(var + EPS) * gn_ref[:, cols]
            ycat_scr[rows, RET_W + hh * DK:RET_W + (hh + 1) * DK] = (_silu(g_ref[rows, cols]) * on).astype(BF16)
    ycat_scr[:, 0:CONV_W] = yconv_ref[...]

    @pl.when(is_ctx)
    def _():
        sffin_ref[0] = sf_scr[...]

    x = jnp.where(is_ctx, xp_ref[...], xs_ref[...])
    u = _dot(ycat_scr[...], wout_ref[...])
    xn = x + mod_ref[0, 2:3, :] * _rms(u, gpost_ref[...])
    xnew_ref[...] = xn
    tok = _rms(xn, gffn_ref[...]) * (1.0 + mod_ref[0, 4:5, :]) + mod_ref[0, 3:4, :]
    tok_ref[...] = tok.astype(BF16)
    toktiles_ref[...] = _halftiles_from_rows(_pack_rows(tok))


def _token_mixer(tiles, xp2, xs2, mod3, g_pre, win_bf, conv_w, conv_b, dec, cos_t, sin_t,
                 s0f, s0b, wout_bf, g_post, g_ffn, gn_g):
    n_tok = tiles.n_tiles * TM
    n_ctx = tiles.n_ctx
    last_ctx = n_ctx - 1

    def full(shape):
        return pl.BlockSpec(shape, lambda i: (0,) * len(shape))

    def xp_spec(phys):
        return pl.BlockSpec((TM, D_MODEL), lambda i: (jnp.minimum(phys(i), last_ctx), 0))

    def xs_spec(phys):
        return pl.BlockSpec((TM, D_MODEL), lambda i: (jnp.maximum(phys(i) - n_ctx, 0), 0))

    mod_spec = pl.BlockSpec((1, 6, D_MODEL), lambda i: (tiles.mod_row(i), 0, 0))
    state_in = pl.BlockSpec((1, HEADS, DK, DK), lambda i: (tiles.lat_pos(i)[0], 0, 0, 0))
    state_out = pl.BlockSpec((1, HEADS, DK, DK), lambda i: (jnp.minimum(i, last_ctx), 0, 0, 0))

    rev = tiles.phys_reversed

    def rope_idx(i):
        _, t = tiles.lat_pos(i)
        return jnp.where(i < n_ctx, 0, 1 + (tiles.lat_tiles - 1 - t))

    rope_spec = pl.BlockSpec((1, TM, DK), lambda i: (rope_idx(i), 0, 0))

    def rows(width, phys):
        return pl.BlockSpec((TM, width), lambda i: (phys(i), 0))

    yconv, q, v, g, kt, sbin, sb_fin = pl.pallas_call(
        functools.partial(_mix_a_body, tiles),
        out_shape=(jax.ShapeDtypeStruct((n_tok, CONV_W), BF16),
                   jax.ShapeDtypeStruct((n_tok, RET_W), BF16),
                   jax.ShapeDtypeStruct((n_tok, RET_W), BF16),
                   jax.ShapeDtypeStruct((n_tok, RET_W), F32),
                   jax.ShapeDtypeStruct((RET_W, n_tok), BF16),
                   jax.ShapeDtypeStruct((n_tok // CHUNK, HEADS, DK, DK), BF16),
                   jax.ShapeDtypeStruct((n_ctx, HEADS, DK, DK), F32)),
        grid=(tiles.n_tiles,),
        in_specs=[xp_spec(rev), xs_spec(rev), mod_spec, full((1, D_MODEL)), full((D_MODEL, IN_COLS)),
                  full((3, CONV_W)), full((1, CONV_W)), full((2, HEADS, DK, DK)),
                  rope_spec, rope_spec, state_in],
        out_specs=(rows(CONV_W, rev), rows(RET_W, rev), rows(RET_W, rev), rows(RET_W, rev),
                   pl.BlockSpec((RET_W, TM), lambda i: (0, rev(i))),
                   pl.BlockSpec((TM // CHUNK, HEADS, DK, DK), lambda i: (rev(i), 0, 0, 0)),
                   state_out),
        scratch_shapes=[pltpu.VMEM((HEADS, DK, DK), F32), pltpu.VMEM((2, HEADS, DK, DK), F32)],
        compiler_params=_cparams(),
        name="mix_a",
    )(xp2, xs2, mod3, g_pre, win_bf, conv_w, conv_b, dec, cos_t, sin_t, s0b)

    ident = lambda i: i
    xnew, tok, tok_tiles, sf_fin = pl.pallas_call(
        functools.partial(_mix_b_body, tiles),
        out_shape=(jax.ShapeDtypeStruct((n_tok, D_MODEL), F32),
                   jax.ShapeDtypeStruct((n_tok, D_MODEL), BF16),
                   jax.ShapeDtypeStruct((n_tok * HALF_TILE, 128), U32),
                   jax.ShapeDtypeStruct((n_ctx, HEADS, DK, DK), F32)),
        grid=(tiles.n_tiles,),
        in_specs=[xp_spec(ident), xs_spec(ident), mod_spec,
                  rows(RET_W, ident),
                  pl.BlockSpec((RET_W, TM), lambda i: (0, i)),
                  rows(RET_W, ident), rows(RET_W, ident), rows(CONV_W, ident),
                  pl.BlockSpec((TM // CHUNK, HEADS, DK, DK), lambda i: (i, 0, 0, 0)),
                  full((D_MODEL, D_MODEL)), full((1, D_MODEL)), full((1, D_MODEL)), full((1, RET_W)),
                  full((2, HEADS, DK, DK)), state_in],
        out_specs=(rows(D_MODEL, ident), rows(D_MODEL, ident),
                   pl.BlockSpec((TM * HALF_TILE, 128), lambda i: (i, 0)), state_out),
        scratch_shapes=[pltpu.VMEM((HEADS, DK, DK), F32), pltpu.VMEM((5, HEADS, DK, DK), F32),
                        pltpu.VMEM((TM, D_MODEL), BF16)],
        compiler_params=_cparams(),
        name="mix_b",
    )(xp2, xs2, mod3, q, kt, v, g, yconv, sbin, wout_bf, g_post, g_ffn, gn_g, dec, s0f)
    return xnew, tok, tok_tiles, sf_fin, sb_fin


def _route_body(tok_ref, rwt_ref, bias_ref, topi_ref, topw_ref):
    h = tok_ref[...]
    logits = lax.dot_general(rwt_ref[...], h, (((1,), (1,)), ((), ())), preferred_element_type=F32)
    shape3 = (GROUP_SIZE, N_GROUPS, 128)
    member = lax.broadcasted_iota(I32, shape3, 0)
    group = lax.broadcasted_iota(I32, shape3, 1)
    expert = group * GROUP_SIZE + member
    group2 = lax.broadcasted_iota(I32, (N_GROUPS, 128), 0)
    neg = -jnp.inf
    for lb in range(RT // 128):
        scores = jax.nn.sigmoid(logits[:, lb * 128:(lb + 1) * 128]).reshape(shape3)
        biased = scores + bias_ref[...].reshape(shape3)
        m1 = jnp.max(biased, axis=0)
        first = jnp.min(jnp.where(biased == m1, member, GROUP_SIZE), axis=0)
        m2 = jnp.max(jnp.where(member == first, neg, biased), axis=0)
        gs = m1 + m2
        beaten = jnp.zeros(gs.shape, I32)
        for s in range(1, N_GROUPS):
            other = pltpu.roll(gs, s, 0)
            wins = (other > gs) | ((other == gs) & (group2 >= s))
            beaten = beaten + wins.astype(I32)
        keep = beaten < TOPK_GROUPS
        cand = jnp.where(keep, biased, neg)
        idx_rows, w_rows = [], []
        for _ in range(TOP_K):
            best = jnp.max(jnp.max(cand, axis=0), axis=0, keepdims=True)
            pick = jnp.min(jnp.min(jnp.where(cand == best, expert, N_EXPERTS), axis=0), axis=0, keepdims=True)
            hit = expert == pick
            w_rows.append(jnp.sum(jnp.sum(jnp.where(hit, scores, 0.0), axis=0), axis=0, keepdims=True))
            idx_rows.append(pick)
            cand = jnp.where(hit, neg, cand)
        w = jnp.concatenate(w_rows, axis=0)
        topi_ref[:, lb * 128:(lb + 1) * 128] = jnp.concatenate(idx_rows, axis=0)
        topw_ref[:, lb * 128:(lb + 1) * 128] = w / jnp.sum(w, axis=0, keepdims=True) * ROUTED_SCALE


def _route(tok, rwt_bf, bias_b):
    n_tok = tok.shape[0]
    return pl.pallas_call(
        _route_body,
        out_shape=(jax.ShapeDtypeStruct((TOP_K, n_tok), I32), jax.ShapeDtypeStruct((TOP_K, n_tok), F32)),
        grid=(n_tok // RT,),
        in_specs=[pl.BlockSpec((RT, D_MODEL), lambda i: (i, 0)),
                  pl.BlockSpec((N_EXPERTS, D_MODEL), lambda i: (0, 0)),
                  pl.BlockSpec((N_EXPERTS, 128), lambda i: (0, 0))],
        out_specs=(pl.BlockSpec((TOP_K, RT), lambda i: (0, i)), pl.BlockSpec((TOP_K, RT), lambda i: (0, i))),
        compiler_params=_cparams(),
        name="route",
    )(tok, rwt_bf, bias_b)


def _onehot(ids_row):
    e = lax.broadcasted_iota(I32, (N_EXPERTS, 256), 0)
    return e == ids_row


def _rank_body(topi_ref, rank_ref, counts_ref, run_scr):
    i = pl.program_id(0)

    @pl.when(i == 0)
    def _():
        run_scr[...] = jnp.zeros(run_scr.shape, F32)

    a0 = lax.broadcasted_iota(I32, (256, 256), 0)
    a1 = lax.broadcasted_iota(I32, (256, 256), 1)
    upper = (a0 <= a1).astype(BF16)
    ones = jnp.ones((256, 256), BF16)
    for k in range(TOP_K):
        for sb in range(RB // 256):
            lanes = slice(sb * 256, (sb + 1) * 256)
            oh = _onehot(topi_ref[k:k + 1, lanes])
            ohb = oh.astype(BF16)
            seen = _dot(ohb, upper) + run_scr[...]
            r = jnp.sum(jnp.where(oh, seen, 0.0), axis=0, keepdims=True) - 1.0
            rank_ref[k:k + 1, lanes] = r.astype(I32)
            run_scr[...] = run_scr[...] + _dot(ohb, ones)

    @pl.when(i == pl.num_programs(0) - 1)
    def _():
        counts_ref[...] = run_scr[:, 0:128]


def _dest_body(topi_ref, rank_ref, counts_ref, dest_ref, start_ref, start_scr):
    i = pl.program_id(0)

    @pl.when(i == 0)
    def _():
        c = counts_ref[...]
        d2 = jnp.floor(c / 16384.0)
        rem = c - d2 * 16384.0
        d1 = jnp.floor(rem / 128.0)
        d0 = rem - d1 * 128.0
        e0 = lax.broadcasted_iota(I32, (N_EXPERTS, N_EXPERTS), 0)
        e1 = lax.broadcasted_iota(I32, (N_EXPERTS, N_EXPERTS), 1)
        below = (e1 < e0).astype(BF16)
        start_scr[...] = (16384.0 * _dot(below, d2.astype(BF16)) + 128.0 * _dot(below, d1.astype(BF16))
                          + _dot(below, d0.astype(BF16)))
        start_ref[...] = start_scr[...]

    start = jnp.concatenate([start_scr[...], start_scr[...]], axis=1)
    for k in range(TOP_K):
        for sb in range(RB // 256):
            lanes = slice(sb * 256, (sb + 1) * 256)
            oh = _onehot(topi_ref[k:k + 1, lanes])
            base = jnp.sum(jnp.where(oh, start, 0.0), axis=0, keepdims=True)
            dest_ref[k:k + 1, lanes] = base.astype(I32) + rank_ref[k:k + 1, lanes]


def _dispatch_plan(topi):
    n_tok = topi.shape[1]
    blk = pl.BlockSpec((TOP_K, RB), lambda i: (0, i))
    whole = pl.BlockSpec((N_EXPERTS, 128), lambda i: (0, 0))
    rank, counts = pl.pallas_call(
        _rank_body,
        out_shape=(jax.ShapeDtypeStruct((TOP_K, n_tok), I32), jax.ShapeDtypeStruct((N_EXPERTS, 128), F32)),
        grid=(n_tok // RB,),
        in_specs=[blk],
        out_specs=(blk, whole),
        scratch_shapes=[pltpu.VMEM((N_EXPERTS, 256), F32)],
        compiler_params=_cparams(),
        name="rank",
    )(topi)
    dest, start = pl.pallas_call(
        _dest_body,
        out_shape=(jax.ShapeDtypeStruct((TOP_K, n_tok), I32), jax.ShapeDtypeStruct((N_EXPERTS, 128), F32)),
        grid=(n_tok // RB,),
        in_specs=[blk, blk, whole],
        out_specs=(blk, whole),
        scratch_shapes=[pltpu.VMEM((N_EXPERTS, 128), F32)],
        compiler_params=_cparams(),
        name="dest",
    )(topi, rank, counts)
    return dest, start, counts


ROW_TILES = D_MODEL // 128


def _transpose8(vs):
    sub = lax.broadcasted_iota(I32, (8, 128), 0)
    for d in (4, 2, 1):
        keep = (sub & d) == 0
        out = list(vs)
        for i in range(8):
            if i & d == 0:
                a, b = vs[i], vs[i + d]
                out[i] = jnp.where(keep, a, pltpu.roll(b, d, 0))
                out[i + d] = jnp.where(keep, pltpu.roll(a, 8 - d, 0), b)
        vs = out
    return vs


def _rows_from_tiles(tiles):
    n_rows = tiles.shape[0] // ROW_TILES
    groups = [_transpose8([tiles[(g * 8 + r) * ROW_TILES:(g * 8 + r + 1) * ROW_TILES] for r in range(8)])
              for g in range(n_rows // 8)]
    return jnp.concatenate([jnp.concatenate([grp[c] for grp in groups], axis=0) for c in range(ROW_TILES)], axis=1)


def _rows_to_tiles(value, row_of_tile=lambda t: t):
    n_rows = value.shape[0]
    pieces = []
    for g in range(n_rows // 8):
        pieces += _transpose8([value[g * 8:(g + 1) * 8, c * 128:(c + 1) * 128] for c in range(ROW_TILES)])
    return jnp.concatenate([pieces[row_of_tile(t)] for t in range(n_rows)], axis=0)


U32 = jnp.uint32
HALF_TILE = 4
HIGH_HALF = 0xFFFF0000


def _pack_rows(x):
    half = D_MODEL // 2
    lo = pltpu.bitcast(x[:, :half].astype(BF16).astype(F32), U32)
    hi = pltpu.bitcast(x[:, half:].astype(BF16).astype(F32), U32)
    return (lo >> 16) | (hi & jnp.uint32(HIGH_HALF))


def _unpack_rows(u):
    lo = pltpu.bitcast(u << 16, F32)
    hi = pltpu.bitcast(u & jnp.uint32(HIGH_HALF), F32)
    return jnp.concatenate([lo, hi], axis=1).astype(BF16)


def _halftiles_from_rows(u):
    pieces = []
    for q in range(u.shape[0] // 16):
        pieces += _transpose8([u[q * 16 + h * 8:q * 16 + h * 8 + 8, c * 128:(c + 1) * 128]
                               for h in range(2) for c in range(HALF_TILE)])
    return jnp.concatenate(pieces, axis=0)


def _halftile_row(r):
    q, j = divmod(r, 16)
    return (q * 8 + j % 8) * 8 + (j // 8) * HALF_TILE


def _rows_from_halftiles(t):
    blocks = []
    for q in range(t.shape[0] // 64):
        outs = _transpose8([t[(q * 8 + p) * 8:(q * 8 + p + 1) * 8] for p in range(8)])
        for h in range(2):
            blocks.append(jnp.concatenate(outs[h * HALF_TILE:(h + 1) * HALF_TILE], axis=1))
    return jnp.concatenate(blocks, axis=0)


def _row_of_linear_slot(s):
    q, m = divmod(s, 16)
    return q * 16 + (m // 2 if m % 2 == 0 else 8 + m // 2)


def _start_row_gather(src_hbm, idx_of, dst, sem, n_rows, priority_of=lambda r: r % 2):
    for r in range(n_rows):
        src_row = pl.multiple_of(idx_of(r) * ROW_TILES, ROW_TILES)
        pltpu.make_async_copy(src_hbm.at[pl.ds(src_row, ROW_TILES)], dst.at[pl.ds(r * ROW_TILES, ROW_TILES)],
                              sem).start(priority=priority_of(r))


DISPATCH_ROWS = 128
DISPATCH_DEPTH = 4


def _dispatch_body(n_tok, dest_ref, tok_hbm, xs_hbm, tbuf, zbuf, sem, fsem, zsem):
    i = pl.program_id(0)
    k = pl.program_id(1)
    n_blk = pl.num_programs(0)
    step = i * TOP_K + k
    n_steps = n_blk * TOP_K
    unit = tbuf.shape[1] // DISPATCH_ROWS
    blk_rows = DISPATCH_ROWS * unit
    buf = i & 1

    def fetch(blk, b):
        return pltpu.make_async_copy(tok_hbm.at[pl.ds(pl.multiple_of(blk * blk_rows, blk_rows), blk_rows)],
                                     tbuf.at[b], fsem.at[b])

    def batch_wait(b):
        pltpu.make_async_copy(tbuf.at[b], xs_hbm.at[pl.ds(0, blk_rows)], sem.at[b]).wait()

    @pl.when(step == 0)
    def _():
        fetch(0, 0).start()
        zbuf[...] = jnp.zeros(zbuf.shape, zbuf.dtype)
        tail = pltpu.make_async_copy(zbuf, xs_hbm.at[pl.ds(xs_hbm.shape[0] - zbuf.shape[0], zbuf.shape[0])], zsem)
        tail.start()
        tail.wait()

    @pl.when(k == 0)
    def _():
        fetch(i, buf).wait()

    @pl.when(jnp.logical_and(i > 0, k < DISPATCH_DEPTH))
    def _():
        for _ in range(TOP_K // DISPATCH_DEPTH):
            batch_wait(1 - buf)

    @pl.when(jnp.logical_and(k == DISPATCH_DEPTH, i + 1 < n_blk))
    def _():
        fetch(i + 1, 1 - buf).start()

    a0 = k * n_tok + i * DISPATCH_ROWS

    def scatter(b):
        for r in range(DISPATCH_ROWS):
            dst_row = pl.multiple_of(dest_ref[a0 + r] * unit, unit)
            pltpu.make_async_copy(tbuf.at[b, pl.ds(_halftile_row(r), unit)], xs_hbm.at[pl.ds(dst_row, unit)],
                                  sem.at[b]).start(priority=r % 2)

    for b in range(2):
        pl.when(buf == b)(functools.partial(scatter, b))

    @pl.when(step == n_steps - 1)
    def _():
        for _ in range(TOP_K):
            batch_wait(buf)


def _dispatch(dest_flat, tok_rows, n_tok, n_rows_out):
    assert DISPATCH_DEPTH < TOP_K and TOP_K % DISPATCH_DEPTH == 0
    unit = tok_rows.shape[0] // n_tok
    width = tok_rows.shape[1]
    assert unit == HALF_TILE and DISPATCH_ROWS % 16 == 0
    dtype = tok_rows.dtype
    return pl.pallas_call(
        functools.partial(_dispatch_body, n_tok),
        out_shape=jax.ShapeDtypeStruct((n_rows_out * unit, width), dtype),
        grid_spec=pltpu.PrefetchScalarGridSpec(
            num_scalar_prefetch=1,
            grid=(n_tok // DISPATCH_ROWS, TOP_K),
            in_specs=[pl.BlockSpec(memory_space=pl.ANY)],
            out_specs=pl.BlockSpec(memory_space=pl.ANY),
            scratch_shapes=[pltpu.VMEM((2, DISPATCH_ROWS * unit, width), dtype),
                            pltpu.VMEM((SUB * unit, width), dtype),
                            pltpu.SemaphoreType.DMA((2,)), pltpu.SemaphoreType.DMA((2,)),
                            pltpu.SemaphoreType.DMA(())]),
        compiler_params=_cparams(2),
        name="dispatch",
    )(dest_flat, tok_rows)


N_XBUF = 4
LOOKAHEAD = 3
W_AHEAD = 2
WEIGHT_PRIORITY = 1


def _experts_body(start_ref, nsub_ref,
                  xs_hbm, wg_hbm, wu_hbm, wd_hbm, y_hbm,
                  xbuf, ybuf0, ybuf1,
                  wg_f32, wu_f32, wd_f32, wg_bf, wu_bf, wd_bf, cur, nxt, gsem, osem, wsem):
    e = pl.program_id(0)
    n_e = pl.num_programs(0)
    nsub = nsub_ref[e]
    sub_rows = SUB * ROW_TILES
    in_rows = SUB * HALF_TILE
    overflow_row = y_hbm.shape[0] - sub_rows

    def weight_copies(ex, slot):
        return [pltpu.make_async_copy(src.at[ex], dst.at[slot], wsem.at[slot, n])
                for n, (src, dst) in enumerate(((wg_hbm, wg_f32), (wu_hbm, wu_f32), (wd_hbm, wd_f32)))]

    def window_row(ex, j, rows_per_slot):
        return pl.multiple_of((start_ref[ex] + j * SUB) * rows_per_slot, rows_per_slot)


    def produce():
        pe = cur[0]
        pj = cur[1]
        pg = cur[2]
        live = pe < n_e
        pe_c = jnp.minimum(pe, n_e - 1)
        row = pl.multiple_of(jnp.where(live, window_row(pe_c, pj, HALF_TILE), 0), HALF_TILE)
        slot = lax.rem(pg, N_XBUF)
        pltpu.make_async_copy(xs_hbm.at[pl.ds(row, in_rows)], xbuf.at[slot], gsem.at[slot]).start()
        last = pj + 1 >= nsub_ref[pe_c]
        cur[0] = jnp.where(jnp.logical_and(live, last), nxt[pe_c], pe)
        cur[1] = jnp.where(last, 0, pj + 1)
        cur[2] = pg + 1

    def fetch_wait(slot):
        pltpu.make_async_copy(xs_hbm.at[pl.ds(0, in_rows)], xbuf.at[slot], gsem.at[slot]).wait()

    ybuf = (ybuf0, ybuf1)

    def out_wait():
        pltpu.make_async_copy(ybuf0, y_hbm.at[pl.ds(0, sub_rows)], osem).wait()

    @pl.when(e == 0)
    def _():
        def fill(i, following):
            x = N_EXPERTS - 1 - i
            nxt[x] = following
            return jnp.where(nsub_ref[x] > 0, x, following)

        cur[0] = lax.fori_loop(0, N_EXPERTS, fill, N_EXPERTS)
        cur[1] = 0
        cur[2] = 0
        cur[3] = 0
        for ahead in range(W_AHEAD):
            for cp in weight_copies(ahead, ahead):
                cp.start(priority=WEIGHT_PRIORITY)
        for _ in range(LOOKAHEAD):
            produce()
        ybuf1[...] = jnp.zeros(ybuf1.shape, F32)
        pltpu.make_async_copy(ybuf1, y_hbm.at[pl.ds(overflow_row, sub_rows)], osem).start()

    wslot = lax.rem(e, W_AHEAD + 1)

    @pl.when(e + W_AHEAD < n_e)
    def _():
        for cp in weight_copies(e + W_AHEAD, lax.rem(e + W_AHEAD, W_AHEAD + 1)):
            cp.start(priority=WEIGHT_PRIORITY)

    for cp in weight_copies(e, wslot):
        cp.wait()
    wg_bf[...] = wg_f32[wslot].astype(BF16)
    wu_bf[...] = wu_f32[wslot].astype(BF16)
    wd_bf[...] = wd_f32[wslot].astype(BF16)

    def step(j, carry):
        g = cur[3]
        produce()
        slot = lax.rem(g, N_XBUF)
        fetch_wait(slot)
        xb = _unpack_rows(_rows_from_halftiles(xbuf[slot]))
        a = _dot(xb, wg_bf[...])
        b = _dot(xb, wu_bf[...])
        y_tiles = _rows_to_tiles(_dot((_silu(a) * b).astype(BF16), wd_bf[...]), _row_of_linear_slot)
        row = window_row(e, j, ROW_TILES)
        for parity in range(2):
            @pl.when((g & 1) == parity)
            def _():
                ybuf[parity][...] = y_tiles
                out_wait()
                pltpu.make_async_copy(ybuf[parity], y_hbm.at[pl.ds(row, sub_rows)], osem).start()
        cur[3] = g + 1
        return carry

    lax.fori_loop(0, nsub, step, 0)

    @pl.when(e == n_e - 1)
    def _():
        total = cur[3]
        out_wait()
        for ahead in range(LOOKAHEAD):
            fetch_wait(lax.rem(total + ahead, N_XBUF))
        ybuf0[...] = jnp.zeros(ybuf0.shape, F32)
        tail = pltpu.make_async_copy(ybuf0, y_hbm.at[pl.ds(overflow_row, sub_rows)], osem)
        tail.start()
        tail.wait()


def _experts(start, nsub, xs_tiles, wg, wu, wd):
    sub_rows = SUB * ROW_TILES
    n_slots = xs_tiles.shape[0] // HALF_TILE
    return pl.pallas_call(
        _experts_body,
        out_shape=jax.ShapeDtypeStruct((n_slots * ROW_TILES, 128), F32),
        grid_spec=pltpu.PrefetchScalarGridSpec(
            num_scalar_prefetch=2,
            grid=(N_EXPERTS,),
            in_specs=[pl.BlockSpec(memory_space=pl.ANY)] * 4,
            out_specs=pl.BlockSpec(memory_space=pl.ANY),
            scratch_shapes=[pltpu.VMEM((N_XBUF, SUB * HALF_TILE, 128), U32),
                            pltpu.VMEM((sub_rows, 128), F32), pltpu.VMEM((sub_rows, 128), F32),
                            pltpu.VMEM((W_AHEAD + 1, D_MODEL, FF), F32), pltpu.VMEM((W_AHEAD + 1, D_MODEL, FF), F32),
                            pltpu.VMEM((W_AHEAD + 1, FF, D_MODEL), F32),
                            pltpu.VMEM((D_MODEL, FF), BF16), pltpu.VMEM((D_MODEL, FF), BF16),
                            pltpu.VMEM((FF, D_MODEL), BF16), pltpu.SMEM((4,), I32), pltpu.SMEM((N_EXPERTS,), I32),
                            pltpu.SemaphoreType.DMA((N_XBUF,)), pltpu.SemaphoreType.DMA(()),
                            pltpu.SemaphoreType.DMA((W_AHEAD + 1, 3))]),
        compiler_params=_cparams(),
        name="experts",
    )(start, nsub, xs_tiles, wg, wu, wd)


def _combine_body(n_tok, n_ctx_tiles, dest_ref,
                  y_hbm, topw_ref, tok_ref, xnew_ref, mod_ref, sg_ref, su_ref, sd_ref, gpost_ref,
                  outp_ref, outs_ref, gbuf, wcol, fbuf, gsem):
    i = pl.program_id(0)
    n = pl.num_programs(0)

    def gather(tile, slot):
        def per_choice(k, carry):
            base = k * n_tok + tile * TT
            _start_row_gather(y_hbm, lambda r: dest_ref[base + r], gbuf.at[slot, k], gsem.at[slot], TT)
            return carry
        lax.fori_loop(0, TOP_K, per_choice, 0)

    def gather_wait(slot):
        for k in range(TOP_K):
            pltpu.make_async_copy(y_hbm.at[pl.ds(0, TT * ROW_TILES)], gbuf.at[slot, k], gsem.at[slot]).wait()

    slot = i & 1

    @pl.when(i == 0)
    def _():
        gather(0, 0)

    def tile_step(cur_slot):
        nxt_slot = 1 - cur_slot
        nxt_tile = jnp.minimum(i + 1, n - 1)
        for k in range(TOP_K):
            base = k * n_tok + nxt_tile * TT
            _start_row_gather(y_hbm, lambda r: dest_ref[base + r], gbuf.at[nxt_slot, k], gsem.at[nxt_slot], TT)

        w_t = jnp.concatenate([topw_ref[...], jnp.zeros((128 - TOP_K, TT), F32)], axis=0).T
        for k in range(TOP_K):
            wcol[k] = jnp.broadcast_to(w_t[:, k:k + 1], (TT, 128))
        h = tok_ref[...]
        fbuf[...] = _dot((_silu(_dot(h, sg_ref[...])) * _dot(h, su_ref[...])).astype(BF16), sd_ref[...])

        gather_wait(cur_slot)
        for g in range(TT // 8):
            rows = slice(g * 8, (g + 1) * 8)
            tiles = []
            for r in range(8):
                t = g * 8 + r
                acc = None
                for k in range(TOP_K):
                    w = jnp.broadcast_to(wcol[k, t:t + 1, :], (8, 128))
                    term = gbuf[cur_slot, k, t * ROW_TILES:(t + 1) * ROW_TILES, :] * w
                    acc = term if acc is None else acc + term
                tiles.append(acc)
            f = fbuf[rows, :] + jnp.concatenate(_transpose8(tiles), axis=1)
            fbuf[rows, :] = xnew_ref[rows, :] + mod_ref[0, 5:6, :] * _rms(f, gpost_ref[...])

    for s in range(2):
        pl.when(slot == s)(functools.partial(tile_step, s))

    @pl.when(i == n - 1)
    def _():
        gather_wait(1 - slot)

    @pl.when(i < n_ctx_tiles)
    def _():
        outp_ref[...] = fbuf[...]

    @pl.when(i >= n_ctx_tiles)
    def _():
        outs_ref[...] = fbuf[...]


def _combine(tiles, dest_flat, ysorted, topw, tok, xnew, mod3, sg_bf, su_bf, sd_bf, g_post):
    n_tok = tok.shape[0]
    n_ctx_tok = tiles.n_ctx * TM
    n_ctx_tiles = n_ctx_tok // TT
    lat_tiles_per_seq = tiles.lat_tiles * TM // TT

    def mod_row(i):
        return jnp.where(i < n_ctx_tiles, 0, 1 + jnp.maximum(i - n_ctx_tiles, 0) // lat_tiles_per_seq)

    def full(shape):
        return pl.BlockSpec(shape, lambda i, *_: (0,) * len(shape))

    rows = pl.BlockSpec((TT, D_MODEL), lambda i, *_: (i, 0))
    return pl.pallas_call(
        functools.partial(_combine_body, n_tok, n_ctx_tiles),
        out_shape=(jax.ShapeDtypeStruct((n_ctx_tok, D_MODEL), F32),
                   jax.ShapeDtypeStruct((n_tok - n_ctx_tok, D_MODEL), F32)),
        grid_spec=pltpu.PrefetchScalarGridSpec(
            num_scalar_prefetch=1,
            grid=(n_tok // TT,),
            in_specs=[pl.BlockSpec(memory_space=pl.ANY),
                      pl.BlockSpec((TOP_K, TT), lambda i, *_: (0, i)),
                      rows, rows,
                      pl.BlockSpec((1, 6, D_MODEL), lambda i, *_: (mod_row(i), 0, 0)),
                      full((D_MODEL, FF)), full((D_MODEL, FF)), full((FF, D_MODEL)), full((1, D_MODEL))],
            out_specs=(pl.BlockSpec((TT, D_MODEL), lambda i, *_: (jnp.minimum(i, n_ctx_tiles - 1), 0)),
                       pl.BlockSpec((TT, D_MODEL), lambda i, *_: (jnp.maximum(i - n_ctx_tiles, 0), 0))),
            scratch_shapes=[pltpu.VMEM((2, TOP_K, TT * ROW_TILES, 128), F32), pltpu.VMEM((TOP_K, TT, 128), F32),
                            pltpu.VMEM((TT, D_MODEL), F32), pltpu.SemaphoreType.DMA((2,))]),
        compiler_params=_cparams(),
        name="combine",
    )(dest_flat, ysorted, topw, tok, xnew, mod3, sg_bf, su_bf, sd_bf, g_post)


def _rope_tables(lat_len, lat_tiles):
    f32 = np.float32
    rows = lat_len // GRID_W
    row = np.repeat(np.arange(rows, dtype=f32), GRID_W)
    col = np.tile(np.arange(GRID_W, dtype=f32), rows)
    inv = np.power(f32(ROPE_BASE), -np.arange(ROPE_FREQS, dtype=f32) / f32(ROPE_FREQS)).astype(f32)
    ang = np.concatenate([row[:, None] * inv[None, :]] * 2 + [col[:, None] * inv[None, :]] * 2, axis=1).astype(f32)
    sign = np.tile(np.concatenate([-np.ones((ROPE_FREQS,), f32), np.ones((ROPE_FREQS,), f32)]), 2)
    cos = np.cos(ang).astype(f32).reshape(lat_tiles, TM, DK)
    sin = (np.sin(ang).astype(f32) * sign[None, :]).reshape(lat_tiles, TM, DK)
    cos = np.concatenate([np.ones((1, TM, DK), f32), cos], axis=0)
    sin = np.concatenate([np.zeros((1, TM, DK), f32), sin], axis=0)
    return jnp.asarray(cos), jnp.asarray(sin)


def kernel(x_prompt, x_sample, state_ret_fwd, state_ret_bwd, c, c_ctx, w_mod, b_mod, norm_mix_pre,
           norm_mix_post, norm_ffn_pre, norm_ffn_post, w_in, conv_w, conv_b, ret_decay_fwd,
           ret_decay_bwd, ret_gn_g, w_out, router_w, router_bias, expert_w_gate, expert_w_up,
           expert_w_down, shared_w_gate, shared_w_up, shared_w_down):
    bp, tp, d = x_prompt.shape
    bs, ts, _ = x_sample.shape
    depth = w_mod.shape[0]
    assert d == D_MODEL and tp == TM and ts % TM == 0 and bs + 1 <= 8
    tiles = _Tiles(bp, bs, ts)
    n_tok = tiles.n_tiles * TM
    n_slots = n_tok * TOP_K + SUB
    cos_t, sin_t = _rope_tables(ts, tiles.lat_tiles)

    perm = (jnp.arange(N_EXPERTS) % N_GROUPS) * GROUP_SIZE + jnp.arange(N_EXPERTS) // N_GROUPS

    xp2 = x_prompt.reshape(bp * tp, d)
    xs2 = x_sample.reshape(bs * ts, d)
    new_f, new_b = [], []
    for l in range(depth):
        c_rows = jnp.concatenate([c_ctx[None, :], c, jnp.zeros((8 - 1 - bs, d), F32)], axis=0)
        mod3 = _modulation(c_rows, w_mod[l], b_mod[l][None, :]).reshape(8, 6, d)
        dec = jnp.broadcast_to(jnp.stack([ret_decay_fwd[l], ret_decay_bwd[l]])[:, :, None, None],
                               (2, HEADS, DK, DK)).astype(F32)
        xnew, tok, tok_tiles, sf_fin, sb_fin = _token_mixer(
            tiles, xp2, xs2, mod3, norm_mix_pre[l][None, :], w_in[l].astype(BF16), conv_w[l],
            conv_b[l][None, :], dec, cos_t, sin_t, state_ret_fwd[:, l], state_ret_bwd[:, l],
            w_out[l].astype(BF16), norm_mix_post[l][None, :], norm_ffn_pre[l][None, :], ret_gn_g[l][None, :])
        new_f.append(sf_fin)
        new_b.append(sb_fin)

        rwt = router_w[l].T[perm].astype(BF16)
        bias_b = jnp.broadcast_to(router_bias[l][perm][:, None], (N_EXPERTS, 128)).astype(F32)
        topi, topw = _route(tok, rwt, bias_b)
        dest, start, counts = _dispatch_plan(topi)
        dest_flat = dest.reshape(-1)
        start_i = start[:, 0].astype(I32)
        nsub = ((counts[:, 0] + float(SUB - 1)) / float(SUB)).astype(I32)
        xs_tiles = _dispatch(dest_flat, tok_tiles, n_tok, n_slots)
        ysorted = _experts(start_i, nsub, xs_tiles, expert_w_gate[l], expert_w_up[l], expert_w_down[l])
        xp2, xs2 = _combine(tiles, dest_flat, ysorted, topw, tok, xnew, mod3,
                            shared_w_gate[l].astype(BF16), shared_w_up[l].astype(BF16),
                            shared_w_down[l].astype(BF16), norm_ffn_post[l][None, :])

    return (xp2.reshape(bp, tp, d), xs2.reshape(bs, ts, d),
            jnp.stack(new_f, axis=1), jnp.stack(new_b, axis=1))
```

```python
import functools

import numpy as np
import jax
import jax.numpy as jnp
from jax import lax
from jax.experimental import pallas as pl
from jax.experimental.pallas import tpu as pltpu

F32 = jnp.float32
BF16 = jnp.bfloat16
I32 = jnp.int32

D_MODEL = 1024
CONV_W = 512
RET_W = 512
HEADS = 4
DK = 128
CHUNK = 128
GRID_W = 64
ROPE_FREQS = 32
ROPE_BASE = 10000.0
IN_COLS = 3 * CONV_W + 4 * RET_W
N_EXPERTS = 256
N_GROUPS = 8
GROUP_SIZE = N_EXPERTS // N_GROUPS
TOPK_GROUPS = 4
TOP_K = 8
FF = 256
ROUTED_SCALE = 2.5
EPS = 1e-6

TM = 256
SUB = 512
TT = 128
RT = 512
RB = 2048
VMEM_LIMIT = 56 * 1024 * 1024


def _cparams(n_axes=1, vmem=VMEM_LIMIT):
    return pltpu.CompilerParams(dimension_semantics=("arbitrary",) * n_axes,
                                vmem_limit_bytes=vmem)


def _silu(x):
    return x * jax.nn.sigmoid(x)


def _log_sigmoid(x):
    return jnp.minimum(x, 0.0) - jnp.log1p(jnp.exp(-jnp.abs(x)))


def _rms(x, g):
    return x * lax.rsqrt(jnp.mean(x * x, axis=-1, keepdims=True) + EPS) * g


def _dot(a, b):
    return jnp.dot(a, b, preferred_element_type=F32)


def _mod_body(c_ref, w_ref, b_ref, o_ref):
    s = _silu(c_ref[...]).astype(BF16)
    o_ref[...] = _dot(s, w_ref[...].astype(BF16)) + b_ref[...]


def _modulation(c_rows, w_mod, b_mod):
    n_col = w_mod.shape[1]
    blk = 1536
    return pl.pallas_call(
        _mod_body,
        out_shape=jax.ShapeDtypeStruct((8, n_col), F32),
        grid=(n_col // blk,),
        in_specs=[pl.BlockSpec((8, D_MODEL), lambda i: (0, 0)),
                  pl.BlockSpec((D_MODEL, blk), lambda i: (0, i)),
                  pl.BlockSpec((1, blk), lambda i: (0, i))],
        out_specs=pl.BlockSpec((8, blk), lambda i: (0, i)),
        compiler_params=_cparams(),
        name="mod",
    )(c_rows, w_mod, b_mod)


class _Tiles:
    def __init__(self, n_ctx_seq, n_lat_seq, lat_len):
        self.n_ctx = n_ctx_seq
        self.lat_tiles = lat_len // TM
        self.n_lat_seq = n_lat_seq
        self.n_tiles = n_ctx_seq + n_lat_seq * self.lat_tiles

    def is_ctx(self, i):
        return i < self.n_ctx

    def lat_pos(self, i):
        j = jnp.maximum(i - self.n_ctx, 0)
        return j // self.lat_tiles, j % self.lat_tiles

    def phys_reversed(self, i):
        b, t = self.lat_pos(i)
        return jnp.where(i < self.n_ctx, i, self.n_ctx + b * self.lat_tiles + (self.lat_tiles - 1 - t))

    def mod_row(self, i):
        b, _ = self.lat_pos(i)
        return jnp.where(i < self.n_ctx, 0, 1 + b)


def _rope(x, cos, sin_signed):
    lane = lax.broadcasted_iota(I32, x.shape, 1)
    partner = jnp.where((lane & 63) < 32, pltpu.roll(x, 96, 1), pltpu.roll(x, 32, 1))
    return x * cos + partner * sin_signed


def _mix_a_body(tiles, xp_ref, xs_ref, mod_ref, gpre_ref, win_ref, cw_ref, cb_ref, dec_ref,
                cos_ref, sin_ref, s0b_ref,
                yconv_ref, q_ref, v_ref, g_ref, kt_ref, sbin_ref, sbfin_ref,
                sb_scr, tab_scr):
    i = pl.program_id(0)
    is_ctx = tiles.is_ctx(i)
    _, t_rev = tiles.lat_pos(i)
    first = jnp.logical_or(is_ctx, t_rev == 0)

    @pl.when(i == 0)
    def _():
        lg = _log_sigmoid(dec_ref[1])
        col = lax.broadcasted_iota(I32, lg.shape, 2).astype(F32)
        tab_scr[0] = jnp.exp(col * lg)
        tab_scr[1] = jnp.exp(float(CHUNK) * lg)

    @pl.when(first)
    def _():
        sb_scr[...] = jnp.where(is_ctx, 0.0, s0b_ref[0])

    x = jnp.where(is_ctx, xp_ref[...], xs_ref[...])
    h = (_rms(x, gpre_ref[...]) * (1.0 + mod_ref[0, 1:2, :]) + mod_ref[0, 0:1, :]).astype(BF16)

    def proj(k):
        return _dot(h, win_ref[:, k * 512:(k + 1) * 512])

    z = proj(1) * proj(2)
    row = lax.broadcasted_iota(I32, z.shape, 0)
    period = jnp.where(is_ctx, TM, GRID_W)
    pos = row & (period - 1)
    left = jnp.where(pos == 0, 0.0, pltpu.roll(z, 1, 0))
    right = jnp.where(pos == period - 1, 0.0, pltpu.roll(z, TM - 1, 0))
    zc = left * cw_ref[0:1, :] + z * cw_ref[1:2, :] + right * cw_ref[2:3, :] + cb_ref[...]
    yconv_ref[...] = (proj(0) * zc).astype(BF16)

    cos = cos_ref[0]
    sin = sin_ref[0]
    q = proj(3)
    k = proj(4)
    q = jnp.concatenate([_rope(q[:, hh * DK:(hh + 1) * DK], cos, sin) for hh in range(HEADS)], axis=1)
    k = jnp.concatenate([_rope(k[:, hh * DK:(hh + 1) * DK], cos, sin) for hh in range(HEADS)], axis=1)
    q_ref[...] = (q * (DK ** -0.5)).astype(BF16)
    kt = k.T
    kt_ref[...] = kt.astype(BF16)
    v = proj(5).astype(BF16)
    v_ref[...] = v
    g_ref[...] = proj(6)

    for c in (1, 0):
        for hh in range(HEADS):
            sbin_ref[c, hh] = sb_scr[hh].astype(BF16)
            kts = (kt[hh * DK:(hh + 1) * DK, c * CHUNK:(c + 1) * CHUNK] * tab_scr[0, hh]).astype(BF16)
            vc = v[c * CHUNK:(c + 1) * CHUNK, hh * DK:(hh + 1) * DK]
            sb_scr[hh] = sb_scr[hh] * tab_scr[1, hh] + _dot(kts, vc)

    @pl.when(is_ctx)
    def _():
        sbfin_ref[0] = sb_scr[...]


def _mix_b_body(tiles, xp_ref, xs_ref, mod_ref, q_ref, kt_ref, v_ref, g_ref, yconv_ref, sbin_ref,
                wout_ref, gpost_ref, gffn_ref, gn_ref, dec_ref, s0f_ref,
                xnew_ref, tok_ref, toktiles_ref, sffin_ref,
                sf_scr, tab_scr, ycat_scr):
    i = pl.program_id(0)
    is_ctx = tiles.is_ctx(i)
    _, t_pos = tiles.lat_pos(i)
    first = jnp.logical_or(is_ctx, t_pos == 0)

    @pl.when(i == 0)
    def _():
        lgf = _log_sigmoid(dec_ref[0])
        lgb = _log_sigmoid(dec_ref[1])
        row = lax.broadcasted_iota(I32, lgf.shape, 1)
        col = lax.broadcasted_iota(I32, lgf.shape, 2)
        d = (row - col).astype(F32)
        tab_scr[0] = (jnp.where(row >= col, jnp.exp(jnp.where(row >= col, d, 0.0) * lgf), 0.0)
                      + jnp.where(col >= row, jnp.exp(jnp.where(col >= row, -d, 0.0) * lgb), 0.0))
        tab_scr[1] = jnp.exp((row + 1).astype(F32) * lgf)
        tab_scr[2] = jnp.exp((CHUNK - row).astype(F32) * lgb)
        tab_scr[3] = jnp.exp((CHUNK - 1 - col).astype(F32) * lgf)
        tab_scr[4] = jnp.exp(float(CHUNK) * lgf)

    @pl.when(first)
    def _():
        sf_scr[...] = jnp.where(is_ctx, 0.0, s0f_ref[0])

    for c in range(TM // CHUNK):
        rows = slice(c * CHUNK, (c + 1) * CHUNK)
        for hh in range(HEADS):
            cols = slice(hh * DK, (hh + 1) * DK)
            qc = q_ref[rows, cols]
            ktc = kt_ref[cols, rows]
            vc = v_ref[rows, cols]
            att = (_dot(qc, ktc) * tab_scr[0, hh]).astype(BF16)
            o = (_dot(att, vc)
                 + tab_scr[1, hh] * _dot(qc, sf_scr[hh].astype(BF16))
                 + tab_scr[2, hh] * _dot(qc, sbin_ref[c, hh]))
            kts = (ktc.astype(F32) * tab_scr[3, hh]).astype(BF16)
            sf_scr[hh] = sf_scr[hh] * tab_scr[4, hh] + _dot(kts, vc)
            mu = jnp.mean(o, axis=-1, keepdims=True)
            dev = o - mu
            var = jnp.mean(dev * dev, axis=-1, keepdims=True)
            on = dev * lax.rsqrt(var + EPS) * gn_ref[:, cols]
            ycat_scr[rows, RET_W + hh * DK:RET_W + (hh + 1) * DK] = (_silu(g_ref[rows, cols]) * on).astype(BF16)
    ycat_scr[:, 0:CONV_W] = yconv_ref[...]

    @pl.when(is_ctx)
    def _():
        sffin_ref[0] = sf_scr[...]

    x = jnp.where(is_ctx, xp_ref[...], xs_ref[...])
    u = _dot(ycat_scr[...], wout_ref[...])
    xn = x + mod_ref[0, 2:3, :] * _rms(u, gpost_ref[...])
    xnew_ref[...] = xn
    tok = _rms(xn, gffn_ref[...]) * (1.0 + mod_ref[0, 4:5, :]) + mod_ref[0, 3:4, :]
    tok_ref[...] = tok.astype(BF16)
    toktiles_ref[...] = _halftiles_from_rows(_pack_rows(tok))


def _token_mixer(tiles, xp2, xs2, mod3, g_pre, win_bf, conv_w, conv_b, dec, cos_t, sin_t,
                 s0f, s0b, wout_bf, g_post, g_ffn, gn_g):
    n_tok = tiles.n_tiles * TM
    n_ctx = tiles.n_ctx
    last_ctx = n_ctx - 1

    def full(shape):
        return pl.BlockSpec(shape, lambda i: (0,) * len(shape))

    def xp_spec(phys):
        return pl.BlockSpec((TM, D_MODEL), lambda i: (jnp.minimum(phys(i), last_ctx), 0))

    def xs_spec(phys):
        return pl.BlockSpec((TM, D_MODEL), lambda i: (jnp.maximum(phys(i) - n_ctx, 0), 0))

    mod_spec = pl.BlockSpec((1, 6, D_MODEL), lambda i: (tiles.mod_row(i), 0, 0))
    state_in = pl.BlockSpec((1, HEADS, DK, DK), lambda i: (tiles.lat_pos(i)[0], 0, 0, 0))
    state_out = pl.BlockSpec((1, HEADS, DK, DK), lambda i: (jnp.minimum(i, last_ctx), 0, 0, 0))

    rev = tiles.phys_reversed

    def rope_idx(i):
        _, t = tiles.lat_pos(i)
        return jnp.where(i < n_ctx, 0, 1 + (tiles.lat_tiles - 1 - t))

    rope_spec = pl.BlockSpec((1, TM, DK), lambda i: (rope_idx(i), 0, 0))

    def rows(width, phys):
        return pl.BlockSpec((TM, width), lambda i: (phys(i), 0))

    yconv, q, v, g, kt, sbin, sb_fin = pl.pallas_call(
        functools.partial(_mix_a_body, tiles),
        out_shape=(jax.ShapeDtypeStruct((n_tok, CONV_W), BF16),
                   jax.ShapeDtypeStruct((n_tok, RET_W), BF16),
                   jax.ShapeDtypeStruct((n_tok, RET_W), BF16),
                   jax.ShapeDtypeStruct((n_tok, RET_W), F32),
                   jax.ShapeDtypeStruct((RET_W, n_tok), BF16),
                   jax.ShapeDtypeStruct((n_tok // CHUNK, HEADS, DK, DK), BF16),
                   jax.ShapeDtypeStruct((n_ctx, HEADS, DK, DK), F32)),
        grid=(tiles.n_tiles,),
        in_specs=[xp_spec(rev), xs_spec(rev), mod_spec, full((1, D_MODEL)), full((D_MODEL, IN_COLS)),
                  full((3, CONV_W)), full((1, CONV_W)), full((2, HEADS, DK, DK)),
                  rope_spec, rope_spec, state_in],
        out_specs=(rows(CONV_W, rev), rows(RET_W, rev), rows(RET_W, rev), rows(RET_W, rev),
                   pl.BlockSpec((RET_W, TM), lambda i: (0, rev(i))),
                   pl.BlockSpec((TM // CHUNK, HEADS, DK, DK), lambda i: (rev(i), 0, 0, 0)),
                   state_out),
        scratch_shapes=[pltpu.VMEM((HEADS, DK, DK), F32), pltpu.VMEM((2, HEADS, DK, DK), F32)],
        compiler_params=_cparams(),
        name="mix_a",
    )(xp2, xs2, mod3, g_pre, win_bf, conv_w, conv_b, dec, cos_t, sin_t, s0b)

    ident = lambda i: i
    xnew, tok, tok_tiles, sf_fin = pl.pallas_call(
        functools.partial(_mix_b_body, tiles),
        out_shape=(jax.ShapeDtypeStruct((n_tok, D_MODEL), F32),
                   jax.ShapeDtypeStruct((n_tok, D_MODEL), BF16),
                   jax.ShapeDtypeStruct((n_tok * HALF_TILE, 128), U32),
                   jax.ShapeDtypeStruct((n_ctx, HEADS, DK, DK), F32)),
        grid=(tiles.n_tiles,),
        in_specs=[xp_spec(ident), xs_spec(ident), mod_spec,
                  rows(RET_W, ident),
                  pl.BlockSpec((RET_W, TM), lambda i: (0, i)),
                  rows(RET_W, ident), rows(RET_W, ident), rows(CONV_W, ident),
                  pl.BlockSpec((TM // CHUNK, HEADS, DK, DK), lambda i: (i, 0, 0, 0)),
                  full((D_MODEL, D_MODEL)), full((1, D_MODEL)), full((1, D_MODEL)), full((1, RET_W)),
                  full((2, HEADS, DK, DK)), state_in],
        out_specs=(rows(D_MODEL, ident), rows(D_MODEL, ident),
                   pl.BlockSpec((TM * HALF_TILE, 128), lambda i: (i, 0)), state_out),
        scratch_shapes=[pltpu.VMEM((HEADS, DK, DK), F32), pltpu.VMEM((5, HEADS, DK, DK), F32),
                        pltpu.VMEM((TM, D_MODEL), BF16)],
        compiler_params=_cparams(),
        name="mix_b",
    )(xp2, xs2, mod3, q, kt, v, g, yconv, sbin, wout_bf, g_post, g_ffn, gn_g, dec, s0f)
    return xnew, tok, tok_tiles, sf_fin, sb_fin


def _route_body(tok_ref, rwt_ref, bias_ref, topi_ref, topw_ref):
    h = tok_ref[...]
    logits = lax.dot_general(rwt_ref[...], h, (((1,), (1,)), ((), ())), preferred_element_type=F32)
    shape3 = (GROUP_SIZE, N_GROUPS, 128)
    member = lax.broadcasted_iota(I32, shape3, 0)
    group = lax.broadcasted_iota(I32, shape3, 1)
    expert = group * GROUP_SIZE + member
    group2 = lax.broadcasted_iota(I32, (N_GROUPS, 128), 0)
    neg = -jnp.inf
    for lb in range(RT // 128):
        scores = jax.nn.sigmoid(logits[:, lb * 128:(lb + 1) * 128]).reshape(shape3)
        biased = scores + bias_ref[...].reshape(shape3)
        m1 = jnp.max(biased, axis=0)
        first = jnp.min(jnp.where(biased == m1, member, GROUP_SIZE), axis=0)
        m2 = jnp.max(jnp.where(member == first, neg, biased), axis=0)
        gs = m1 + m2
        beaten = jnp.zeros(gs.shape, I32)
        for s in range(1, N_GROUPS):
            other = pltpu.roll(gs, s, 0)
            wins = (other > gs) | ((other == gs) & (group2 >= s))
            beaten = beaten + wins.astype(I32)
        keep = beaten < TOPK_GROUPS
        cand = jnp.where(keep, biased, neg)
        idx_rows, w_rows = [], []
        for _ in range(TOP_K):
            best = jnp.max(jnp.max(cand, axis=0), axis=0, keepdims=True)
            pick = jnp.min(jnp.min(jnp.where(cand == best, expert, N_EXPERTS), axis=0), axis=0, keepdims=True)
            hit = expert == pick
            w_rows.append(jnp.sum(jnp.sum(jnp.where(hit, scores, 0.0), axis=0), axis=0, keepdims=True))
            idx_rows.append(pick)
            cand = jnp.where(hit, neg, cand)
        w = jnp.concatenate(w_rows, axis=0)
        topi_ref[:, lb * 128:(lb + 1) * 128] = jnp.concatenate(idx_rows, axis=0)
        topw_ref[:, lb * 128:(lb + 1) * 128] = w / jnp.sum(w, axis=0, keepdims=True) * ROUTED_SCALE


def _route(tok, rwt_bf, bias_b):
    n_tok = tok.shape[0]
    return pl.pallas_call(
        _route_body,
        out_shape=(jax.ShapeDtypeStruct((TOP_K, n_tok), I32), jax.ShapeDtypeStruct((TOP_K, n_tok), F32)),
        grid=(n_tok // RT,),
        in_specs=[pl.BlockSpec((RT, D_MODEL), lambda i: (i, 0)),
                  pl.BlockSpec((N_EXPERTS, D_MODEL), lambda i: (0, 0)),
                  pl.BlockSpec((N_EXPERTS, 128), lambda i: (0, 0))],
        out_specs=(pl.BlockSpec((TOP_K, RT), lambda i: (0, i)), pl.BlockSpec((TOP_K, RT), lambda i: (0, i))),
        compiler_params=_cparams(),
        name="route",
    )(tok, rwt_bf, bias_b)


def _onehot(ids_row):
    e = lax.broadcasted_iota(I32, (N_EXPERTS, 256), 0)
    return e == ids_row


def _rank_body(topi_ref, rank_ref, counts_ref, run_scr):
    i = pl.program_id(0)

    @pl.when(i == 0)
    def _():
        run_scr[...] = jnp.zeros(run_scr.shape, F32)

    a0 = lax.broadcasted_iota(I32, (256, 256), 0)
    a1 = lax.broadcasted_iota(I32, (256, 256), 1)
    upper = (a0 <= a1).astype(BF16)
    ones = jnp.ones((256, 256), BF16)
    for k in range(TOP_K):
        for sb in range(RB // 256):
            lanes = slice(sb * 256, (sb + 1) * 256)
            oh = _onehot(topi_ref[k:k + 1, lanes])
            ohb = oh.astype(BF16)
            seen = _dot(ohb, upper) + run_scr[...]
            r = jnp.sum(jnp.where(oh, seen, 0.0), axis=0, keepdims=True) - 1.0
            rank_ref[k:k + 1, lanes] = r.astype(I32)
            run_scr[...] = run_scr[...] + _dot(ohb, ones)

    @pl.when(i == pl.num_programs(0) - 1)
    def _():
        counts_ref[...] = run_scr[:, 0:128]


def _dest_body(topi_ref, rank_ref, counts_ref, dest_ref, start_ref, start_scr):
    i = pl.program_id(0)

    @pl.when(i == 0)
    def _():
        c = counts_ref[...]
        d2 = jnp.floor(c / 16384.0)
        rem = c - d2 * 16384.0
        d1 = jnp.floor(rem / 128.0)
        d0 = rem - d1 * 128.0
        e0 = lax.broadcasted_iota(I32, (N_EXPERTS, N_EXPERTS), 0)
        e1 = lax.broadcasted_iota(I32, (N_EXPERTS, N_EXPERTS), 1)
        below = (e1 < e0).astype(BF16)
        start_scr[...] = (16384.0 * _dot(below, d2.astype(BF16)) + 128.0 * _dot(below, d1.astype(BF16))
                          + _dot(below, d0.astype(BF16)))
        start_ref[...] = start_scr[...]

    start = jnp.concatenate([start_scr[...], start_scr[...]], axis=1)
    for k in range(TOP_K):
        for sb in range(RB // 256):
            lanes = slice(sb * 256, (sb + 1) * 256)
            oh = _onehot(topi_ref[k:k + 1, lanes])
            base = jnp.sum(jnp.where(oh, start, 0.0), axis=0, keepdims=True)
            dest_ref[k:k + 1, lanes] = base.astype(I32) + rank_ref[k:k + 1, lanes]


def _dispatch_plan(topi):
    n_tok = topi.shape[1]
    blk = pl.BlockSpec((TOP_K, RB), lambda i: (0, i))
    whole = pl.BlockSpec((N_EXPERTS, 128), lambda i: (0, 0))
    rank, counts = pl.pallas_call(
        _rank_body,
        out_shape=(jax.ShapeDtypeStruct((TOP_K, n_tok), I32), jax.ShapeDtypeStruct((N_EXPERTS, 128), F32)),
        grid=(n_tok // RB,),
        in_specs=[blk],
        out_specs=(blk, whole),
        scratch_shapes=[pltpu.VMEM((N_EXPERTS, 256), F32)],
        compiler_params=_cparams(),
        name="rank",
    )(topi)
    dest, start = pl.pallas_call(
        _dest_body,
        out_shape=(jax.ShapeDtypeStruct((TOP_K, n_tok), I32), jax.ShapeDtypeStruct((N_EXPERTS, 128), F32)),
        grid=(n_tok // RB,),
        in_specs=[blk, blk, whole],
        out_specs=(blk, whole),
        scratch_shapes=[pltpu.VMEM((N_EXPERTS, 128), F32)],
        compiler_params=_cparams(),
        name="dest",
    )(topi, rank, counts)
    return dest, start, counts


ROW_TILES = D_MODEL // 128


def _transpose8(vs):
    sub = lax.broadcasted_iota(I32, (8, 128), 0)
    for d in (4, 2, 1):
        keep = (sub & d) == 0
        out = list(vs)
        for i in range(8):
            if i & d == 0:
                a, b = vs[i], vs[i + d]
                out[i] = jnp.where(keep, a, pltpu.roll(b, d, 0))
                out[i + d] = jnp.where(keep, pltpu.roll(a, 8 - d, 0), b)
        vs = out
    return vs


def _rows_from_tiles(tiles):
    n_rows = tiles.shape[0] // ROW_TILES
    groups = [_transpose8([tiles[(g * 8 + r) * ROW_TILES:(g * 8 + r + 1) * ROW_TILES] for r in range(8)])
              for g in range(n_rows // 8)]
    return jnp.concatenate([jnp.concatenate([grp[c] for grp in groups], axis=0) for c in range(ROW_TILES)], axis=1)


def _rows_to_tiles(value, row_of_tile=lambda t: t):
    n_rows = value.shape[0]
    pieces = []
    for g in range(n_rows // 8):
        pieces += _transpose8([value[g * 8:(g + 1) * 8, c * 128:(c + 1) * 128] for c in range(ROW_TILES)])
    return jnp.concatenate([pieces[row_of_tile(t)] for t in range(n_rows)], axis=0)


U32 = jnp.uint32
HALF_TILE = 4
HIGH_HALF = 0xFFFF0000


def _pack_rows(x):
    half = D_MODEL // 2
    lo = pltpu.bitcast(x[:, :half].astype(BF16).astype(F32), U32)
    hi = pltpu.bitcast(x[:, half:].astype(BF16).astype(F32), U32)
    return (lo >> 16) | (hi & jnp.uint32(HIGH_HALF))


def _unpack_rows(u):
    lo = pltpu.bitcast(u << 16, F32)
    hi = pltpu.bitcast(u & jnp.uint32(HIGH_HALF), F32)
    return jnp.concatenate([lo, hi], axis=1).astype(BF16)


def _halftiles_from_rows(u):
    pieces = []
    for q in range(u.shape[0] // 16):
        pieces += _transpose8([u[q * 16 + h * 8:q * 16 + h * 8 + 8, c * 128:(c + 1) * 128]
                               for h in range(2) for c in range(HALF_TILE)])
    return jnp.concatenate(pieces, axis=0)


def _halftile_row(r):
    q, j = divmod(r, 16)
    return (q * 8 + j % 8) * 8 + (j // 8) * HALF_TILE


def _rows_from_halftiles(t):
    blocks = []
    for q in range(t.shape[0] // 64):
        outs = _transpose8([t[(q * 8 + p) * 8:(q * 8 + p + 1) * 8] for p in range(8)])
        for h in range(2):
            blocks.append(jnp.concatenate(outs[h * HALF_TILE:(h + 1) * HALF_TILE], axis=1))
    return jnp.concatenate(blocks, axis=0)


def _row_of_linear_slot(s):
    q, m = divmod(s, 16)
    return q * 16 + (m // 2 if m % 2 == 0 else 8 + m // 2)


def _pair_position(r):
    q, j = divmod(r, 16)
    return q * 16 + (2 * j if j < 8 else 2 * (j - 8) + 1)


def _start_row_gather(src_hbm, idx_of, dst, sem, n_rows):
    for r in range(n_rows):
        src_row = pl.multiple_of(idx_of(r) * HALF_TILE, HALF_TILE)
        pltpu.make_async_copy(src_hbm.at[pl.ds(src_row, HALF_TILE)],
                              dst.at[pl.ds(_pair_position(r) * HALF_TILE, HALF_TILE)], sem).start(priority=r % 2)


DISPATCH_ROWS = 128
DISPATCH_DEPTH = 4


def _dispatch_body(n_tok, dest_ref, tok_hbm, xs_hbm, tbuf, zbuf, sem, fsem, zsem):
    i = pl.program_id(0)
    k = pl.program_id(1)
    n_blk = pl.num_programs(0)
    step = i * TOP_K + k
    n_steps = n_blk * TOP_K
    unit = tbuf.shape[1] // DISPATCH_ROWS
    blk_rows = DISPATCH_ROWS * unit
    buf = i & 1

    def fetch(blk, b):
        return pltpu.make_async_copy(tok_hbm.at[pl.ds(pl.multiple_of(blk * blk_rows, blk_rows), blk_rows)],
                                     tbuf.at[b], fsem.at[b])

    def batch_wait(b):
        pltpu.make_async_copy(tbuf.at[b], xs_hbm.at[pl.ds(0, blk_rows)], sem.at[b]).wait()

    @pl.when(step == 0)
    def _():
        fetch(0, 0).start()
        zbuf[...] = jnp.zeros(zbuf.shape, zbuf.dtype)
        tail = pltpu.make_async_copy(zbuf, xs_hbm.at[pl.ds(xs_hbm.shape[0] - zbuf.shape[0], zbuf.shape[0])], zsem)
        tail.start()
        tail.wait()

    @pl.when(k == 0)
    def _():
        fetch(i, buf).wait()

    @pl.when(jnp.logical_and(i > 0, k < DISPATCH_DEPTH))
    def _():
        for _ in range(TOP_K // DISPATCH_DEPTH):
            batch_wait(1 - buf)

    @pl.when(jnp.logical_and(k == DISPATCH_DEPTH, i + 1 < n_blk))
    def _():
        fetch(i + 1, 1 - buf).start()

    a0 = k * n_tok + i * DISPATCH_ROWS

    def scatter(b):
        for r in range(DISPATCH_ROWS):
            dst_row = pl.multiple_of(dest_ref[a0 + r] * unit, unit)
            pltpu.make_async_copy(tbuf.at[b, pl.ds(_halftile_row(r), unit)], xs_hbm.at[pl.ds(dst_row, unit)],
                                  sem.at[b]).start(priority=r % 2)

    for b in range(2):
        pl.when(buf == b)(functools.partial(scatter, b))

    @pl.when(step == n_steps - 1)
    def _():
        for _ in range(TOP_K):
            batch_wait(buf)


def _dispatch(dest_flat, tok_rows, n_tok, n_rows_out):
    assert DISPATCH_DEPTH < TOP_K and TOP_K % DISPATCH_DEPTH == 0
    unit = tok_rows.shape[0] // n_tok
    width = tok_rows.shape[1]
    assert unit == HALF_TILE and DISPATCH_ROWS % 16 == 0
    dtype = tok_rows.dtype
    return pl.pallas_call(
        functools.partial(_dispatch_body, n_tok),
        out_shape=jax.ShapeDtypeStruct((n_rows_out * unit, width), dtype),
        grid_spec=pltpu.PrefetchScalarGridSpec(
            num_scalar_prefetch=1,
            grid=(n_tok // DISPATCH_ROWS, TOP_K),
            in_specs=[pl.BlockSpec(memory_space=pl.ANY)],
            out_specs=pl.BlockSpec(memory_space=pl.ANY),
            scratch_shapes=[pltpu.VMEM((2, DISPATCH_ROWS * unit, width), dtype),
                            pltpu.VMEM((SUB * unit, width), dtype),
                            pltpu.SemaphoreType.DMA((2,)), pltpu.SemaphoreType.DMA((2,)),
                            pltpu.SemaphoreType.DMA(())]),
        compiler_params=_cparams(2),
        name="dispatch",
    )(dest_flat, tok_rows)


N_XBUF = 4
LOOKAHEAD = 3
W_AHEAD = 2
WEIGHT_PRIORITY = 1


def _experts_body(start_ref, nsub_ref,
                  xs_hbm, wg_hbm, wu_hbm, wd_hbm, y_hbm,
                  xbuf, ybuf0, ybuf1,
                  wg_f32, wu_f32, wd_f32, wg_bf, wu_bf, wd_bf, cur, nxt, gsem, osem, wsem):
    e = pl.program_id(0)
    n_e = pl.num_programs(0)
    nsub = nsub_ref[e]
    sub_rows = SUB * HALF_TILE
    in_rows = sub_rows
    overflow_row = y_hbm.shape[0] - sub_rows

    def weight_copies(ex, slot):
        return [pltpu.make_async_copy(src.at[ex], dst.at[slot], wsem.at[slot, n])
                for n, (src, dst) in enumerate(((wg_hbm, wg_f32), (wu_hbm, wu_f32), (wd_hbm, wd_f32)))]

    def window_row(ex, j, rows_per_slot):
        return pl.multiple_of((start_ref[ex] + j * SUB) * rows_per_slot, rows_per_slot)


    def produce():
        pe = cur[0]
        pj = cur[1]
        pg = cur[2]
        live = pe < n_e
        pe_c = jnp.minimum(pe, n_e - 1)
        row = pl.multiple_of(jnp.where(live, window_row(pe_c, pj, HALF_TILE), 0), HALF_TILE)
        slot = lax.rem(pg, N_XBUF)
        pltpu.make_async_copy(xs_hbm.at[pl.ds(row, in_rows)], xbuf.at[slot], gsem.at[slot]).start()
        last = pj + 1 >= nsub_ref[pe_c]
        cur[0] = jnp.where(jnp.logical_and(live, last), nxt[pe_c], pe)
        cur[1] = jnp.where(last, 0, pj + 1)
        cur[2] = pg + 1

    def fetch_wait(slot):
        pltpu.make_async_copy(xs_hbm.at[pl.ds(0, in_rows)], xbuf.at[slot], gsem.at[slot]).wait()

    ybuf = (ybuf0, ybuf1)

    def out_wait():
        pltpu.make_async_copy(ybuf0, y_hbm.at[pl.ds(0, sub_rows)], osem).wait()

    @pl.when(e == 0)
    def _():
        def fill(i, following):
            x = N_EXPERTS - 1 - i
            nxt[x] = following
            return jnp.where(nsub_ref[x] > 0, x, following)

        cur[0] = lax.fori_loop(0, N_EXPERTS, fill, N_EXPERTS)
        cur[1] = 0
        cur[2] = 0
        cur[3] = 0
        for ahead in range(W_AHEAD):
            for cp in weight_copies(ahead, ahead):
                cp.start(priority=WEIGHT_PRIORITY)
        for _ in range(LOOKAHEAD):
            produce()
        ybuf1[...] = jnp.zeros(ybuf1.shape, U32)
        pltpu.make_async_copy(ybuf1, y_hbm.at[pl.ds(overflow_row, sub_rows)], osem).start()

    wslot = lax.rem(e, W_AHEAD + 1)

    @pl.when(e + W_AHEAD < n_e)
    def _():
        for cp in weight_copies(e + W_AHEAD, lax.rem(e + W_AHEAD, W_AHEAD + 1)):
            cp.start(priority=WEIGHT_PRIORITY)

    for cp in weight_copies(e, wslot):
        cp.wait()
    wg_bf[...] = wg_f32[wslot].astype(BF16)
    wu_bf[...] = wu_f32[wslot].astype(BF16)
    wd_bf[...] = wd_f32[wslot].astype(BF16)

    def step(j, carry):
        g = cur[3]
        produce()
        slot = lax.rem(g, N_XBUF)
        fetch_wait(slot)
        xb = _unpack_rows(_rows_from_halftiles(xbuf[slot]))
        a = _dot(xb, wg_bf[...])
        b = _dot(xb, wu_bf[...])
        y_tiles = _halftiles_from_rows(_pack_rows(_dot((_silu(a) * b).astype(BF16), wd_bf[...])))
        row = window_row(e, j, HALF_TILE)
        for parity in range(2):
            @pl.when((g & 1) == parity)
            def _():
                ybuf[parity][...] = y_tiles
                out_wait()
                pltpu.make_async_copy(ybuf[parity], y_hbm.at[pl.ds(row, sub_rows)], osem).start()
        cur[3] = g + 1
        return carry

    lax.fori_loop(0, nsub, step, 0)

    @pl.when(e == n_e - 1)
    def _():
        total = cur[3]
        out_wait()
        for ahead in range(LOOKAHEAD):
            fetch_wait(lax.rem(total + ahead, N_XBUF))
        ybuf0[...] = jnp.zeros(ybuf0.shape, U32)
        tail = pltpu.make_async_copy(ybuf0, y_hbm.at[pl.ds(overflow_row, sub_rows)], osem)
        tail.start()
        tail.wait()


def _experts(start, nsub, xs_tiles, wg, wu, wd):
    sub_rows = SUB * HALF_TILE
    return pl.pallas_call(
        _experts_body,
        out_shape=jax.ShapeDtypeStruct(xs_tiles.shape, U32),
        grid_spec=pltpu.PrefetchScalarGridSpec(
            num_scalar_prefetch=2,
            grid=(N_EXPERTS,),
            in_specs=[pl.BlockSpec(memory_space=pl.ANY)] * 4,
            out_specs=pl.BlockSpec(memory_space=pl.ANY),
            scratch_shapes=[pltpu.VMEM((N_XBUF, sub_rows, 128), U32),
                            pltpu.VMEM((sub_rows, 128), U32), pltpu.VMEM((sub_rows, 128), U32),
                            pltpu.VMEM((W_AHEAD + 1, D_MODEL, FF), F32), pltpu.VMEM((W_AHEAD + 1, D_MODEL, FF), F32),
                            pltpu.VMEM((W_AHEAD + 1, FF, D_MODEL), F32),
                            pltpu.VMEM((D_MODEL, FF), BF16), pltpu.VMEM((D_MODEL, FF), BF16),
                            pltpu.VMEM((FF, D_MODEL), BF16), pltpu.SMEM((4,), I32), pltpu.SMEM((N_EXPERTS,), I32),
                            pltpu.SemaphoreType.DMA((N_XBUF,)), pltpu.SemaphoreType.DMA(()),
                            pltpu.SemaphoreType.DMA((W_AHEAD + 1, 3))]),
        compiler_params=_cparams(),
        name="experts",
    )(start, nsub, xs_tiles, wg, wu, wd)


def _combine_body(n_tok, n_ctx_tiles, dest_ref,
                  y_hbm, topw_ref, tok_ref, xnew_ref, mod_ref, sg_ref, su_ref, sd_ref, gpost_ref,
                  outp_ref, outs_ref, gbuf, wcol, fbuf, gsem):
    i = pl.program_id(0)
    n = pl.num_programs(0)

    def gather(tile, slot):
        def per_choice(k, carry):
            base = k * n_tok + tile * TT
            _start_row_gather(y_hbm, lambda r: dest_ref[base + r], gbuf.at[slot, k], gsem.at[slot], TT)
            return carry
        lax.fori_loop(0, TOP_K, per_choice, 0)

    def gather_wait(slot):
        for k in range(TOP_K):
            pltpu.make_async_copy(y_hbm.at[pl.ds(0, TT * HALF_TILE)], gbuf.at[slot, k], gsem.at[slot]).wait()

    slot = i & 1

    @pl.when(i == 0)
    def _():
        gather(0, 0)

    def tile_step(cur_slot):
        nxt_slot = 1 - cur_slot
        nxt_tile = jnp.minimum(i + 1, n - 1)
        for k in range(TOP_K):
            base = k * n_tok + nxt_tile * TT
            _start_row_gather(y_hbm, lambda r: dest_ref[base + r], gbuf.at[nxt_slot, k], gsem.at[nxt_slot], TT)

        w_t = jnp.concatenate([topw_ref[...], jnp.zeros((128 - TOP_K, TT), F32)], axis=0).T
        for k in range(TOP_K):
            wcol[k] = jnp.broadcast_to(w_t[:, k:k + 1], (TT, 128))
        h = tok_ref[...]
        fbuf[...] = _dot((_silu(_dot(h, sg_ref[...])) * _dot(h, su_ref[...])).astype(BF16), sd_ref[...])

        gather_wait(cur_slot)
        first_half = lax.broadcasted_iota(I32, (8, 128), 0) < HALF_TILE
        for q in range(TT // 16):
            acc_lo, acc_hi = [], []
            for v in range(8):
                t0 = q * 16 + v
                lo_sum = hi_sum = None
                for k in range(TOP_K):
                    u = gbuf[cur_slot, k, (q * 8 + v) * 8:(q * 8 + v + 1) * 8, :]
                    w = jnp.where(first_half, jnp.broadcast_to(wcol[k, t0:t0 + 1, :], (8, 128)),
                                  jnp.broadcast_to(wcol[k, t0 + 8:t0 + 9, :], (8, 128)))
                    lo = pltpu.bitcast(u << 16, F32) * w
                    hi = pltpu.bitcast(u & jnp.uint32(HIGH_HALF), F32) * w
                    lo_sum = lo if lo_sum is None else lo_sum + lo
                    hi_sum = hi if hi_sum is None else hi_sum + hi
                acc_lo.append(lo_sum)
                acc_hi.append(hi_sum)
            rows_lo = _transpose8(acc_lo)
            rows_hi = _transpose8(acc_hi)
            for h in range(2):
                rows = slice(q * 16 + h * 8, q * 16 + h * 8 + 8)
                routed = jnp.concatenate(rows_lo[h * HALF_TILE:(h + 1) * HALF_TILE]
                                         + rows_hi[h * HALF_TILE:(h + 1) * HALF_TILE], axis=1)
                f = fbuf[rows, :] + routed
                fbuf[rows, :] = xnew_ref[rows, :] + mod_ref[0, 5:6, :] * _rms(f, gpost_ref[...])

    for s in range(2):
        pl.when(slot == s)(functools.partial(tile_step, s))

    @pl.when(i == n - 1)
    def _():
        gather_wait(1 - slot)

    @pl.when(i < n_ctx_tiles)
    def _():
        outp_ref[...] = fbuf[...]

    @pl.when(i >= n_ctx_tiles)
    def _():
        outs_ref[...] = fbuf[...]


def _combine(tiles, dest_flat, ysorted, topw, tok, xnew, mod3, sg_bf, su_bf, sd_bf, g_post):
    n_tok = tok.shape[0]
    n_ctx_tok = tiles.n_ctx * TM
    n_ctx_tiles = n_ctx_tok // TT
    lat_tiles_per_seq = tiles.lat_tiles * TM // TT

    def mod_row(i):
        return jnp.where(i < n_ctx_tiles, 0, 1 + jnp.maximum(i - n_ctx_tiles, 0) // lat_tiles_per_seq)

    def full(shape):
        return pl.BlockSpec(shape, lambda i, *_: (0,) * len(shape))

    rows = pl.BlockSpec((TT, D_MODEL), lambda i, *_: (i, 0))
    return pl.pallas_call(
        functools.partial(_combine_body, n_tok, n_ctx_tiles),
        out_shape=(jax.ShapeDtypeStruct((n_ctx_tok, D_MODEL), F32),
                   jax.ShapeDtypeStruct((n_tok - n_ctx_tok, D_MODEL), F32)),
        grid_spec=pltpu.PrefetchScalarGridSpec(
            num_scalar_prefetch=1,
            grid=(n_tok // TT,),
            in_specs=[pl.BlockSpec(memory_space=pl.ANY),
                      pl.BlockSpec((TOP_K, TT), lambda i, *_: (0, i)),
                      rows, rows,
                      pl.BlockSpec((1, 6, D_MODEL), lambda i, *_: (mod_row(i), 0, 0)),
                      full((D_MODEL, FF)), full((D_MODEL, FF)), full((FF, D_MODEL)), full((1, D_MODEL))],
            out_specs=(pl.BlockSpec((TT, D_MODEL), lambda i, *_: (jnp.minimum(i, n_ctx_tiles - 1), 0)),
                       pl.BlockSpec((TT, D_MODEL), lambda i, *_: (jnp.maximum(i - n_ctx_tiles, 0), 0))),
            scratch_shapes=[pltpu.VMEM((2, TOP_K, TT * HALF_TILE, 128), U32), pltpu.VMEM((TOP_K, TT, 128), F32),
                            pltpu.VMEM((TT, D_MODEL), F32), pltpu.SemaphoreType.DMA((2,))]),
        compiler_params=_cparams(),
        name="combine",
    )(dest_flat, ysorted, topw, tok, xnew, mod3, sg_bf, su_bf, sd_bf, g_post)


def _rope_tables(lat_len, lat_tiles):
    f32 = np.float32
    rows = lat_len // GRID_W
    row = np.repeat(np.arange(rows, dtype=f32), GRID_W)
    col = np.tile(np.arange(GRID_W, dtype=f32), rows)
    inv = np.power(f32(ROPE_BASE), -np.arange(ROPE_FREQS, dtype=f32) / f32(ROPE_FREQS)).astype(f32)
    ang = np.concatenate([row[:, None] * inv[None, :]] * 2 + [col[:, None] * inv[None, :]] * 2, axis=1).astype(f32)
    sign = np.tile(np.concatenate([-np.ones((ROPE_FREQS,), f32), np.ones((ROPE_FREQS,), f32)]), 2)
    cos = np.cos(ang).astype(f32).reshape(lat_tiles, TM, DK)
    sin = (np.sin(ang).astype(f32) * sign[None, :]).reshape(lat_tiles, TM, DK)
    cos = np.concatenate([np.ones((1, TM, DK), f32), cos], axis=0)
    sin = np.concatenate([np.zeros((1, TM, DK), f32), sin], axis=0)
    return jnp.asarray(cos), jnp.asarray(sin)


def kernel(x_prompt, x_sample, state_ret_fwd, state_ret_bwd, c, c_ctx, w_mod, b_mod, norm_mix_pre,
           norm_mix_post, norm_ffn_pre, norm_ffn_post, w_in, conv_w, conv_b, ret_decay_fwd,
           ret_decay_bwd, ret_gn_g, w_out, router_w, router_bias, expert_w_gate, expert_w_up,
           expert_w_down, shared_w_gate, shared_w_up, shared_w_down):
    bp, tp, d = x_prompt.shape
    bs, ts, _ = x_sample.shape
    depth = w_mod.shape[0]
    assert d == D_MODEL and tp == TM and ts % TM == 0 and bs + 1 <= 8
    tiles = _Tiles(bp, bs, ts)
    n_tok = tiles.n_tiles * TM
    n_slots = n_tok * TOP_K + SUB
    cos_t, sin_t = _rope_tables(ts, tiles.lat_tiles)

    perm = (jnp.arange(N_EXPERTS) % N_GROUPS) * GROUP_SIZE + jnp.arange(N_EXPERTS) // N_GROUPS

    xp2 = x_prompt.reshape(bp * tp, d)
    xs2 = x_sample.reshape(bs * ts, d)
    new_f, new_b = [], []
    for l in range(depth):
        c_rows = jnp.concatenate([c_ctx[None, :], c, jnp.zeros((8 - 1 - bs, d), F32)], axis=0)
        mod3 = _modulation(c_rows, w_mod[l], b_mod[l][None, :]).reshape(8, 6, d)
        dec = jnp.broadcast_to(jnp.stack([ret_decay_fwd[l], ret_decay_bwd[l]])[:, :, None, None],
                               (2, HEADS, DK, DK)).astype(F32)
        xnew, tok, tok_tiles, sf_fin, sb_fin = _token_mixer(
            tiles, xp2, xs2, mod3, norm_mix_pre[l][None, :], w_in[l].astype(BF16), conv_w[l],
            conv_b[l][None, :], dec, cos_t, sin_t, state_ret_fwd[:, l], state_ret_bwd[:, l],
            w_out[l].astype(BF16), norm_mix_post[l][None, :], norm_ffn_pre[l][None, :], ret_gn_g[l][None, :])
        new_f.append(sf_fin)
        new_b.append(sb_fin)

        rwt = router_w[l].T[perm].astype(BF16)
        bias_b = jnp.broadcast_to(router_bias[l][perm][:, None], (N_EXPERTS, 128)).astype(F32)
        topi, topw = _route(tok, rwt, bias_b)
        dest, start, counts = _dispatch_plan(topi)
        dest_flat = dest.reshape(-1)
        start_i = start[:, 0].astype(I32)
        nsub = ((counts[:, 0] + float(SUB - 1)) / float(SUB)).astype(I32)
        xs_tiles = _dispatch(dest_flat, tok_tiles, n_tok, n_slots)
        ysorted = _experts(start_i, nsub, xs_tiles, expert_w_gate[l], expert_w_up[l], expert_w_down[l])
        xp2, xs2 = _combine(tiles, dest_flat, ysorted, topw, tok, xnew, mod3,
                            shared_w_gate[l].astype(BF16), shared_w_up[l].astype(BF16),
                            shared_w_down[l].astype(BF16), norm_ffn_post[l][None, :])

    return (xp2.reshape(bp, tp, d), xs2.reshape(bs, ts, d),
            jnp.stack(new_f, axis=1), jnp.stack(new_b, axis=1))
```

```python
import functools

import numpy as np
import jax
import jax.numpy as jnp
from jax import lax
from jax.experimental import pallas as pl
from jax.experimental.pallas import tpu as pltpu

F32 = jnp.float32
BF16 = jnp.bfloat16
I32 = jnp.int32

D_MODEL = 1024
CONV_W = 512
RET_W = 512
HEADS = 4
DK = 128
CHUNK = 128
GRID_W = 64
ROPE_FREQS = 32
ROPE_BASE = 10000.0
IN_COLS = 3 * CONV_W + 4 * RET_W
N_EXPERTS = 256
N_GROUPS = 8
GROUP_SIZE = N_EXPERTS // N_GROUPS
TOPK_GROUPS = 4
TOP_K = 8
FF = 256
ROUTED_SCALE = 2.5
EPS = 1e-6

TM = 256
SUB = 512
TT = 128
RT = 512
RB = 2048
VMEM_LIMIT = 56 * 1024 * 1024


def _cparams(n_axes=1, vmem=VMEM_LIMIT):
    return pltpu.CompilerParams(dimension_semantics=("arbitrary",) * n_axes,
                                vmem_limit_bytes=vmem)


def _silu(x):
    return x * jax.nn.sigmoid(x)


def _log_sigmoid(x):
    return jnp.minimum(x, 0.0) - jnp.log1p(jnp.exp(-jnp.abs(x)))


def _rms(x, g):
    return x * lax.rsqrt(jnp.mean(x * x, axis=-1, keepdims=True) + EPS) * g


def _dot(a, b):
    return jnp.dot(a, b, preferred_element_type=F32)


def _mod_body(c_ref, w_ref, b_ref, o_ref):
    s = _silu(c_ref[...]).astype(BF16)
    o_ref[...] = _dot(s, w_ref[...].astype(BF16)) + b_ref[...]


def _modulation(c_rows, w_mod, b_mod):
    n_col = w_mod.shape[1]
    blk = 1536
    return pl.pallas_call(
        _mod_body,
        out_shape=jax.ShapeDtypeStruct((8, n_col), F32),
        grid=(n_col // blk,),
        in_specs=[pl.BlockSpec((8, D_MODEL), lambda i: (0, 0)),
                  pl.BlockSpec((D_MODEL, blk), lambda i: (0, i)),
                  pl.BlockSpec((1, blk), lambda i: (0, i))],
        out_specs=pl.BlockSpec((8, blk), lambda i: (0, i)),
        compiler_params=_cparams(),
        name="mod",
    )(c_rows, w_mod, b_mod)


class _Tiles:
    def __init__(self, n_ctx_seq, n_lat_seq, lat_len):
        self.n_ctx = n_ctx_seq
        self.lat_tiles = lat_len // TM
        self.n_lat_seq = n_lat_seq
        self.n_tiles = n_ctx_seq + n_lat_seq * self.lat_tiles

    def is_ctx(self, i):
        return i < self.n_ctx

    def lat_pos(self, i):
        j = jnp.maximum(i - self.n_ctx, 0)
        return j // self.lat_tiles, j % self.lat_tiles

    def phys_reversed(self, i):
        b, t = self.lat_pos(i)
        return jnp.where(i < self.n_ctx, i, self.n_ctx + b * self.lat_tiles + (self.lat_tiles - 1 - t))

    def mod_row(self, i):
        b, _ = self.lat_pos(i)
        return jnp.where(i < self.n_ctx, 0, 1 + b)


def _rope(x, cos, sin_signed):
    lane = lax.broadcasted_iota(I32, x.shape, 1)
    partner = jnp.where((lane & 63) < 32, pltpu.roll(x, 96, 1), pltpu.roll(x, 32, 1))
    return x * cos + partner * sin_signed


def _mix_a_body(tiles, xp_ref, xs_ref, mod_ref, gpre_ref, win_ref, cw_ref, cb_ref, dec_ref,
                cos_ref, sin_ref, s0b_ref,
                yconv_ref, q_ref, v_ref, g_ref, kt_ref, sbin_ref, sbfin_ref,
                sb_scr, tab_scr):
    i = pl.program_id(0)
    is_ctx = tiles.is_ctx(i)
    _, t_rev = tiles.lat_pos(i)
    first = jnp.logical_or(is_ctx, t_rev == 0)

    @pl.when(i == 0)
    def _():
        lg = _log_sigmoid(dec_ref[1])
        col = lax.broadcasted_iota(I32, lg.shape, 2).astype(F32)
        tab_scr[0] = jnp.exp(col * lg)
        tab_scr[1] = jnp.exp(float(CHUNK) * lg)

    @pl.when(first)
    def _():
        sb_scr[...] = jnp.where(is_ctx, 0.0, s0b_ref[0])

    x = jnp.where(is_ctx, xp_ref[...], xs_ref[...])
    h = (_rms(x, gpre_ref[...]) * (1.0 + mod_ref[0, 1:2, :]) + mod_ref[0, 0:1, :]).astype(BF16)

    def proj(k):
        return _dot(h, win_ref[:, k * 512:(k + 1) * 512])

    z = proj(1) * proj(2)
    row = lax.broadcasted_iota(I32, z.shape, 0)
    period = jnp.where(is_ctx, TM, GRID_W)
    pos = row & (period - 1)
    left = jnp.where(pos == 0, 0.0, pltpu.roll(z, 1, 0))
    right = jnp.where(pos == period - 1, 0.0, pltpu.roll(z, TM - 1, 0))
    zc = left * cw_ref[0:1, :] + z * cw_ref[1:2, :] + right * cw_ref[2:3, :] + cb_ref[...]
    yconv_ref[...] = (proj(0) * zc).astype(BF16)

    cos = cos_ref[0]
    sin = sin_ref[0]
    q = proj(3)
    k = proj(4)
    q = jnp.concatenate([_rope(q[:, hh * DK:(hh + 1) * DK], cos, sin) for hh in range(HEADS)], axis=1)
    k = jnp.concatenate([_rope(k[:, hh * DK:(hh + 1) * DK], cos, sin) for hh in range(HEADS)], axis=1)
    q_ref[...] = (q * (DK ** -0.5)).astype(BF16)
    kt = k.T
    kt_ref[...] = kt.astype(BF16)
    v = proj(5).astype(BF16)
    v_ref[...] = v
    g_ref[...] = proj(6)

    for c in (1, 0):
        for hh in range(HEADS):
            sbin_ref[c, hh] = sb_scr[hh].astype(BF16)
            kts = (kt[hh * DK:(hh + 1) * DK, c * CHUNK:(c + 1) * CHUNK] * tab_scr[0, hh]).astype(BF16)
            vc = v[c * CHUNK:(c + 1) * CHUNK, hh * DK:(hh + 1) * DK]
            sb_scr[hh] = sb_scr[hh] * tab_scr[1, hh] + _dot(kts, vc)

    @pl.when(is_ctx)
    def _():
        sbfin_ref[0] = sb_scr[...]


def _mix_b_body(tiles, xp_ref, xs_ref, mod_ref, q_ref, kt_ref, v_ref, g_ref, yconv_ref, sbin_ref,
                wout_ref, gpost_ref, gffn_ref, gn_ref, dec_ref, s0f_ref,
                xnew_ref, tok_ref, toktiles_ref, sffin_ref,
                sf_scr, tab_scr, ycat_scr):
    i = pl.program_id(0)
    is_ctx = tiles.is_ctx(i)
    _, t_pos = tiles.lat_pos(i)
    first = jnp.logical_or(is_ctx, t_pos == 0)

    @pl.when(i == 0)
    def _():
        lgf = _log_sigmoid(dec_ref[0])
        lgb = _log_sigmoid(dec_ref[1])
        row = lax.broadcasted_iota(I32, lgf.shape, 1)
        col = lax.broadcasted_iota(I32, lgf.shape, 2)
        d = (row - col).astype(F32)
        tab_scr[0] = (jnp.where(row >= col, jnp.exp(jnp.where(row >= col, d, 0.0) * lgf), 0.0)
                      + jnp.where(col >= row, jnp.exp(jnp.where(col >= row, -d, 0.0) * lgb), 0.0))
        tab_scr[1] = jnp.exp((row + 1).astype(F32) * lgf)
        tab_scr[2] = jnp.exp((CHUNK - row).astype(F32) * lgb)
        tab_scr[3] = jnp.exp((CHUNK - 1 - col).astype(F32) * lgf)
        tab_scr[4] = jnp.exp(float(CHUNK) * lgf)

    @pl.when(first)
    def _():
        sf_scr[...] = jnp.where(is_ctx, 0.0, s0f_ref[0])

    for c in range(TM // CHUNK):
        rows = slice(c * CHUNK, (c + 1) * CHUNK)
        for hh in range(HEADS):
            cols = slice(hh * DK, (hh + 1) * DK)
            qc = q_ref[rows, cols]
            ktc = kt_ref[cols, rows]
            vc = v_ref[rows, cols]
            att = (_dot(qc, ktc) * tab_scr[0, hh]).astype(BF16)
            o = (_dot(att, vc)
                 + tab_scr[1, hh] * _dot(qc, sf_scr[hh].astype(BF16))
                 + tab_scr[2, hh] * _dot(qc, sbin_ref[c, hh]))
            kts = (ktc.astype(F32) * tab_scr[3, hh]).astype(BF16)
            sf_scr[hh] = sf_scr[hh] * tab_scr[4, hh] + _dot(kts, vc)
            mu = jnp.mean(o, axis=-1, keepdims=True)
            dev = o - mu
            var = jnp.mean(dev * dev, axis=-1, keepdims=True)
            on = dev * lax.rsqrt(var + EPS) * gn_ref[:, cols]
            ycat_scr[rows, RET_W + hh * DK:RET_W + (hh + 1) * DK] = (_silu(g_ref[rows, cols]) * on).astype(BF16)
    ycat_scr[:, 0:CONV_W] = yconv_ref[...]

    @pl.when(is_ctx)
    def _():
        sffin_ref[0] = sf_scr[...]

    x = jnp.where(is_ctx, xp_ref[...], xs_ref[...])
    u = _dot(ycat_scr[...], wout_ref[...])
    xn = x + mod_ref[0, 2:3, :] * _rms(u, gpost_ref[...])
    xnew_ref[...] = xn
    tok = _rms(xn, gffn_ref[...]) * (1.0 + mod_ref[0, 4:5, :]) + mod_ref[0, 3:4, :]
    tok_ref[...] = tok.astype(BF16)
    toktiles_ref[...] = _halftiles_from_rows(_pack_rows(tok))


def _token_mixer(tiles, xp2, xs2, mod3, g_pre, win_bf, conv_w, conv_b, dec, cos_t, sin_t,
                 s0f, s0b, wout_bf, g_post, g_ffn, gn_g):
    n_tok = tiles.n_tiles * TM
    n_ctx = tiles.n_ctx
    last_ctx = n_ctx - 1

    def full(shape):
        return pl.BlockSpec(shape, lambda i: (0,) * len(shape))

    def xp_spec(phys):
        return pl.BlockSpec((TM, D_MODEL), lambda i: (jnp.minimum(phys(i), last_ctx), 0))

    def xs_spec(phys):
        return pl.BlockSpec((TM, D_MODEL), lambda i: (jnp.maximum(phys(i) - n_ctx, 0), 0))

    mod_spec = pl.BlockSpec((1, 6, D_MODEL), lambda i: (tiles.mod_row(i), 0, 0))
    state_in = pl.BlockSpec((1, HEADS, DK, DK), lambda i: (tiles.lat_pos(i)[0], 0, 0, 0))
    state_out = pl.BlockSpec((1, HEADS, DK, DK), lambda i: (jnp.minimum(i, last_ctx), 0, 0, 0))

    rev = tiles.phys_reversed

    def rope_idx(i):
        _, t = tiles.lat_pos(i)
        return jnp.where(i < n_ctx, 0, 1 + (tiles.lat_tiles - 1 - t))

    rope_spec = pl.BlockSpec((1, TM, DK), lambda i: (rope_idx(i), 0, 0))

    def rows(width, phys):
        return pl.BlockSpec((TM, width), lambda i: (phys(i), 0))

    yconv, q, v, g, kt, sbin, sb_fin = pl.pallas_call(
        functools.partial(_mix_a_body, tiles),
        out_shape=(jax.ShapeDtypeStruct((n_tok, CONV_W), BF16),
                   jax.ShapeDtypeStruct((n_tok, RET_W), BF16),
                   jax.ShapeDtypeStruct((n_tok, RET_W), BF16),
                   jax.ShapeDtypeStruct((n_tok, RET_W), F32),
                   jax.ShapeDtypeStruct((RET_W, n_tok), BF16),
                   jax.ShapeDtypeStruct((n_tok // CHUNK, HEADS, DK, DK), BF16),
                   jax.ShapeDtypeStruct((n_ctx, HEADS, DK, DK), F32)),
        grid=(tiles.n_tiles,),
        in_specs=[xp_spec(rev), xs_spec(rev), mod_spec, full((1, D_MODEL)), full((D_MODEL, IN_COLS)),
                  full((3, CONV_W)), full((1, CONV_W)), full((2, HEADS, DK, DK)),
                  rope_spec, rope_spec, state_in],
        out_specs=(rows(CONV_W, rev), rows(RET_W, rev), rows(RET_W, rev), rows(RET_W, rev),
                   pl.BlockSpec((RET_W, TM), lambda i: (0, rev(i))),
                   pl.BlockSpec((TM // CHUNK, HEADS, DK, DK), lambda i: (rev(i), 0, 0, 0)),
                   state_out),
        scratch_shapes=[pltpu.VMEM((HEADS, DK, DK), F32), pltpu.VMEM((2, HEADS, DK, DK), F32)],
        compiler_params=_cparams(),
        name="mix_a",
    )(xp2, xs2, mod3, g_pre, win_bf, conv_w, conv_b, dec, cos_t, sin_t, s0b)

    ident = lambda i: i
    xnew, tok, tok_tiles, sf_fin = pl.pallas_call(
        functools.partial(_mix_b_body, tiles),
        out_shape=(jax.ShapeDtypeStruct((n_tok, D_MODEL), F32),
                   jax.ShapeDtypeStruct((n_tok, D_MODEL), BF16),
                   jax.ShapeDtypeStruct((n_tok * HALF_TILE, 128), U32),
                   jax.ShapeDtypeStruct((n_ctx, HEADS, DK, DK), F32)),
        grid=(tiles.n_tiles,),
        in_specs=[xp_spec(ident), xs_spec(ident), mod_spec,
                  rows(RET_W, ident),
                  pl.BlockSpec((RET_W, TM), lambda i: (0, i)),
                  rows(RET_W, ident), rows(RET_W, ident), rows(CONV_W, ident),
                  pl.BlockSpec((TM // CHUNK, HEADS, DK, DK), lambda i: (i, 0, 0, 0)),
                  full((D_MODEL, D_MODEL)), full((1, D_MODEL)), full((1, D_MODEL)), full((1, RET_W)),
                  full((2, HEADS, DK, DK)), state_in],
        out_specs=(rows(D_MODEL, ident), rows(D_MODEL, ident),
                   pl.BlockSpec((TM * HALF_TILE, 128), lambda i: (i, 0)), state_out),
        scratch_shapes=[pltpu.VMEM((HEADS, DK, DK), F32), pltpu.VMEM((5, HEADS, DK, DK), F32),
                        pltpu.VMEM((TM, D_MODEL), BF16)],
        compiler_params=_cparams(),
        name="mix_b",
    )(xp2, xs2, mod3, q, kt, v, g, yconv, sbin, wout_bf, g_post, g_ffn, gn_g, dec, s0f)
    return xnew, tok, tok_tiles, sf_fin, sb_fin


def _route_body(tok_ref, rwt_ref, bias_ref, topi_ref, topw_ref):
    h = tok_ref[...]
    logits = lax.dot_general(rwt_ref[...], h, (((1,), (1,)), ((), ())), preferred_element_type=F32)
    shape3 = (GROUP_SIZE, N_GROUPS, 128)
    member = lax.broadcasted_iota(I32, shape3, 0)
    group = lax.broadcasted_iota(I32, shape3, 1)
    expert = group * GROUP_SIZE + member
    group2 = lax.broadcasted_iota(I32, (N_GROUPS, 128), 0)
    neg = -jnp.inf
    for lb in range(RT // 128):
        scores = jax.nn.sigmoid(logits[:, lb * 128:(lb + 1) * 128]).reshape(shape3)
        biased = scores + bias_ref[...].reshape(shape3)
        m1 = jnp.max(biased, axis=0)
        first = jnp.min(jnp.where(biased == m1, member, GROUP_SIZE), axis=0)
        m2 = jnp.max(jnp.where(member == first, neg, biased), axis=0)
        gs = m1 + m2
        beaten = jnp.zeros(gs.shape, I32)
        for s in range(1, N_GROUPS):
            other = pltpu.roll(gs, s, 0)
            wins = (other > gs) | ((other == gs) & (group2 >= s))
            beaten = beaten + wins.astype(I32)
        keep = beaten < TOPK_GROUPS
        cand = jnp.where(keep, biased, neg)
        idx_rows, w_rows = [], []
        for _ in range(TOP_K):
            best = jnp.max(jnp.max(cand, axis=0), axis=0, keepdims=True)
            pick = jnp.min(jnp.min(jnp.where(cand == best, expert, N_EXPERTS), axis=0), axis=0, keepdims=True)
            hit = expert == pick
            w_rows.append(jnp.sum(jnp.sum(jnp.where(hit, scores, 0.0), axis=0), axis=0, keepdims=True))
            idx_rows.append(pick)
            cand = jnp.where(hit, neg, cand)
        w = jnp.concatenate(w_rows, axis=0)
        topi_ref[:, lb * 128:(lb + 1) * 128] = jnp.concatenate(idx_rows, axis=0)
        topw_ref[:, lb * 128:(lb + 1) * 128] = w / jnp.sum(w, axis=0, keepdims=True) * ROUTED_SCALE


def _route(tok, rwt_bf, bias_b):
    n_tok = tok.shape[0]
    return pl.pallas_call(
        _route_body,
        out_shape=(jax.ShapeDtypeStruct((TOP_K, n_tok), I32), jax.ShapeDtypeStruct((TOP_K, n_tok), F32)),
        grid=(n_tok // RT,),
        in_specs=[pl.BlockSpec((RT, D_MODEL), lambda i: (i, 0)),
                  pl.BlockSpec((N_EXPERTS, D_MODEL), lambda i: (0, 0)),
                  pl.BlockSpec((N_EXPERTS, 128), lambda i: (0, 0))],
        out_specs=(pl.BlockSpec((TOP_K, RT), lambda i: (0, i)), pl.BlockSpec((TOP_K, RT), lambda i: (0, i))),
        compiler_params=_cparams(),
        name="route",
    )(tok, rwt_bf, bias_b)


def _onehot(ids_row):
    e = lax.broadcasted_iota(I32, (N_EXPERTS, 256), 0)
    return e == ids_row


def _rank_body(topi_ref, rank_ref, counts_ref, run_scr):
    i = pl.program_id(0)

    @pl.when(i == 0)
    def _():
        run_scr[...] = jnp.zeros(run_scr.shape, F32)

    a0 = lax.broadcasted_iota(I32, (256, 256), 0)
    a1 = lax.broadcasted_iota(I32, (256, 256), 1)
    upper = (a0 <= a1).astype(BF16)
    ones = jnp.ones((256, 256), BF16)
    for k in range(TOP_K):
        for sb in range(RB // 256):
            lanes = slice(sb * 256, (sb + 1) * 256)
            oh = _onehot(topi_ref[k:k + 1, lanes])
            ohb = oh.astype(BF16)
            seen = _dot(ohb, upper) + run_scr[...]
            r = jnp.sum(jnp.where(oh, seen, 0.0), axis=0, keepdims=True) - 1.0
            rank_ref[k:k + 1, lanes] = r.astype(I32)
            run_scr[...] = run_scr[...] + _dot(ohb, ones)

    @pl.when(i == pl.num_programs(0) - 1)
    def _():
        counts_ref[...] = run_scr[:, 0:128]


def _dest_body(topi_ref, rank_ref, counts_ref, dest_ref, start_ref, start_scr):
    i = pl.program_id(0)

    @pl.when(i == 0)
    def _():
        c = counts_ref[...]
        d2 = jnp.floor(c / 16384.0)
        rem = c - d2 * 16384.0
        d1 = jnp.floor(rem / 128.0)
        d0 = rem - d1 * 128.0
        e0 = lax.broadcasted_iota(I32, (N_EXPERTS, N_EXPERTS), 0)
        e1 = lax.broadcasted_iota(I32, (N_EXPERTS, N_EXPERTS), 1)
        below = (e1 < e0).astype(BF16)
        start_scr[...] = (16384.0 * _dot(below, d2.astype(BF16)) + 128.0 * _dot(below, d1.astype(BF16))
                          + _dot(below, d0.astype(BF16)))
        start_ref[...] = start_scr[...]

    start = jnp.concatenate([start_scr[...], start_scr[...]], axis=1)
    for k in range(TOP_K):
        for sb in range(RB // 256):
            lanes = slice(sb * 256, (sb + 1) * 256)
            oh = _onehot(topi_ref[k:k + 1, lanes])
            base = jnp.sum(jnp.where(oh, start, 0.0), axis=0, keepdims=True)
            dest_ref[k:k + 1, lanes] = base.astype(I32) + rank_ref[k:k + 1, lanes]


def _dispatch_plan(topi):
    n_tok = topi.shape[1]
    blk = pl.BlockSpec((TOP_K, RB), lambda i: (0, i))
    whole = pl.BlockSpec((N_EXPERTS, 128), lambda i: (0, 0))
    rank, counts = pl.pallas_call(
        _rank_body,
        out_shape=(jax.ShapeDtypeStruct((TOP_K, n_tok), I32), jax.ShapeDtypeStruct((N_EXPERTS, 128), F32)),
        grid=(n_tok // RB,),
        in_specs=[blk],
        out_specs=(blk, whole),
        scratch_shapes=[pltpu.VMEM((N_EXPERTS, 256), F32)],
        compiler_params=_cparams(),
        name="rank",
    )(topi)
    dest, start = pl.pallas_call(
        _dest_body,
        out_shape=(jax.ShapeDtypeStruct((TOP_K, n_tok), I32), jax.ShapeDtypeStruct((N_EXPERTS, 128), F32)),
        grid=(n_tok // RB,),
        in_specs=[blk, blk, whole],
        out_specs=(blk, whole),
        scratch_shapes=[pltpu.VMEM((N_EXPERTS, 128), F32)],
        compiler_params=_cparams(),
        name="dest",
    )(topi, rank, counts)
    return dest, start, counts


U32 = jnp.uint32
HALF_TILE = 4
HIGH_HALF = 0xFFFF0000


def _transpose8(vs):
    sub = lax.broadcasted_iota(I32, (8, 128), 0)
    for d in (4, 2, 1):
        keep = (sub & d) == 0
        out = list(vs)
        for i in range(8):
            if i & d == 0:
                a, b = vs[i], vs[i + d]
                out[i] = jnp.where(keep, a, pltpu.roll(b, d, 0))
                out[i + d] = jnp.where(keep, pltpu.roll(a, 8 - d, 0), b)
        vs = out
    return vs


def _pack_rows(x):
    half = D_MODEL // 2
    lo = pltpu.bitcast(x[:, :half].astype(BF16).astype(F32), U32)
    hi = pltpu.bitcast(x[:, half:].astype(BF16).astype(F32), U32)
    return (lo >> 16) | (hi & jnp.uint32(HIGH_HALF))


def _unpack_rows(u):
    lo = pltpu.bitcast(u << 16, F32)
    hi = pltpu.bitcast(u & jnp.uint32(HIGH_HALF), F32)
    return jnp.concatenate([lo, hi], axis=1).astype(BF16)


def _halftiles_from_rows(u):
    pieces = []
    for q in range(u.shape[0] // 16):
        pieces += _transpose8([u[q * 16 + h * 8:q * 16 + h * 8 + 8, c * 128:(c + 1) * 128]
                               for h in range(2) for c in range(HALF_TILE)])
    return jnp.concatenate(pieces, axis=0)


def _halftile_row(r):
    q, j = divmod(r, 16)
    return (q * 8 + j % 8) * 8 + (j // 8) * HALF_TILE


def _rows_from_halftiles(t):
    blocks = []
    for q in range(t.shape[0] // 64):
        outs = _transpose8([t[(q * 8 + p) * 8:(q * 8 + p + 1) * 8] for p in range(8)])
        for h in range(2):
            blocks.append(jnp.concatenate(outs[h * HALF_TILE:(h + 1) * HALF_TILE], axis=1))
    return jnp.concatenate(blocks, axis=0)


def _pair_position(r):
    q, j = divmod(r, 16)
    return q * 16 + (2 * j if j < 8 else 2 * (j - 8) + 1)


def _start_row_gather(src_hbm, idx_of, dst, sem, n_rows):
    for r in range(n_rows):
        src_row = pl.multiple_of(idx_of(r) * HALF_TILE, HALF_TILE)
        pltpu.make_async_copy(src_hbm.at[pl.ds(src_row, HALF_TILE)],
                              dst.at[pl.ds(_pair_position(r) * HALF_TILE, HALF_TILE)], sem).start(priority=r % 2)


DISPATCH_ROWS = 128
DISPATCH_DEPTH = 4


def _dispatch_body(n_tok, dest_ref, tok_hbm, xs_hbm, tbuf, zbuf, sem, fsem, zsem):
    i = pl.program_id(0)
    k = pl.program_id(1)
    n_blk = pl.num_programs(0)
    step = i * TOP_K + k
    n_steps = n_blk * TOP_K
    unit = tbuf.shape[1] // DISPATCH_ROWS
    blk_rows = DISPATCH_ROWS * unit
    buf = i & 1

    def fetch(blk, b):
        return pltpu.make_async_copy(tok_hbm.at[pl.ds(pl.multiple_of(blk * blk_rows, blk_rows), blk_rows)],
                                     tbuf.at[b], fsem.at[b])

    def batch_wait(b):
        pltpu.make_async_copy(tbuf.at[b], xs_hbm.at[pl.ds(0, blk_rows)], sem.at[b]).wait()

    @pl.when(step == 0)
    def _():
        fetch(0, 0).start()
        zbuf[...] = jnp.zeros(zbuf.shape, zbuf.dtype)
        tail = pltpu.make_async_copy(zbuf, xs_hbm.at[pl.ds(xs_hbm.shape[0] - zbuf.shape[0], zbuf.shape[0])], zsem)
        tail.start()
        tail.wait()

    @pl.when(k == 0)
    def _():
        fetch(i, buf).wait()

    @pl.when(jnp.logical_and(i > 0, k < DISPATCH_DEPTH))
    def _():
        for _ in range(TOP_K // DISPATCH_DEPTH):
            batch_wait(1 - buf)

    @pl.when(jnp.logical_and(k == DISPATCH_DEPTH, i + 1 < n_blk))
    def _():
        fetch(i + 1, 1 - buf).start()

    a0 = k * n_tok + i * DISPATCH_ROWS

    def scatter(b):
        for r in range(DISPATCH_ROWS):
            dst_row = pl.multiple_of(dest_ref[a0 + r] * unit, unit)
            pltpu.make_async_copy(tbuf.at[b, pl.ds(_halftile_row(r), unit)], xs_hbm.at[pl.ds(dst_row, unit)],
                                  sem.at[b]).start(priority=r % 2)

    for b in range(2):
        pl.when(buf == b)(functools.partial(scatter, b))

    @pl.when(step == n_steps - 1)
    def _():
        for _ in range(TOP_K):
            batch_wait(buf)


def _dispatch(dest_flat, tok_rows, n_tok, n_rows_out):
    assert DISPATCH_DEPTH < TOP_K and TOP_K % DISPATCH_DEPTH == 0
    unit = tok_rows.shape[0] // n_tok
    width = tok_rows.shape[1]
    assert unit == HALF_TILE and DISPATCH_ROWS % 16 == 0
    dtype = tok_rows.dtype
    return pl.pallas_call(
        functools.partial(_dispatch_body, n_tok),
        out_shape=jax.ShapeDtypeStruct((n_rows_out * unit, width), dtype),
        grid_spec=pltpu.PrefetchScalarGridSpec(
            num_scalar_prefetch=1,
            grid=(n_tok // DISPATCH_ROWS, TOP_K),
            in_specs=[pl.BlockSpec(memory_space=pl.ANY)],
            out_specs=pl.BlockSpec(memory_space=pl.ANY),
            scratch_shapes=[pltpu.VMEM((2, DISPATCH_ROWS * unit, width), dtype),
                            pltpu.VMEM((SUB * unit, width), dtype),
                            pltpu.SemaphoreType.DMA((2,)), pltpu.SemaphoreType.DMA((2,)),
                            pltpu.SemaphoreType.DMA(())]),
        compiler_params=_cparams(2),
        name="dispatch",
    )(dest_flat, tok_rows)


N_XBUF = 4
LOOKAHEAD = 3
W_AHEAD = 2
WEIGHT_PRIORITY = 1


def _experts_body(start_ref, nsub_ref,
                  xs_hbm, wg_hbm, wu_hbm, wd_hbm, y_hbm,
                  xbuf, ybuf0, ybuf1,
                  wg_f32, wu_f32, wd_f32, wg_bf, wu_bf, wd_bf, cur, nxt, gsem, osem, wsem):
    e = pl.program_id(0)
    n_e = pl.num_programs(0)
    nsub = nsub_ref[e]
    sub_rows = SUB * HALF_TILE
    overflow_row = y_hbm.shape[0] - sub_rows

    def weight_copies(ex, slot):
        return [pltpu.make_async_copy(src.at[ex], dst.at[slot], wsem.at[slot, n])
                for n, (src, dst) in enumerate(((wg_hbm, wg_f32), (wu_hbm, wu_f32), (wd_hbm, wd_f32)))]

    def window_row(ex, j, rows_per_slot):
        return pl.multiple_of((start_ref[ex] + j * SUB) * rows_per_slot, rows_per_slot)


    def produce():
        pe = cur[0]
        pj = cur[1]
        pg = cur[2]
        live = pe < n_e
        pe_c = jnp.minimum(pe, n_e - 1)
        row = pl.multiple_of(jnp.where(live, window_row(pe_c, pj, HALF_TILE), 0), HALF_TILE)
        slot = lax.rem(pg, N_XBUF)
        pltpu.make_async_copy(xs_hbm.at[pl.ds(row, sub_rows)], xbuf.at[slot], gsem.at[slot]).start()
        last = pj + 1 >= nsub_ref[pe_c]
        cur[0] = jnp.where(jnp.logical_and(live, last), nxt[pe_c], pe)
        cur[1] = jnp.where(last, 0, pj + 1)
        cur[2] = pg + 1

    def fetch_wait(slot):
        pltpu.make_async_copy(xs_hbm.at[pl.ds(0, sub_rows)], xbuf.at[slot], gsem.at[slot]).wait()

    ybuf = (ybuf0, ybuf1)

    def out_wait():
        pltpu.make_async_copy(ybuf0, y_hbm.at[pl.ds(0, sub_rows)], osem).wait()

    @pl.when(e == 0)
    def _():
        def fill(i, following):
            x = N_EXPERTS - 1 - i
            nxt[x] = following
            return jnp.where(nsub_ref[x] > 0, x, following)

        cur[0] = lax.fori_loop(0, N_EXPERTS, fill, N_EXPERTS)
        cur[1] = 0
        cur[2] = 0
        cur[3] = 0
        for ahead in range(W_AHEAD):
            for cp in weight_copies(ahead, ahead):
                cp.start(priority=WEIGHT_PRIORITY)
        for _ in range(LOOKAHEAD):
            produce()
        ybuf1[...] = jnp.zeros(ybuf1.shape, U32)
        pltpu.make_async_copy(ybuf1, y_hbm.at[pl.ds(overflow_row, sub_rows)], osem).start()

    wslot = lax.rem(e, W_AHEAD + 1)

    @pl.when(e + W_AHEAD < n_e)
    def _():
        for cp in weight_copies(e + W_AHEAD, lax.rem(e + W_AHEAD, W_AHEAD + 1)):
            cp.start(priority=WEIGHT_PRIORITY)

    for cp in weight_copies(e, wslot):
        cp.wait()
    wg_bf[...] = wg_f32[wslot].astype(BF16)
    wu_bf[...] = wu_f32[wslot].astype(BF16)
    wd_bf[...] = wd_f32[wslot].astype(BF16)

    def step(j, carry):
        g = cur[3]
        produce()
        slot = lax.rem(g, N_XBUF)
        fetch_wait(slot)
        xb = _unpack_rows(_rows_from_halftiles(xbuf[slot]))
        a = _dot(xb, wg_bf[...])
        b = _dot(xb, wu_bf[...])
        y_tiles = _halftiles_from_rows(_pack_rows(_dot((_silu(a) * b).astype(BF16), wd_bf[...])))
        row = window_row(e, j, HALF_TILE)
        for parity in range(2):
            @pl.when((g & 1) == parity)
            def _():
                ybuf[parity][...] = y_tiles
                out_wait()
                pltpu.make_async_copy(ybuf[parity], y_hbm.at[pl.ds(row, sub_rows)], osem).start()
        cur[3] = g + 1
        return carry

    lax.fori_loop(0, nsub, step, 0)

    @pl.when(e == n_e - 1)
    def _():
        total = cur[3]
        out_wait()
        for ahead in range(LOOKAHEAD):
            fetch_wait(lax.rem(total + ahead, N_XBUF))
        ybuf0[...] = jnp.zeros(ybuf0.shape, U32)
        tail = pltpu.make_async_copy(ybuf0, y_hbm.at[pl.ds(overflow_row, sub_rows)], osem)
        tail.start()
        tail.wait()


def _experts(start, nsub, xs_tiles, wg, wu, wd):
    sub_rows = SUB * HALF_TILE
    return pl.pallas_call(
        _experts_body,
        out_shape=jax.ShapeDtypeStruct(xs_tiles.shape, U32),
        grid_spec=pltpu.PrefetchScalarGridSpec(
            num_scalar_prefetch=2,
            grid=(N_EXPERTS,),
            in_specs=[pl.BlockSpec(memory_space=pl.ANY)] * 4,
            out_specs=pl.BlockSpec(memory_space=pl.ANY),
            scratch_shapes=[pltpu.VMEM((N_XBUF, sub_rows, 128), U32),
                            pltpu.VMEM((sub_rows, 128), U32), pltpu.VMEM((sub_rows, 128), U32),
                            pltpu.VMEM((W_AHEAD + 1, D_MODEL, FF), F32), pltpu.VMEM((W_AHEAD + 1, D_MODEL, FF), F32),
                            pltpu.VMEM((W_AHEAD + 1, FF, D_MODEL), F32),
                            pltpu.VMEM((D_MODEL, FF), BF16), pltpu.VMEM((D_MODEL, FF), BF16),
                            pltpu.VMEM((FF, D_MODEL), BF16), pltpu.SMEM((4,), I32), pltpu.SMEM((N_EXPERTS,), I32),
                            pltpu.SemaphoreType.DMA((N_XBUF,)), pltpu.SemaphoreType.DMA(()),
                            pltpu.SemaphoreType.DMA((W_AHEAD + 1, 3))]),
        compiler_params=_cparams(),
        name="experts",
    )(start, nsub, xs_tiles, wg, wu, wd)


def _combine_body(n_tok, n_ctx_tiles, dest_ref,
                  y_hbm, topw_ref, tok_ref, xnew_ref, mod_ref, sg_ref, su_ref, sd_ref, gpost_ref,
                  outp_ref, outs_ref, gbuf, wcol, fbuf, gsem):
    i = pl.program_id(0)
    n = pl.num_programs(0)

    def gather(tile, slot):
        def per_choice(k, carry):
            base = k * n_tok + tile * TT
            _start_row_gather(y_hbm, lambda r: dest_ref[base + r], gbuf.at[slot, k], gsem.at[slot], TT)
            return carry
        lax.fori_loop(0, TOP_K, per_choice, 0)

    def gather_wait(slot):
        for k in range(TOP_K):
            pltpu.make_async_copy(y_hbm.at[pl.ds(0, TT * HALF_TILE)], gbuf.at[slot, k], gsem.at[slot]).wait()

    slot = i & 1

    @pl.when(i == 0)
    def _():
        gather(0, 0)

    def tile_step(cur_slot):
        nxt_slot = 1 - cur_slot
        nxt_tile = jnp.minimum(i + 1, n - 1)
        for k in range(TOP_K):
            base = k * n_tok + nxt_tile * TT
            _start_row_gather(y_hbm, lambda r: dest_ref[base + r], gbuf.at[nxt_slot, k], gsem.at[nxt_slot], TT)

        w_t = jnp.concatenate([topw_ref[...], jnp.zeros((128 - TOP_K, TT), F32)], axis=0).T
        for k in range(TOP_K):
            wcol[k] = jnp.broadcast_to(w_t[:, k:k + 1], (TT, 128))
        h = tok_ref[...]
        fbuf[...] = _dot((_silu(_dot(h, sg_ref[...])) * _dot(h, su_ref[...])).astype(BF16), sd_ref[...])

        gather_wait(cur_slot)
        first_half = lax.broadcasted_iota(I32, (8, 128), 0) < HALF_TILE
        for q in range(TT // 16):
            acc_lo, acc_hi = [], []
            for v in range(8):
                t0 = q * 16 + v
                lo_sum = hi_sum = None
                for k in range(TOP_K):
                    u = gbuf[cur_slot, k, (q * 8 + v) * 8:(q * 8 + v + 1) * 8, :]
                    w = jnp.where(first_half, jnp.broadcast_to(wcol[k, t0:t0 + 1, :], (8, 128)),
                                  jnp.broadcast_to(wcol[k, t0 + 8:t0 + 9, :], (8, 128)))
                    lo = pltpu.bitcast(u << 16, F32) * w
                    hi = pltpu.bitcast(u & jnp.uint32(HIGH_HALF), F32) * w
                    lo_sum = lo if lo_sum is None else lo_sum + lo
                    hi_sum = hi if hi_sum is None else hi_sum + hi
                acc_lo.append(lo_sum)
                acc_hi.append(hi_sum)
            rows_lo = _transpose8(acc_lo)
            rows_hi = _transpose8(acc_hi)
            for h in range(2):
                rows = slice(q * 16 + h * 8, q * 16 + h * 8 + 8)
                routed = jnp.concatenate(rows_lo[h * HALF_TILE:(h + 1) * HALF_TILE]
                                         + rows_hi[h * HALF_TILE:(h + 1) * HALF_TILE], axis=1)
                f = fbuf[rows, :] + routed
                fbuf[rows, :] = xnew_ref[rows, :] + mod_ref[0, 5:6, :] * _rms(f, gpost_ref[...])

    for s in range(2):
        pl.when(slot == s)(functools.partial(tile_step, s))

    @pl.when(i == n - 1)
    def _():
        gather_wait(1 - slot)

    @pl.when(i < n_ctx_tiles)
    def _():
        outp_ref[...] = fbuf[...]

    @pl.when(i >= n_ctx_tiles)
    def _():
        outs_ref[...] = fbuf[...]


def _combine(tiles, dest_flat, ysorted, topw, tok, xnew, mod3, sg_bf, su_bf, sd_bf, g_post):
    n_tok = tok.shape[0]
    n_ctx_tok = tiles.n_ctx * TM
    n_ctx_tiles = n_ctx_tok // TT
    lat_tiles_per_seq = tiles.lat_tiles * TM // TT

    def mod_row(i):
        return jnp.where(i < n_ctx_tiles, 0, 1 + jnp.maximum(i - n_ctx_tiles, 0) // lat_tiles_per_seq)

    def full(shape):
        return pl.BlockSpec(shape, lambda i, *_: (0,) * len(shape))

    rows = pl.BlockSpec((TT, D_MODEL), lambda i, *_: (i, 0))
    return pl.pallas_call(
        functools.partial(_combine_body, n_tok, n_ctx_tiles),
        out_shape=(jax.ShapeDtypeStruct((n_ctx_tok, D_MODEL), F32),
                   jax.ShapeDtypeStruct((n_tok - n_ctx_tok, D_MODEL), F32)),
        grid_spec=pltpu.PrefetchScalarGridSpec(
            num_scalar_prefetch=1,
            grid=(n_tok // TT,),
            in_specs=[pl.BlockSpec(memory_space=pl.ANY),
                      pl.BlockSpec((TOP_K, TT), lambda i, *_: (0, i)),
                      rows, rows,
                      pl.BlockSpec((1, 6, D_MODEL), lambda i, *_: (mod_row(i), 0, 0)),
                      full((D_MODEL, FF)), full((D_MODEL, FF)), full((FF, D_MODEL)), full((1, D_MODEL))],
            out_specs=(pl.BlockSpec((TT, D_MODEL), lambda i, *_: (jnp.minimum(i, n_ctx_tiles - 1), 0)),
                       pl.BlockSpec((TT, D_MODEL), lambda i, *_: (jnp.maximum(i - n_ctx_tiles, 0), 0))),
            scratch_shapes=[pltpu.VMEM((2, TOP_K, TT * HALF_TILE, 128), U32), pltpu.VMEM((TOP_K, TT, 128), F32),
                            pltpu.VMEM((TT, D_MODEL), F32), pltpu.SemaphoreType.DMA((2,))]),
        compiler_params=_cparams(),
        name="combine",
    )(dest_flat, ysorted, topw, tok, xnew, mod3, sg_bf, su_bf, sd_bf, g_post)


def _rope_tables(lat_len, lat_tiles):
    f32 = np.float32
    rows = lat_len // GRID_W
    row = np.repeat(np.arange(rows, dtype=f32), GRID_W)
    col = np.tile(np.arange(GRID_W, dtype=f32), rows)
    inv = np.power(f32(ROPE_BASE), -np.arange(ROPE_FREQS, dtype=f32) / f32(ROPE_FREQS)).astype(f32)
    ang = np.concatenate([row[:, None] * inv[None, :]] * 2 + [col[:, None] * inv[None, :]] * 2, axis=1).astype(f32)
    sign = np.tile(np.concatenate([-np.ones((ROPE_FREQS,), f32), np.ones((ROPE_FREQS,), f32)]), 2)
    cos = np.cos(ang).astype(f32).reshape(lat_tiles, TM, DK)
    sin = (np.sin(ang).astype(f32) * sign[None, :]).reshape(lat_tiles, TM, DK)
    cos = np.concatenate([np.ones((1, TM, DK), f32), cos], axis=0)
    sin = np.concatenate([np.zeros((1, TM, DK), f32), sin], axis=0)
    return jnp.asarray(cos), jnp.asarray(sin)


def kernel(x_prompt, x_sample, state_ret_fwd, state_ret_bwd, c, c_ctx, w_mod, b_mod, norm_mix_pre,
           norm_mix_post, norm_ffn_pre, norm_ffn_post, w_in, conv_w, conv_b, ret_decay_fwd,
           ret_decay_bwd, ret_gn_g, w_out, router_w, router_bias, expert_w_gate, expert_w_up,
           expert_w_down, shared_w_gate, shared_w_up, shared_w_down):
    bp, tp, d = x_prompt.shape
    bs, ts, _ = x_sample.shape
    depth = w_mod.shape[0]
    assert d == D_MODEL and tp == TM and ts % TM == 0 and bs + 1 <= 8
    tiles = _Tiles(bp, bs, ts)
    n_tok = tiles.n_tiles * TM
    n_slots = n_tok * TOP_K + SUB
    cos_t, sin_t = _rope_tables(ts, tiles.lat_tiles)

    perm = (jnp.arange(N_EXPERTS) % N_GROUPS) * GROUP_SIZE + jnp.arange(N_EXPERTS) // N_GROUPS

    xp2 = x_prompt.reshape(bp * tp, d)
    xs2 = x_sample.reshape(bs * ts, d)
    new_f, new_b = [], []
    for l in range(depth):
        c_rows = jnp.concatenate([c_ctx[None, :], c, jnp.zeros((8 - 1 - bs, d), F32)], axis=0)
        mod3 = _modulation(c_rows, w_mod[l], b_mod[l][None, :]).reshape(8, 6, d)
        dec = jnp.broadcast_to(jnp.stack([ret_decay_fwd[l], ret_decay_bwd[l]])[:, :, None, None],
                               (2, HEADS, DK, DK)).astype(F32)
        xnew, tok, tok_tiles, sf_fin, sb_fin = _token_mixer(
            tiles, xp2, xs2, mod3, norm_mix_pre[l][None, :], w_in[l].astype(BF16), conv_w[l],
            conv_b[l][None, :], dec, cos_t, sin_t, state_ret_fwd[:, l], state_ret_bwd[:, l],
            w_out[l].astype(BF16), norm_mix_post[l][None, :], norm_ffn_pre[l][None, :], ret_gn_g[l][None, :])
        new_f.append(sf_fin)
        new_b.append(sb_fin)

        rwt = router_w[l].T[perm].astype(BF16)
        bias_b = jnp.broadcast_to(router_bias[l][perm][:, None], (N_EXPERTS, 128)).astype(F32)
        topi, topw = _route(tok, rwt, bias_b)
        dest, start, counts = _dispatch_plan(topi)
        dest_flat = dest.reshape(-1)
        start_i = start[:, 0].astype(I32)
        nsub = ((counts[:, 0] + float(SUB - 1)) / float(SUB)).astype(I32)
        xs_tiles = _dispatch(dest_flat, tok_tiles, n_tok, n_slots)
        ysorted = _experts(start_i, nsub, xs_tiles, expert_w_gate[l], expert_w_up[l], expert_w_down[l])
        xp2, xs2 = _combine(tiles, dest_flat, ysorted, topw, tok, xnew, mod3,
                            shared_w_gate[l].astype(BF16), shared_w_up[l].astype(BF16),
                            shared_w_down[l].astype(BF16), norm_ffn_post[l][None, :])

    return (xp2.reshape(bp, tp, d), xs2.reshape(bs, ts, d),
            jnp.stack(new_f, axis=1), jnp.stack(new_b, axis=1))
```
